```python
import math
import jax, jax.numpy as jnp
from jax import lax
import numpy as np

D_MODEL = 2048
BATCH = 2
SEQ = 16384
DEPTH = 2

N_A_LAYERS = DEPTH // 2
N_B_LAYERS = DEPTH - N_A_LAYERS
RMS_EPS = 1e-6
GLA_HEADS = 4
GLA_DK = D_MODEL // (2 * GLA_HEADS)
GLA_DV = D_MODEL // GLA_HEADS
GLA_GATE_RANK = 16
GLA_GATE_NORMALIZER = 16.0
GLA_LOG_GATE_MIN = -1.0
GLA_CHUNK = 64
GLA_PROJ = 2 * GLA_HEADS * GLA_DK + 2 * GLA_HEADS * GLA_DV + GLA_GATE_RANK
SWA_HEAD_DIM = 64
SWA_Q_HEADS = D_MODEL // SWA_HEAD_DIM
SWA_KV_HEADS = SWA_Q_HEADS // 8
SWA_GROUPS = SWA_Q_HEADS // SWA_KV_HEADS
WINDOW = 128
NUM_BUCKETS = 32
REL_MAX_DISTANCE = 128
MOE_GROUPS = 4
MOE_EXPERTS_PER_GROUP = 8
MOE_N_EXPERTS = MOE_GROUPS * MOE_EXPERTS_PER_GROUP
MOE_TOP_K = 2
MOE_D_FF = 512
MOE_TOKEN_BLOCK = 1024

kernel_name = 'yoco_gla_swa_sink_hmoe'


def rms_norm(x, g):
    x32 = x.astype(jnp.float32)
    y = x32 * lax.rsqrt(jnp.mean(x32 * x32, axis=-1, keepdims=True) + RMS_EPS)
    return (y * g.astype(jnp.float32)).astype(x.dtype)


def gla_mixer(x, norm_g, w_in, w_gate_up, b_gate, head_g, w_out):
    bsz, seq, _ = x.shape
    n_chunks = seq // GLA_CHUNK
    h = rms_norm(x, norm_g)
    proj = h @ w_in
    qk = GLA_HEADS * GLA_DK
    vd = GLA_HEADS * GLA_DV
    q, k, v, r, g_lr = jnp.split(proj, [qk, 2 * qk, 2 * qk + vd, 2 * qk + 2 * vd], axis=-1)
    log_a = jax.nn.log_sigmoid((g_lr @ w_gate_up + b_gate).astype(jnp.float32)) / GLA_GATE_NORMALIZER
    log_a = jnp.maximum(log_a, GLA_LOG_GATE_MIN)

    def chunks(t, dh):
        t = t.astype(jnp.float32).reshape(bsz, n_chunks, GLA_CHUNK, GLA_HEADS, dh)
        return t.transpose(0, 3, 1, 2, 4)

    q = chunks(q, GLA_DK) * (GLA_DK ** -0.5)
    k = chunks(k, GLA_DK)
    v = chunks(v, GLA_DV)
    b = jnp.cumsum(chunks(log_a, GLA_DK), axis=3)
    b_last = b[:, :, :, -1:, :]
    q_e = q * jnp.exp(b)
    k_e = k * jnp.exp(-b)
    k_last = k * jnp.exp(b_last - b)
    causal = jnp.tril(jnp.ones((GLA_CHUNK, GLA_CHUNK), dtype=bool))
    att = jnp.where(causal, jnp.einsum('bhnid,bhnjd->bhnij', q_e, k_e), 0.0)
    o_intra = jnp.einsum('bhnij,bhnjv->bhniv', att, v)

    def step(state, inp):
        qe_n, kl_n, v_n, dec_n = inp
        o_n = jnp.einsum('bhcd,bhdv->bhcv', qe_n, state)
        state = dec_n[..., None] * state + jnp.einsum('bhcd,bhcv->bhdv', kl_n, v_n)
        return state, o_n

    xs = (jnp.moveaxis(q_e, 2, 0), jnp.moveaxis(k_last, 2, 0), jnp.moveaxis(v, 2, 0),
          jnp.moveaxis(jnp.exp(b_last[:, :, :, 0, :]), 2, 0))
    s0 = jnp.zeros((bsz, GLA_HEADS, GLA_DK, GLA_DV), jnp.float32)
    _, o_inter = lax.scan(step, s0, xs)
    o = o_intra + jnp.moveaxis(o_inter, 0, 2)
    o = o.transpose(0, 2, 3, 1, 4).reshape(bsz, seq, GLA_HEADS, GLA_DV)
    o = rms_norm(o, head_g) * jax.nn.silu(r.astype(jnp.float32)).reshape(bsz, seq, GLA_HEADS, GLA_DV)
    return o.reshape(bsz, seq, vd).astype(x.dtype) @ w_out


def shared_kv(x, kv_norm, w_kv, k_norm):
    bsz, seq, _ = x.shape
    n_blk = seq // WINDOW
    kv = rms_norm(x, kv_norm) @ w_kv
    k, v = jnp.split(kv, 2, axis=-1)
    k = rms_norm(k.reshape(bsz, seq, SWA_KV_HEADS, SWA_HEAD_DIM), k_norm)
    v = v.reshape(bsz, seq, SWA_KV_HEADS, SWA_HEAD_DIM)

    def banded(t):
        tb = t.reshape(bsz, n_blk, WINDOW, SWA_KV_HEADS, SWA_HEAD_DIM)
        prev = jnp.concatenate([jnp.zeros_like(tb[:, :1]), tb[:, :-1]], axis=1)
        return jnp.moveaxis(jnp.concatenate([prev, tb], axis=2), 1, 0)

    return banded(k), banded(v)


def t5_bucket(dist):
    max_exact = NUM_BUCKETS // 2
    n = np.maximum(dist, 0)
    large = max_exact + (np.log(np.maximum(n, max_exact) / max_exact) / math.log(REL_MAX_DISTANCE / max_exact)
                         * (NUM_BUCKETS - max_exact)).astype(np.int32)
    return np.where(n < max_exact, n, np.minimum(large, NUM_BUCKETS - 1)).astype(np.int32)


def swa_mixer(x, k_blocks, v_blocks, rel_bias, norm_g, w_q, q_norm, sinks, w_out):
    bsz, seq, _ = x.shape
    n_blk = seq // WINDOW
    q = (rms_norm(x, norm_g) @ w_q).reshape(bsz, seq, SWA_Q_HEADS, SWA_HEAD_DIM)
    q = rms_norm(q, q_norm) * (SWA_HEAD_DIM ** -0.5)
    q = jnp.moveaxis(q.reshape(bsz, n_blk, WINDOW, SWA_KV_HEADS, SWA_GROUPS, SWA_HEAD_DIM), 1, 0)
    row = np.arange(WINDOW)[:, None]
    col = np.arange(2 * WINDOW)[None, :]
    dist = WINDOW + row - col
    band = jnp.asarray((dist >= 0) & (dist < WINDOW))
    first_ok = jnp.asarray(np.broadcast_to(col >= WINDOW, dist.shape))
    bias = rel_bias.astype(jnp.float32)[t5_bucket(dist)]
    bias = jnp.transpose(bias, (2, 0, 1)).reshape(SWA_KV_HEADS, SWA_GROUPS, WINDOW, 2 * WINDOW)
    sink = sinks.astype(jnp.float32).reshape(SWA_KV_HEADS, SWA_GROUPS)[None, :, :, None]

    def block(inp):
        q_n, k_n, v_n, n = inp
        s = jnp.einsum('bqhgd,bkhd->bhgqk', q_n.astype(jnp.float32), k_n.astype(jnp.float32)) + bias
        valid = band & (first_ok | (n > 0))
        s = jnp.where(valid, s, -jnp.inf)
        m = jnp.maximum(jnp.max(s, axis=-1), sink)
        p = jnp.exp(s - m[..., None])
        denom = jnp.sum(p, axis=-1) + jnp.exp(sink - m)
        o = jnp.einsum('bhgqk,bkhd->bqhgd', p, v_n.astype(jnp.float32))
        return o / jnp.transpose(denom, (0, 3, 1, 2))[..., None]

    o = lax.map(block, (q, k_blocks, v_blocks, jnp.arange(n_blk)))
    o = jnp.moveaxis(o, 0, 1).reshape(bsz, seq, SWA_Q_HEADS * SWA_HEAD_DIM).astype(x.dtype)
    return o @ w_out


def hier_moe(x, norm_g, w_group, b_group, w_router, b_router, w_gate, w_up, w_down):
    bsz, seq, d = x.shape
    t = bsz * seq
    h = rms_norm(x, norm_g).reshape(t, d)
    g_logits = (h @ w_group).astype(jnp.float32) + b_group.astype(jnp.float32)
    g_probs = jax.nn.softmax(g_logits, axis=-1)
    g_idx = jnp.argmax(g_logits, axis=-1)
    p_group = jnp.take_along_axis(g_probs, g_idx[:, None], axis=-1)
    e_logits = ((h @ w_router).astype(jnp.float32) + b_router.astype(jnp.float32)).reshape(t, MOE_GROUPS, MOE_EXPERTS_PER_GROUP)
    e_in = jnp.take_along_axis(e_logits, g_idx[:, None, None], axis=1)[:, 0]
    top_v, top_i = lax.top_k(e_in, MOE_TOP_K)
    top_w = jax.nn.softmax(top_v, axis=-1) * p_group
    flat = g_idx[:, None] * MOE_EXPERTS_PER_GROUP + top_i
    combine = jnp.sum(jax.nn.one_hot(flat, MOE_N_EXPERTS, dtype=jnp.float32) * top_w[..., None], axis=1)
    blk = math.gcd(t, MOE_TOKEN_BLOCK)

    def block(inp):
        h_b, c_b = inp
        a = jax.nn.silu(jnp.einsum('td,edf->tef', h_b, w_gate)) * jnp.einsum('td,edf->tef', h_b, w_up)
        return jnp.einsum('tef,efd->td', a * c_b[..., None].astype(a.dtype), w_down)

    y = lax.map(block, (h.reshape(t // blk, blk, d), combine.reshape(t // blk, blk, MOE_N_EXPERTS)))
    return y.reshape(bsz, seq, d)


def setup_inputs(seed: int = 0) -> dict:
    key = jax.random.key(seed)
    ks = jax.random.split(key, 24)
    f32 = jnp.float32

    def nrm(k, shape, scale):
        return jax.random.normal(k, shape, f32) * scale

    def gain(k, shape):
        return 1.0 + 0.02 * jax.random.normal(k, shape, f32)

    d = D_MODEL
    hq = SWA_Q_HEADS
    hd = SWA_HEAD_DIM
    ne = MOE_N_EXPERTS
    return {
        'x': nrm(ks[0], (BATCH, SEQ, d), 1.0),
        'gla_norm': gain(ks[1], (N_A_LAYERS, d)),
        'gla_w_in': nrm(ks[2], (N_A_LAYERS, d, GLA_PROJ), d ** -0.5),
        'gla_w_gate_up': nrm(ks[3], (N_A_LAYERS, GLA_GATE_RANK, GLA_HEADS * GLA_DK), GLA_GATE_RANK ** -0.5),
        'gla_b_gate': nrm(ks[4], (N_A_LAYERS, GLA_HEADS * GLA_DK), 0.1),
        'gla_head_norm': gain(ks[5], (N_A_LAYERS, GLA_DV)),
        'gla_w_out': nrm(ks[6], (N_A_LAYERS, GLA_HEADS * GLA_DV, d), (GLA_HEADS * GLA_DV) ** -0.5),
        'kv_norm': gain(ks[7], (d,)),
        'w_kv': nrm(ks[8], (d, 2 * SWA_KV_HEADS * hd), d ** -0.5),
        'k_norm': gain(ks[9], (hd,)),
        'rel_bias': nrm(ks[10], (NUM_BUCKETS, hq), 0.5),
        'swa_norm': gain(ks[11], (N_B_LAYERS, d)),
        'swa_w_q': nrm(ks[12], (N_B_LAYERS, d, hq * hd), d ** -0.5),
        'swa_q_norm': gain(ks[13], (N_B_LAYERS, hd)),
        'swa_sinks': nrm(ks[14], (N_B_LAYERS, hq), 1.0),
        'swa_w_out': nrm(ks[15], (N_B_LAYERS, hq * hd, d), (hq * hd) ** -0.5),
        'moe_norm': gain(ks[16], (DEPTH, d)),
        'moe_w_group': nrm(ks[17], (DEPTH, d, MOE_GROUPS), d ** -0.5),
        'moe_b_group': nrm(ks[18], (DEPTH, MOE_GROUPS), 0.01),
        'moe_w_router': nrm(ks[19], (DEPTH, d, ne), d ** -0.5),
        'moe_b_router': nrm(ks[20], (DEPTH, ne), 0.01),
        'moe_w_gate': nrm(ks[21], (DEPTH, ne, d, MOE_D_FF), d ** -0.5),
        'moe_w_up': nrm(ks[22], (DEPTH, ne, d, MOE_D_FF), d ** -0.5),
        'moe_w_down': nrm(ks[23], (DEPTH, ne, MOE_D_FF, d), MOE_D_FF ** -0.5),
    }


def reference(x, gla_norm, gla_w_in, gla_w_gate_up, gla_b_gate, gla_head_norm, gla_w_out,
              kv_norm, w_kv, k_norm, rel_bias, swa_norm, swa_w_q, swa_q_norm, swa_sinks, swa_w_out,
              moe_norm, moe_w_group, moe_b_group, moe_w_router, moe_b_router, moe_w_gate, moe_w_up, moe_w_down):
    h = x
    k_blocks = None
    v_blocks = None
    for layer in range(DEPTH):
        if layer < N_A_LAYERS:
            a = layer
            h = h + gla_mixer(h, gla_norm[a], gla_w_in[a], gla_w_gate_up[a], gla_b_gate[a],
                              gla_head_norm[a], gla_w_out[a])
        else:
            if layer == N_A_LAYERS:
                k_blocks, v_blocks = shared_kv(h, kv_norm, w_kv, k_norm)
            b = layer - N_A_LAYERS
            h = h + swa_mixer(h, k_blocks, v_blocks, rel_bias, swa_norm[b], swa_w_q[b], swa_q_norm[b],
                              swa_sinks[b], swa_w_out[b])
        h = h + hier_moe(h, moe_norm[layer], moe_w_group[layer], moe_b_group[layer], moe_w_router[layer],
                         moe_b_router[layer], moe_w_gate[layer], moe_w_up[layer], moe_w_down[layer])
    return h
```

```python
import functools
import math

import jax
import jax.numpy as jnp
import numpy as np
from jax import lax
from jax.experimental import pallas as pl
from jax.experimental.pallas import tpu as pltpu

F32 = jnp.float32
BF16 = jnp.bfloat16

RMS_EPS = 1e-6
GLA_HEADS = 4
GLA_GATE_RANK = 16
GLA_GATE_NORMALIZER = 16.0
GLA_LOG_GATE_MIN = -1.0
GLA_CHUNK = 64
SWA_HEAD_DIM = 64
SWA_GROUPS = 8
WINDOW = 128
NUM_BUCKETS = 32
REL_MAX_DISTANCE = 128
MOE_GROUPS = 4
MOE_EXPERTS_PER_GROUP = 8

LANES = 128
VMEM_LIMIT_BYTES = 56 * 1024 * 1024

PREP_ROWS = 256
PROJ_ROWS = 512
ROUTER_ROWS = 512
EXPERT_ROWS = 256
COMBINE_ROWS = 256
SWA_PROJ_ROWS = 256
DMA_UNROLL = 8
INVERSE_STEPS = 64


def _params(*semantics):
    return pltpu.CompilerParams(dimension_semantics=semantics, vmem_limit_bytes=VMEM_LIMIT_BYTES)


def _dot(a, b):
    return jnp.dot(a, b, preferred_element_type=F32)


def _dot_nt(a, b):
    return lax.dot_general(a, b, (((1,), (1,)), ((), ())), preferred_element_type=F32)


def _split(x):
    hi = x.astype(BF16)
    lo = (x - hi.astype(F32)).astype(BF16)
    return hi, lo


def _dot3(a, b, dot=_dot):
    ah, al = _split(a)
    bh, bl = _split(b)
    return dot(ah, bh) + dot(al, bh) + dot(ah, bl)


def _rms_inv(x):
    return lax.rsqrt(jnp.mean(x * x, axis=-1, keepdims=True) + RMS_EPS)


def _sigmoid(x):
    return 1.0 / (1.0 + jnp.exp(-x))


def _log_sigmoid(x):
    return jnp.minimum(x, 0.0) - jnp.log1p(jnp.exp(-jnp.abs(x)))


def _norm_matmul_kernel(x_ref, g_ref, w_ref, o_ref, xn_ref):
    @pl.when(pl.program_id(1) == 0)
    def _():
        x = x_ref[...]
        xn_ref[...] = (x * _rms_inv(x) * g_ref[...]).astype(BF16)

    o_ref[...] = _dot(xn_ref[...], w_ref[...]).astype(o_ref.dtype)


def _norm_matmul(x, g, w, *, rows, cols, out_dtype):
    t, d = x.shape
    n = w.shape[1]
    return pl.pallas_call(
        _norm_matmul_kernel,
        out_shape=jax.ShapeDtypeStruct((t, n), out_dtype),
        grid=(t // rows, n // cols),
        in_specs=[
            pl.BlockSpec((rows, d), lambda i, j: (i, 0)),
            pl.BlockSpec((1, d), lambda i, j: (0, 0)),
            pl.BlockSpec((d, cols), lambda i, j: (0, j)),
        ],
        out_specs=pl.BlockSpec((rows, cols), lambda i, j: (i, j)),
        scratch_shapes=[pltpu.VMEM((rows, d), BF16)],
        compiler_params=_params("parallel", "arbitrary"),
        name="norm_matmul",
    )(x, g.reshape(1, d), w)


def _matmul_residual_kernel(a_ref, w_ref, res_ref, o_ref):
    o_ref[...] = res_ref[...] + _dot(a_ref[...], w_ref[...])


def _matmul_residual(a, w, res, *, rows):
    t, k = a.shape
    n = w.shape[1]
    return pl.pallas_call(
        _matmul_residual_kernel,
        out_shape=jax.ShapeDtypeStruct((t, n), F32),
        grid=(t // rows,),
        in_specs=[
            pl.BlockSpec((rows, k), lambda i: (i, 0)),
            pl.BlockSpec((k, n), lambda i: (0, 0)),
            pl.BlockSpec((rows, n), lambda i: (i, 0)),
        ],
        out_specs=pl.BlockSpec((rows, n), lambda i: (i, 0)),
        compiler_params=_params("parallel"),
        name="matmul_residual",
    )(a, w, res)


def _pair_constants(chunk):
    pair = 2 * chunk
    i = np.arange(pair)[:, None]
    j = np.arange(pair)[None, :]
    tri = (j <= i).astype(np.float32)
    sel = ((i >= chunk) & (j < chunk)).astype(np.float32)
    blk = ((i // chunk) == (j // chunk)).astype(np.float32)
    ones = np.ones((pair, pair), np.float32)
    c_row = np.concatenate([tri, sel], axis=0)
    c_col = np.concatenate([tri.T, sel.T, blk, ones], axis=1)
    return jnp.asarray(c_row, BF16), jnp.asarray(c_col, BF16)


def _gla_prep_kernel(x_ref, g_ref, wq_ref, wkt_ref, wg_ref, wgu_ref, wgut_ref, bgr_ref, bgc_ref,
                     crow_ref, ccol_ref, qe_ref, qes_ref, ket_ref, klt_ref, ktt_ref, dect_ref,
                     *, scale, pair):
    x = x_ref[...]
    xn = (x * _rms_inv(x) * g_ref[...]).astype(BF16)
    q = _dot(xn, wq_ref[...]) * scale
    kt = _dot_nt(wkt_ref[...], xn)
    glr = _dot(xn, wg_ref[...])
    z = _dot3(glr, wgu_ref[...]) + bgr_ref[...]
    zt = _dot3(wgut_ref[...], glr, dot=_dot_nt) + bgc_ref[...]
    la = jnp.maximum(_log_sigmoid(z) / GLA_GATE_NORMALIZER, GLA_LOG_GATE_MIN)
    lat = jnp.maximum(_log_sigmoid(zt) / GLA_GATE_NORMALIZER, GLA_LOG_GATE_MIN)
    crow = crow_ref[...]
    ccol = ccol_ref[...]
    for p in range(x.shape[0] // pair):
        rows = slice(p * pair, (p + 1) * pair)
        hi, lo = _split(la[rows])
        cr = _dot(crow, hi) + _dot(crow, lo)
        b_pair = cr[:pair]
        b_chunk = b_pair - cr[pair:]
        qp = q[rows]
        qe_ref[rows, :] = (qp * jnp.exp(b_chunk)).astype(BF16)
        qes_ref[rows, :] = (qp * jnp.exp(b_pair)).astype(BF16)
        hi, lo = _split(lat[:, rows])
        cc = _dot(hi, ccol) + _dot(lo, ccol)
        bt_pair = cc[:, :pair]
        bt_chunk = bt_pair - cc[:, pair:2 * pair]
        end_chunk = cc[:, 2 * pair:3 * pair]
        end_pair = cc[:, 3 * pair:]
        ktp = kt[:, rows]
        ket_ref[:, rows] = (ktp * jnp.exp(-bt_chunk)).astype(BF16)
        klt_ref[:, rows] = (ktp * jnp.exp(end_chunk - bt_chunk)).astype(BF16)
        ktt_ref[:, rows] = (ktp * jnp.exp(end_pair - bt_pair)).astype(BF16)
        dect_ref[:, rows] = jnp.exp(end_pair)


def _gla_prep(h, norm_g, w_in, w_gate_up, b_gate):
    t, d = h.shape
    hk = d // 2
    rank = w_gate_up.shape[0]
    pair = 2 * GLA_CHUNK
    rows = PREP_ROWS
    dk = hk // GLA_HEADS
    wq = w_in[:, :hk].astype(BF16)
    wkt = w_in[:, hk:2 * hk].T.astype(BF16)
    wg = jnp.pad(w_in[:, 3 * d:], ((0, 0), (0, LANES - rank))).astype(BF16)
    wgu = jnp.pad(w_gate_up, ((0, LANES - rank), (0, 0)))
    c_row, c_col = _pair_constants(GLA_CHUNK)
    const = lambda shape: pl.BlockSpec(shape, lambda i: (0, 0))
    row_out = pl.BlockSpec((rows, hk), lambda i: (i, 0))
    col_out = pl.BlockSpec((hk, rows), lambda i: (0, i))
    return pl.pallas_call(
        functools.partial(_gla_prep_kernel, scale=dk ** -0.5, pair=pair),
        out_shape=(
            jax.ShapeDtypeStruct((t, hk), BF16),
            jax.ShapeDtypeStruct((t, hk), BF16),
            jax.ShapeDtypeStruct((hk, t), BF16),
            jax.ShapeDtypeStruct((hk, t), BF16),
            jax.ShapeDtypeStruct((hk, t), BF16),
            jax.ShapeDtypeStruct((hk, t), F32),
        ),
        grid=(t // rows,),
        in_specs=[
            pl.BlockSpec((rows, d), lambda i: (i, 0)),
            const((1, d)),
            const((d, hk)),
            const((hk, d)),
            const((d, LANES)),
            const((LANES, hk)),
            const((hk, LANES)),
            const((1, hk)),
            const((hk, 1)),
            const((2 * pair, pair)),
            const((pair, 4 * pair)),
        ],
        out_specs=(row_out, row_out, col_out, col_out, col_out, col_out),
        compiler_params=_params("parallel"),
        name="gla_prep",
    )(h, norm_g.reshape(1, d), wq, wkt, wg, wgu, wgu.T, b_gate.reshape(1, hk),
      b_gate.reshape(hk, 1), c_row, c_col)


def _gla_core_kernel(qe_ref, qes_ref, ket_ref, klt_ref, ktt_ref, dect_ref, v_ref, r_ref, hg_ref,
                     o_ref, s_ref, *, heads, chunk):
    @pl.when(pl.program_id(1) == 0)
    def _():
        s_ref[...] = jnp.zeros_like(s_ref)

    pair = 2 * chunk
    dk = qe_ref.shape[1] // heads
    dv = v_ref.shape[1] // heads
    row = lax.broadcasted_iota(jnp.int32, (pair, pair), 0)
    col = lax.broadcasted_iota(jnp.int32, (pair, pair), 1)
    second = row >= chunk
    first_keys = col < chunk
    m_intra = jnp.logical_and(col <= row, jnp.logical_not(jnp.logical_xor(second, col >= chunk)))
    m_cross = jnp.logical_and(second, first_keys)
    for h in range(heads):
        ks = slice(h * dk, (h + 1) * dk)
        vs = slice(h * dv, (h + 1) * dv)
        qe = qe_ref[:, ks]
        v = v_ref[:, vs]
        att = jnp.where(m_intra, _dot(qe, ket_ref[ks, :]),
                        jnp.where(m_cross, _dot(qe, klt_ref[ks, :]), 0.0))
        state = s_ref[h]
        o = _dot(att.astype(BF16), v) + _dot(qes_ref[:, ks], state.astype(BF16))
        dec = dect_ref[ks, :]
        dec = jnp.concatenate([dec] * (dv // pair), axis=1) if dv > pair else dec[:, :dv]
        s_ref[h] = dec * state + _dot(ktt_ref[ks, :], v)
        on = o * _rms_inv(o) * hg_ref[...]
        r = r_ref[:, vs].astype(F32)
        o_ref[:, vs] = (on * (r * _sigmoid(r))).astype(BF16)


def _gla_core(qe, qes, ket, klt, ktt, dect, vr, head_g, *, batch):
    t, hk = qe.shape
    hv = vr.shape[1] // 2
    pair = 2 * GLA_CHUNK
    npair = t // batch // pair
    dk = hk // GLA_HEADS
    dv = hv // GLA_HEADS
    row_k = pl.BlockSpec((pair, hk), lambda b, p: (b * npair + p, 0))
    col_k = pl.BlockSpec((hk, pair), lambda b, p: (0, b * npair + p))
    return pl.pallas_call(
        functools.partial(_gla_core_kernel, heads=GLA_HEADS, chunk=GLA_CHUNK),
        out_shape=jax.ShapeDtypeStruct((t, hv), BF16),
        grid=(batch, npair),
        in_specs=[
            row_k, row_k, col_k, col_k, col_k, col_k,
            pl.BlockSpec((pair, hv), lambda b, p: (b * npair + p, 0)),
            pl.BlockSpec((pair, hv), lambda b, p: (b * npair + p, 1)),
            pl.BlockSpec((1, dv), lambda b, p: (0, 0)),
        ],
        out_specs=pl.BlockSpec((pair, hv), lambda b, p: (b * npair + p, 0)),
        scratch_shapes=[pltpu.VMEM((GLA_HEADS, dk, dv), F32)],
        compiler_params=_params("parallel", "arbitrary"),
        name="gla_core",
    )(qe, qes, ket, klt, ktt, dect, vr, vr, head_g.reshape(1, dv))


def _gla_layer(h, batch, norm_g, w_in, w_gate_up, b_gate, head_g, w_out):
    d = h.shape[1]
    hk = d // 2
    qe, qes, ket, klt, ktt, dect = _gla_prep(h, norm_g, w_in, w_gate_up, b_gate)
    vr = _norm_matmul(h, norm_g, w_in[:, 2 * hk:2 * hk + 2 * d].astype(BF16),
                      rows=PROJ_ROWS, cols=d, out_dtype=BF16)
    og = _gla_core(qe, qes, ket, klt, ktt, dect, vr, head_g, batch=batch)
    return _matmul_residual(og, w_out.astype(BF16), h, rows=PROJ_ROWS)


META_E, META_W, META_RANK = 0, 2, 4


def _router_kernel(h_ref, g_ref, w_ref, b_ref, tril_ref, meta_ref, cnt_ref, base_ref,
                   *, groups, per_group):
    @pl.when(pl.program_id(0) == 0)
    def _():
        base_ref[...] = jnp.zeros_like(base_ref)

    x = h_ref[...]
    xn = x * _rms_inv(x) * g_ref[...]
    logits = _dot3(xn, w_ref[...]) + b_ref[...]
    lane = lax.broadcasted_iota(jnp.int32, logits.shape, 1).astype(F32)
    neg = -jnp.inf
    far = float(LANES)

    def first_max(vals):
        m = jnp.max(vals, axis=-1, keepdims=True)
        return m, jnp.min(jnp.where(vals == m, lane, far), axis=-1, keepdims=True)

    gl = jnp.where(lane < groups, logits, neg)
    gmax, gidx = first_max(gl)
    p_group = 1.0 / jnp.sum(jnp.exp(gl - gmax), axis=-1, keepdims=True)
    lo = groups + per_group * gidx
    el = jnp.where(jnp.logical_and(lane >= lo, lane < lo + per_group), logits, neg)
    v1, i1 = first_max(el)
    v2, i2 = first_max(jnp.where(lane == i1, neg, el))
    t = jnp.exp(v2 - v1)
    w1 = p_group / (1.0 + t)
    w2 = p_group * t / (1.0 + t)

    oh1 = lane == i1
    oh2 = lane == i2
    onehot = jnp.where(jnp.logical_or(oh1, oh2), 1.0, 0.0).astype(BF16)
    seen = base_ref[...] + _dot(tril_ref[...], onehot)
    rank1 = jnp.sum(jnp.where(oh1, seen, 0.0), axis=-1, keepdims=True) - 1.0
    rank2 = jnp.sum(jnp.where(oh2, seen, 0.0), axis=-1, keepdims=True) - 1.0
    base_ref[...] = seen[-1:, :]
    cnt_ref[...] = jnp.broadcast_to(seen[-1:, :], cnt_ref.shape)

    rec = jnp.zeros_like(logits)
    for k, val in ((META_E, i1 - groups), (META_E + 1, i2 - groups), (META_W, w1),
                   (META_W + 1, w2), (META_RANK, rank1), (META_RANK + 1, rank2)):
        rec = jnp.where(lane == k, val, rec)
    meta_ref[...] = rec


def _moe_router(h, norm_g, w_group, b_group, w_router, b_router):
    t, d = h.shape
    rows = ROUTER_ROWS
    groups = w_group.shape[1]
    ne = w_router.shape[1]
    pad = LANES - groups - ne
    w = jnp.pad(jnp.concatenate([w_group, w_router], axis=1), ((0, 0), (0, pad)))
    b = jnp.pad(jnp.concatenate([b_group, b_router]), (0, pad)).reshape(1, LANES)
    tril = jnp.asarray(np.tril(np.ones((rows, rows), np.float32)), BF16)
    meta, cnt = pl.pallas_call(
        functools.partial(_router_kernel, groups=groups, per_group=ne // groups),
        out_shape=(jax.ShapeDtypeStruct((t, LANES), F32), jax.ShapeDtypeStruct((8, LANES), F32)),
        grid=(t // rows,),
        in_specs=[
            pl.BlockSpec((rows, d), lambda i: (i, 0)),
            pl.BlockSpec((1, d), lambda i: (0, 0)),
            pl.BlockSpec((d, LANES), lambda i: (0, 0)),
            pl.BlockSpec((1, LANES), lambda i: (0, 0)),
            pl.BlockSpec((rows, rows), lambda i: (0, 0)),
        ],
        out_specs=(pl.BlockSpec((rows, LANES), lambda i: (i, 0)),
                   pl.BlockSpec((8, LANES), lambda i: (0, 0))),
        scratch_shapes=[pltpu.VMEM((1, LANES), F32)],
        compiler_params=_params("arbitrary"),
        name="moe_router",
    )(h, norm_g.reshape(1, d), w, b, tril)
    return meta, cnt[0, groups:groups + ne]


def _inverse_kernel(pos0_ref, pos1_ref, src_ref):
    step = pl.program_id(0)
    half = pl.num_programs(0) // 2
    slots = src_ref.shape[0] // INVERSE_STEPS
    tokens = pos0_ref.shape[0] // INVERSE_STEPS

    @pl.when(step < half)
    def _():
        def fill(i, carry):
            for u in range(DMA_UNROLL):
                src_ref[step * slots + i * DMA_UNROLL + u] = 0
            return carry

        lax.fori_loop(0, slots // DMA_UNROLL, fill, 0)

    @pl.when(step >= half)
    def _():
        def place(i, carry):
            for u in range(DMA_UNROLL):
                tok = (step - half) * tokens + i * DMA_UNROLL + u
                src_ref[pos0_ref[tok]] = tok
                src_ref[pos1_ref[tok]] = tok
            return carry

        lax.fori_loop(0, tokens // DMA_UNROLL, place, 0)


def _moe_inverse(pos0, pos1, slots):
    smem = pl.BlockSpec(memory_space=pltpu.SMEM)
    return pl.pallas_call(
        _inverse_kernel,
        out_shape=jax.ShapeDtypeStruct((slots,), jnp.int32),
        grid=(2 * INVERSE_STEPS,),
        in_specs=[smem, smem],
        out_specs=smem,
        compiler_params=_params("arbitrary"),
        name="moe_inverse",
    )(pos0, pos1)


def _row_gather_start(idx_ref, first, count, src_hbm, dst, sem):
    def body(i, carry):
        for u in range(DMA_UNROLL):
            r = i * DMA_UNROLL + u
            pltpu.make_async_copy(src_hbm.at[pl.ds(idx_ref[first + r], 1)], dst.at[pl.ds(r, 1)],
                                  sem).start()
        return carry

    lax.fori_loop(0, count // DMA_UNROLL, body, 0)


def _row_gather_wait(src_hbm, dst, sem):
    pltpu.make_async_copy(src_hbm.at[pl.ds(0, dst.shape[0])], dst, sem).wait()


def _experts_kernel(src_ref, texp_ref, nused_ref, h_hbm, g_ref, wg_ref, wu_ref, wd_ref, y_ref,
                    xbuf, sem, *, rows):
    del texp_ref
    i = pl.program_id(0)
    nused = nused_ref[0]

    def start(tile):
        slot = tile % 2
        _row_gather_start(src_ref, tile * rows, rows, h_hbm, xbuf.at[slot], sem.at[slot])

    @pl.when(i == 0)
    def _():
        start(0)

    @pl.when(i + 1 < nused)
    def _():
        start(i + 1)

    @pl.when(i < nused)
    def _():
        slot = i % 2
        _row_gather_wait(h_hbm, xbuf.at[slot], sem.at[slot])
        x = xbuf[slot]
        xn = (x * _rms_inv(x) * g_ref[...]).astype(BF16)
        gate = _dot(xn, wg_ref[0])
        up = _dot(xn, wu_ref[0])
        act = (gate * _sigmoid(gate)) * up
        y_ref[...] = _dot(act.astype(BF16), wd_ref[0])

    @pl.when(i >= nused)
    def _():
        y_ref[...] = jnp.zeros_like(y_ref)


def _moe_experts(h, norm_g, w_gate, w_up, w_down, src, tile_expert, n_used):
    t, d = h.shape
    ne, _, f = w_gate.shape
    rows = EXPERT_ROWS
    slots = src.shape[0]
    wmap = lambda i, src_r, texp_r, nused_r: (texp_r[i], 0, 0)
    return pl.pallas_call(
        functools.partial(_experts_kernel, rows=rows),
        out_shape=jax.ShapeDtypeStruct((slots, d), F32),
        grid_spec=pltpu.PrefetchScalarGridSpec(
            num_scalar_prefetch=3,
            grid=(slots // rows,),
            in_specs=[
                pl.BlockSpec(memory_space=pl.ANY),
                pl.BlockSpec((1, d), lambda i, *_: (0, 0)),
                pl.BlockSpec((1, d, f), wmap),
                pl.BlockSpec((1, d, f), wmap),
                pl.BlockSpec((1, f, d), wmap),
            ],
            out_specs=pl.BlockSpec((rows, d), lambda i, *_: (i, 0)),
            scratch_shapes=[pltpu.VMEM((2, rows, d), F32), pltpu.SemaphoreType.DMA((2,))],
        ),
        compiler_params=_params("arbitrary"),
        name="moe_experts",
    )(src, tile_expert, n_used, h, norm_g.reshape(1, d), w_gate, w_up, w_down)


def _combine_kernel(pos0_ref, pos1_ref, h_ref, meta_ref, y_hbm, o_ref, buf0, buf1, sem, *, rows):
    i = pl.program_id(0)

    def start(tile):
        slot = tile % 2
        _row_gather_start(pos0_ref, tile * rows, rows, y_hbm, buf0.at[slot], sem.at[0, slot])
        _row_gather_start(pos1_ref, tile * rows, rows, y_hbm, buf1.at[slot], sem.at[1, slot])

    @pl.when(i == 0)
    def _():
        start(0)

    @pl.when(i + 1 < pl.num_programs(0))
    def _():
        start(i + 1)

    slot = i % 2
    _row_gather_wait(y_hbm, buf0.at[slot], sem.at[0, slot])
    _row_gather_wait(y_hbm, buf1.at[slot], sem.at[1, slot])
    meta = meta_ref[...]
    w0 = meta[:, META_W:META_W + 1]
    w1 = meta[:, META_W + 1:META_W + 2]
    o_ref[...] = h_ref[...] + w0 * buf0[slot] + w1 * buf1[slot]


def _moe_combine(h, meta, y, pos0, pos1):
    t, d = h.shape
    rows = COMBINE_ROWS
    return pl.pallas_call(
        functools.partial(_combine_kernel, rows=rows),
        out_shape=jax.ShapeDtypeStruct((t, d), F32),
        grid_spec=pltpu.PrefetchScalarGridSpec(
            num_scalar_prefetch=2,
            grid=(t // rows,),
            in_specs=[
                pl.BlockSpec((rows, d), lambda i, *_: (i, 0)),
                pl.BlockSpec((rows, LANES), lambda i, *_: (i, 0)),
                pl.BlockSpec(memory_space=pl.ANY),
            ],
            out_specs=pl.BlockSpec((rows, d), lambda i, *_: (i, 0)),
            scratch_shapes=[pltpu.VMEM((2, rows, d), F32), pltpu.VMEM((2, rows, d), F32),
                            pltpu.SemaphoreType.DMA((2, 2))],
        ),
        compiler_params=_params("arbitrary"),
        name="moe_combine",
    )(pos0, pos1, h, meta, y)


def _moe_layer(h, norm_g, w_group, b_group, w_router, b_router, w_gate, w_up, w_down):
    t, _ = h.shape
    ne = w_router.shape[1]
    rows = EXPERT_ROWS
    meta, counts = _moe_router(h, norm_g, w_group, b_group, w_router, b_router)
    counts = counts.astype(jnp.int32)
    padded = (counts + rows - 1) // rows * rows
    ends = jnp.cumsum(padded)
    starts = ends - padded
    expert = meta[:, META_E:META_E + 2].astype(jnp.int32)
    rank = meta[:, META_RANK:META_RANK + 2].astype(jnp.int32)
    pos = starts[expert] + rank
    slots = 2 * t + ne * rows
    n_tiles = slots // rows
    n_used = (ends[-1] // rows).astype(jnp.int32)
    tile_start = jnp.arange(n_tiles, dtype=jnp.int32) * rows
    tile_expert = jnp.searchsorted(ends, jnp.minimum(tile_start, ends[-1] - 1), side="right")
    tile_expert = jnp.minimum(tile_expert, ne - 1).astype(jnp.int32)
    pos0 = pos[:, 0]
    pos1 = pos[:, 1]
    src = _moe_inverse(pos0, pos1, slots)
    y = _moe_experts(h, norm_g, w_gate.astype(BF16), w_up.astype(BF16), w_down.astype(BF16), src,
                     tile_expert, n_used.reshape(1))
    return _moe_combine(h, meta, y, pos0, pos1)


def _swa_proj_kernel(x_ref, gq_ref, gkv_ref, wq_ref, wkv_ref, kg_ref, q_ref, k_ref, v_ref):
    x = x_ref[...]
    xs = x * _rms_inv(x)
    q_ref[...] = _dot((xs * gq_ref[...]).astype(BF16), wq_ref[...]).astype(BF16)
    kv = _dot((xs * gkv_ref[...]).astype(BF16), wkv_ref[...])
    half = kv.shape[1] // 2
    v_ref[...] = kv[:, half:].astype(BF16)
    for hd in range(half // LANES):
        sl = slice(hd * LANES, (hd + 1) * LANES)
        k = kv[:, sl]
        k_ref[:, sl] = (k * _rms_inv(k) * kg_ref[...]).astype(BF16)


def _swa_proj(h, q_norm_g, kv_norm_g, w_q, w_kv, k_norm):
    t, d = h.shape
    hd = SWA_HEAD_DIM
    kvh = w_kv.shape[1] // (2 * hd)
    rows = SWA_PROJ_ROWS
    rep = LANES // hd
    c = np.arange(kvh * LANES)
    src_col = (c // LANES) * hd + (c % hd)
    w_dup = jnp.concatenate([w_kv[:, src_col], w_kv[:, kvh * hd + src_col]], axis=1).astype(BF16)
    kg = jnp.tile(k_norm, rep).reshape(1, LANES)
    const = lambda shape: pl.BlockSpec(shape, lambda i: (0, 0))
    return pl.pallas_call(
        _swa_proj_kernel,
        out_shape=(
            jax.ShapeDtypeStruct((t, w_q.shape[1]), BF16),
            jax.ShapeDtypeStruct((t, kvh * LANES), BF16),
            jax.ShapeDtypeStruct((t, kvh * LANES), BF16),
        ),
        grid=(t // rows,),
        in_specs=[
            pl.BlockSpec((rows, d), lambda i: (i, 0)),
            const((1, d)), const((1, d)),
            const((d, w_q.shape[1])), const((d, 2 * kvh * LANES)), const((1, LANES)),
        ],
        out_specs=(
            pl.BlockSpec((rows, w_q.shape[1]), lambda i: (i, 0)),
            pl.BlockSpec((rows, kvh * LANES), lambda i: (i, 0)),
            pl.BlockSpec((rows, kvh * LANES), lambda i: (i, 0)),
        ),
        compiler_params=_params("parallel"),
        name="swa_proj",
    )(h, q_norm_g.reshape(1, d), kv_norm_g.reshape(1, d), w_q.astype(BF16), w_dup, kg)


def _swa_attn_kernel(q_ref, kp_ref, kc_ref, vp_ref, vc_ref, bias_ref, sink_ref, qg_ref, blk_ref,
                     o_ref, *, kvh, pairs, window):
    first_block = pl.program_id(1) == 0
    hd = LANES // 2
    lane = lax.broadcasted_iota(jnp.int32, (window, LANES), 1)
    low = lane < hd
    width = 2 * window
    scol = lax.broadcasted_iota(jnp.int32, (pairs * window, 2 * width), 1)
    masked_prev = jnp.logical_and((scol & window) == 0, first_block)
    olane = lax.broadcasted_iota(jnp.int32, (pairs * window, LANES), 1)
    zero = jnp.zeros((), BF16)

    def block_diag(prev, cur):
        return jnp.concatenate([jnp.where(low, prev, zero), jnp.where(low, cur, zero),
                                jnp.where(low, zero, prev), jnp.where(low, zero, cur)], axis=0)

    for h in range(kvh):
        ksl = slice(h * LANES, (h + 1) * LANES)
        kk = block_diag(kp_ref[:, ksl], kc_ref[:, ksl])
        vv = block_diag(vp_ref[:, ksl], vc_ref[:, ksl])
        base = h * pairs * LANES
        q2 = jnp.concatenate([q_ref[:, base + p * LANES:base + (p + 1) * LANES]
                              for p in range(pairs)], axis=0).astype(F32)
        ms = _dot((q2 * q2).astype(BF16), blk_ref[...])
        qn = (q2 * lax.rsqrt(ms + RMS_EPS) * qg_ref[...]).astype(BF16)
        s = _dot_nt(qn, kk) + bias_ref[h]
        s = jnp.where(masked_prev, -jnp.inf, s)
        sink = sink_ref[h]
        probs = []
        rden = []
        for half in range(2):
            sh = s[:, half * width:(half + 1) * width]
            sk = sink[:, half:half + 1]
            m = jnp.maximum(jnp.max(sh, axis=-1, keepdims=True), sk)
            p = jnp.exp(sh - m)
            probs.append(p.astype(BF16))
            rden.append(1.0 / (jnp.sum(p, axis=-1, keepdims=True) + jnp.exp(sk - m)))
        o2 = _dot(jnp.concatenate(probs, axis=1), vv) * jnp.where(olane < hd, rden[0], rden[1])
        for p in range(pairs):
            o_ref[:, base + p * LANES:base + (p + 1) * LANES] = (
                o2[p * window:(p + 1) * window].astype(BF16))


def _t5_bucket(dist):
    max_exact = NUM_BUCKETS // 2
    n = np.maximum(dist, 0)
    large = max_exact + (np.log(np.maximum(n, max_exact) / max_exact)
                         / math.log(REL_MAX_DISTANCE / max_exact)
                         * (NUM_BUCKETS - max_exact)).astype(np.int32)
    return np.where(n < max_exact, n, np.minimum(large, NUM_BUCKETS - 1)).astype(np.int32)


def _swa_attn(q, kd, vd, rel_bias, q_norm, sinks, *, batch):
    t, dq = q.shape
    kvh = kd.shape[1] // LANES
    hq = dq // SWA_HEAD_DIM
    pairs = hq // kvh // 2
    w = WINDOW
    nblk = t // batch // w
    row = np.arange(w)[:, None]
    col = np.arange(2 * w)[None, :]
    dist = w + row - col
    band = (dist >= 0) & (dist < w)
    bias = rel_bias.astype(F32)[_t5_bucket(dist)]
    bias = jnp.where(jnp.asarray(band)[..., None], bias, -jnp.inf)
    bias = bias.transpose(2, 0, 1).reshape(kvh, pairs, 2, w, 2 * w).transpose(0, 1, 3, 2, 4)
    bias = bias.reshape(kvh, pairs * w, 4 * w)
    sink = jnp.broadcast_to(sinks.astype(F32).reshape(kvh, pairs, 1, 2), (kvh, pairs, w, 2))
    sink = sink.reshape(kvh, pairs * w, 2)
    qg = (jnp.tile(q_norm, LANES // SWA_HEAD_DIM) * SWA_HEAD_DIM ** -0.5).reshape(1, LANES)
    lane = np.arange(LANES)
    blk = ((lane[:, None] // SWA_HEAD_DIM) == (lane[None, :] // SWA_HEAD_DIM)) / SWA_HEAD_DIM
    cur = lambda b, n: (b * nblk + n, 0)
    prev = lambda b, n: (b * nblk + jnp.maximum(n - 1, 0), 0)
    const3 = lambda shape: pl.BlockSpec(shape, lambda b, n: (0, 0, 0))
    return pl.pallas_call(
        functools.partial(_swa_attn_kernel, kvh=kvh, pairs=pairs, window=w),
        out_shape=jax.ShapeDtypeStruct((t, dq), BF16),
        grid=(batch, nblk),
        in_specs=[
            pl.BlockSpec((w, dq), cur),
            pl.BlockSpec((w, kvh * LANES), prev), pl.BlockSpec((w, kvh * LANES), cur),
            pl.BlockSpec((w, kvh * LANES), prev), pl.BlockSpec((w, kvh * LANES), cur),
            const3((kvh, pairs * w, 4 * w)), const3((kvh, pairs * w, 2)),
            pl.BlockSpec((1, LANES), lambda b, n: (0, 0)),
            pl.BlockSpec((LANES, LANES), lambda b, n: (0, 0)),
        ],
        out_specs=pl.BlockSpec((w, dq), cur),
        compiler_params=_params("parallel", "parallel"),
        name="swa_attn",
    )(q, kd, kd, vd, vd, bias, sink, qg, jnp.asarray(blk, BF16))


def _swa_layer(h, batch, kv_norm, w_kv, k_norm, rel_bias, norm_g, w_q, q_norm, sinks, w_out):
    q, kd, vd = _swa_proj(h, norm_g, kv_norm, w_q, w_kv, k_norm)
    o = _swa_attn(q, kd, vd, rel_bias, q_norm, sinks, batch=batch)
    return _matmul_residual(o, w_out.astype(BF16), h, rows=PROJ_ROWS)


def kernel(x, gla_norm, gla_w_in, gla_w_gate_up, gla_b_gate, gla_head_norm, gla_w_out, kv_norm, w_kv, k_norm, rel_bias, swa_norm, swa_w_q, swa_q_norm, swa_sinks, swa_w_out, moe_norm, moe_w_group, moe_b_group, moe_w_router, moe_b_router, moe_w_gate, moe_w_up, moe_w_down):
    batch, seq, d = x.shape
    assert gla_norm.shape[0] == 1 and swa_norm.shape[0] == 1 and moe_norm.shape[0] == 2
    h = x.reshape(batch * seq, d)
    h = _gla_layer(h, batch, gla_norm[0], gla_w_in[0], gla_w_gate_up[0], gla_b_gate[0],
                   gla_head_norm[0], gla_w_out[0])
    h = _moe_layer(h, moe_norm[0], moe_w_group[0], moe_b_group[0], moe_w_router[0],
                   moe_b_router[0], moe_w_gate[0], moe_w_up[0], moe_w_down[0])
    h = _swa_layer(h, batch, kv_norm, w_kv, k_norm, rel_bias, swa_norm[0], swa_w_q[0],
                   swa_q_norm[0], swa_sinks[0], swa_w_out[0])
    h = _moe_layer(h, moe_norm[1], moe_w_group[1], moe_b_group[1], moe_w_router[1],
                   moe_b_router[1], moe_w_gate[1], moe_w_up[1], moe_w_down[1])
    return h.reshape(batch, seq, d)
```

```python
import functools
import math

import jax
import jax.numpy as jnp
import numpy as np
from jax import lax
from jax.experimental import pallas as pl
from jax.experimental.pallas import tpu as pltpu

F32 = jnp.float32
BF16 = jnp.bfloat16

RMS_EPS = 1e-6
GLA_HEADS = 4
GLA_GATE_RANK = 16
GLA_GATE_NORMALIZER = 16.0
GLA_LOG_GATE_MIN = -1.0
GLA_CHUNK = 64
SWA_HEAD_DIM = 64
SWA_GROUPS = 8
WINDOW = 128
NUM_BUCKETS = 32
REL_MAX_DISTANCE = 128
MOE_GROUPS = 4
MOE_EXPERTS_PER_GROUP = 8

LANES = 128
SUBLANES = 8
VMEM_LIMIT_BYTES = 56 * 1024 * 1024

PREP_ROWS = 256
PROJ_ROWS = 512
ROUTER_ROWS = 512
EXPERT_ROWS = 512
COMBINE_ROWS = 256
SWA_PROJ_ROWS = 256
DMA_UNROLL = 8
INVERSE_STEPS = 64


def _params(*semantics):
    return pltpu.CompilerParams(dimension_semantics=semantics, vmem_limit_bytes=VMEM_LIMIT_BYTES)


def _dot(a, b):
    return jnp.dot(a, b, preferred_element_type=F32)


def _dot_nt(a, b):
    return lax.dot_general(a, b, (((1,), (1,)), ((), ())), preferred_element_type=F32)


def _split(x):
    hi = x.astype(BF16)
    lo = (x - hi.astype(F32)).astype(BF16)
    return hi, lo


def _dot3(a, b, dot=_dot):
    ah, al = _split(a)
    bh, bl = _split(b)
    return dot(ah, bh) + dot(al, bh) + dot(ah, bl)


def _rms_inv(x):
    return lax.rsqrt(jnp.mean(x * x, axis=-1, keepdims=True) + RMS_EPS)


def _sigmoid(x):
    return 1.0 / (1.0 + jnp.exp(-x))


def _log_sigmoid(x):
    return jnp.minimum(x, 0.0) - jnp.log1p(jnp.exp(-jnp.abs(x)))


def _norm_matmul_kernel(x_ref, g_ref, w_ref, o_ref, xn_ref):
    @pl.when(pl.program_id(1) == 0)
    def _():
        x = x_ref[...]
        xn_ref[...] = (x * _rms_inv(x) * g_ref[...]).astype(BF16)

    o_ref[...] = _dot(xn_ref[...], w_ref[...]).astype(o_ref.dtype)


def _norm_matmul(x, g, w, *, rows, cols, out_dtype):
    t, d = x.shape
    n = w.shape[1]
    return pl.pallas_call(
        _norm_matmul_kernel,
        out_shape=jax.ShapeDtypeStruct((t, n), out_dtype),
        grid=(t // rows, n // cols),
        in_specs=[
            pl.BlockSpec((rows, d), lambda i, j: (i, 0)),
            pl.BlockSpec((1, d), lambda i, j: (0, 0)),
            pl.BlockSpec((d, cols), lambda i, j: (0, j)),
        ],
        out_specs=pl.BlockSpec((rows, cols), lambda i, j: (i, j)),
        scratch_shapes=[pltpu.VMEM((rows, d), BF16)],
        compiler_params=_params("parallel", "arbitrary"),
        name="norm_matmul",
    )(x, g.reshape(1, d), w)


def _matmul_residual_kernel(a_ref, w_ref, res_ref, o_ref):
    o_ref[...] = res_ref[...] + _dot(a_ref[...], w_ref[...])


def _matmul_residual(a, w, res, *, rows):
    t, k = a.shape
    n = w.shape[1]
    return pl.pallas_call(
        _matmul_residual_kernel,
        out_shape=jax.ShapeDtypeStruct((t, n), F32),
        grid=(t // rows,),
        in_specs=[
            pl.BlockSpec((rows, k), lambda i: (i, 0)),
            pl.BlockSpec((k, n), lambda i: (0, 0)),
            pl.BlockSpec((rows, n), lambda i: (i, 0)),
        ],
        out_specs=pl.BlockSpec((rows, n), lambda i: (i, 0)),
        compiler_params=_params("parallel"),
        name="matmul_residual",
    )(a, w, res)


def _pair_constants(chunk):
    pair = 2 * chunk
    i = np.arange(pair)[:, None]
    j = np.arange(pair)[None, :]
    tri = (j <= i).astype(np.float32)
    sel = ((i >= chunk) & (j < chunk)).astype(np.float32)
    blk = ((i // chunk) == (j // chunk)).astype(np.float32)
    ones = np.ones((pair, pair), np.float32)
    c_row = np.concatenate([tri, sel], axis=0)
    c_col = np.concatenate([tri.T, sel.T, blk, ones], axis=1)
    return jnp.asarray(c_row, BF16), jnp.asarray(c_col, BF16)


def _gla_prep_kernel(x_ref, g_ref, wq_ref, wkt_ref, wg_ref, wgu_ref, wgut_ref, bgr_ref, bgc_ref,
                     crow_ref, ccol_ref, qe_ref, qes_ref, ket_ref, klt_ref, ktt_ref, dect_ref,
                     *, scale, pair):
    x = x_ref[...]
    xn = (x * _rms_inv(x) * g_ref[...]).astype(BF16)
    q = _dot(xn, wq_ref[...]) * scale
    kt = _dot_nt(wkt_ref[...], xn)
    glr = _dot(xn, wg_ref[...])
    z = _dot3(glr, wgu_ref[...]) + bgr_ref[...]
    zt = _dot3(wgut_ref[...], glr, dot=_dot_nt) + bgc_ref[...]
    la = jnp.maximum(_log_sigmoid(z) / GLA_GATE_NORMALIZER, GLA_LOG_GATE_MIN)
    lat = jnp.maximum(_log_sigmoid(zt) / GLA_GATE_NORMALIZER, GLA_LOG_GATE_MIN)
    crow = crow_ref[...]
    ccol = ccol_ref[...]
    for p in range(x.shape[0] // pair):
        rows = slice(p * pair, (p + 1) * pair)
        hi, lo = _split(la[rows])
        cr = _dot(crow, hi) + _dot(crow, lo)
        b_pair = cr[:pair]
        b_chunk = b_pair - cr[pair:]
        qp = q[rows]
        qe_ref[rows, :] = (qp * jnp.exp(b_chunk)).astype(BF16)
        qes_ref[rows, :] = (qp * jnp.exp(b_pair)).astype(BF16)
        hi, lo = _split(lat[:, rows])
        cc = _dot(hi, ccol) + _dot(lo, ccol)
        bt_pair = cc[:, :pair]
        bt_chunk = bt_pair - cc[:, pair:2 * pair]
        end_chunk = cc[:, 2 * pair:3 * pair]
        end_pair = cc[:, 3 * pair:]
        ktp = kt[:, rows]
        ket_ref[:, rows] = (ktp * jnp.exp(-bt_chunk)).astype(BF16)
        klt_ref[:, rows] = (ktp * jnp.exp(end_chunk - bt_chunk)).astype(BF16)
        ktt_ref[:, rows] = (ktp * jnp.exp(end_pair - bt_pair)).astype(BF16)
        dect_ref[:, rows] = jnp.exp(end_pair)


def _gla_prep(h, norm_g, w_in, w_gate_up, b_gate):
    t, d = h.shape
    hk = d // 2
    rank = w_gate_up.shape[0]
    pair = 2 * GLA_CHUNK
    rows = PREP_ROWS
    dk = hk // GLA_HEADS
    wq = w_in[:, :hk].astype(BF16)
    wkt = w_in[:, hk:2 * hk].T.astype(BF16)
    wg = jnp.pad(w_in[:, 3 * d:], ((0, 0), (0, LANES - rank))).astype(BF16)
    wgu = jnp.pad(w_gate_up, ((0, LANES - rank), (0, 0)))
    c_row, c_col = _pair_constants(GLA_CHUNK)
    const = lambda shape: pl.BlockSpec(shape, lambda i: (0, 0))
    row_out = pl.BlockSpec((rows, hk), lambda i: (i, 0))
    col_out = pl.BlockSpec((hk, rows), lambda i: (0, i))
    return pl.pallas_call(
        functools.partial(_gla_prep_kernel, scale=dk ** -0.5, pair=pair),
        out_shape=(
            jax.ShapeDtypeStruct((t, hk), BF16),
            jax.ShapeDtypeStruct((t, hk), BF16),
            jax.ShapeDtypeStruct((hk, t), BF16),
            jax.ShapeDtypeStruct((hk, t), BF16),
            jax.ShapeDtypeStruct((hk, t), BF16),
            jax.ShapeDtypeStruct((hk, t), F32),
        ),
        grid=(t // rows,),
        in_specs=[
            pl.BlockSpec((rows, d), lambda i: (i, 0)),
            const((1, d)),
            const((d, hk)),
            const((hk, d)),
            const((d, LANES)),
            const((LANES, hk)),
            const((hk, LANES)),
            const((1, hk)),
            const((hk, 1)),
            const((2 * pair, pair)),
            const((pair, 4 * pair)),
        ],
        out_specs=(row_out, row_out, col_out, col_out, col_out, col_out),
        compiler_params=_params("parallel"),
        name="gla_prep",
    )(h, norm_g.reshape(1, d), wq, wkt, wg, wgu, wgu.T, b_gate.reshape(1, hk),
      b_gate.reshape(hk, 1), c_row, c_col)


def _gla_core_kernel(qe_ref, qes_ref, ket_ref, klt_ref, ktt_ref, dect_ref, v_ref, r_ref, hg_ref,
                     o_ref, s_ref, *, heads, chunk):
    @pl.when(pl.program_id(1) == 0)
    def _():
        s_ref[...] = jnp.zeros_like(s_ref)

    pair = 2 * chunk
    dk = qe_ref.shape[1] // heads
    dv = v_ref.shape[1] // heads
    row = lax.broadcasted_iota(jnp.int32, (pair, pair), 0)
    col = lax.broadcasted_iota(jnp.int32, (pair, pair), 1)
    second = row >= chunk
    first_keys = col < chunk
    m_intra = jnp.logical_and(col <= row, jnp.logical_not(jnp.logical_xor(second, col >= chunk)))
    m_cross = jnp.logical_and(second, first_keys)
    for h in range(heads):
        ks = slice(h * dk, (h + 1) * dk)
        vs = slice(h * dv, (h + 1) * dv)
        qe = qe_ref[:, ks]
        v = v_ref[:, vs]
        att = jnp.where(m_intra, _dot(qe, ket_ref[ks, :]),
                        jnp.where(m_cross, _dot(qe, klt_ref[ks, :]), 0.0))
        state = s_ref[h]
        o = _dot(att.astype(BF16), v) + _dot(qes_ref[:, ks], state.astype(BF16))
        dec = dect_ref[ks, :]
        dec = jnp.concatenate([dec] * (dv // pair), axis=1) if dv > pair else dec[:, :dv]
        s_ref[h] = dec * state + _dot(ktt_ref[ks, :], v)
        on = o * _rms_inv(o) * hg_ref[...]
        r = r_ref[:, vs].astype(F32)
        o_ref[:, vs] = (on * (r * _sigmoid(r))).astype(BF16)


def _gla_core(qe, qes, ket, klt, ktt, dect, vr, head_g, *, batch):
    t, hk = qe.shape
    hv = vr.shape[1] // 2
    pair = 2 * GLA_CHUNK
    npair = t // batch // pair
    dk = hk // GLA_HEADS
    dv = hv // GLA_HEADS
    row_k = pl.BlockSpec((pair, hk), lambda b, p: (b * npair + p, 0))
    col_k = pl.BlockSpec((hk, pair), lambda b, p: (0, b * npair + p))
    return pl.pallas_call(
        functools.partial(_gla_core_kernel, heads=GLA_HEADS, chunk=GLA_CHUNK),
        out_shape=jax.ShapeDtypeStruct((t, hv), BF16),
        grid=(batch, npair),
        in_specs=[
            row_k, row_k, col_k, col_k, col_k, col_k,
            pl.BlockSpec((pair, hv), lambda b, p: (b * npair + p, 0)),
            pl.BlockSpec((pair, hv), lambda b, p: (b * npair + p, 1)),
            pl.BlockSpec((1, dv), lambda b, p: (0, 0)),
        ],
        out_specs=pl.BlockSpec((pair, hv), lambda b, p: (b * npair + p, 0)),
        scratch_shapes=[pltpu.VMEM((GLA_HEADS, dk, dv), F32)],
        compiler_params=_params("parallel", "arbitrary"),
        name="gla_core",
    )(qe, qes, ket, klt, ktt, dect, vr, vr, head_g.reshape(1, dv))


def _gla_layer(h, batch, norm_g, w_in, w_gate_up, b_gate, head_g, w_out):
    d = h.shape[1]
    hk = d // 2
    qe, qes, ket, klt, ktt, dect = _gla_prep(h, norm_g, w_in, w_gate_up, b_gate)
    vr = _norm_matmul(h, norm_g, w_in[:, 2 * hk:2 * hk + 2 * d].astype(BF16),
                      rows=PROJ_ROWS, cols=d, out_dtype=BF16)
    og = _gla_core(qe, qes, ket, klt, ktt, dect, vr, head_g, batch=batch)
    return _matmul_residual(og, w_out.astype(BF16), h, rows=PROJ_ROWS)


META_E, META_W, META_RANK = 0, 2, 4


def _router_kernel(h_ref, g_ref, w_ref, b_ref, tril_ref, meta_ref, cnt_ref, base_ref,
                   *, groups, per_group):
    @pl.when(pl.program_id(0) == 0)
    def _():
        base_ref[...] = jnp.zeros_like(base_ref)

    x = h_ref[...]
    xn = x * _rms_inv(x) * g_ref[...]
    logits = _dot3(xn, w_ref[...]) + b_ref[...]
    lane = lax.broadcasted_iota(jnp.int32, logits.shape, 1).astype(F32)
    neg = -jnp.inf
    far = float(LANES)

    def first_max(vals):
        m = jnp.max(vals, axis=-1, keepdims=True)
        return m, jnp.min(jnp.where(vals == m, lane, far), axis=-1, keepdims=True)

    gl = jnp.where(lane < groups, logits, neg)
    gmax, gidx = first_max(gl)
    p_group = 1.0 / jnp.sum(jnp.exp(gl - gmax), axis=-1, keepdims=True)
    lo = groups + per_group * gidx
    el = jnp.where(jnp.logical_and(lane >= lo, lane < lo + per_group), logits, neg)
    v1, i1 = first_max(el)
    v2, i2 = first_max(jnp.where(lane == i1, neg, el))
    t = jnp.exp(v2 - v1)
    w1 = p_group / (1.0 + t)
    w2 = p_group * t / (1.0 + t)

    oh1 = lane == i1
    oh2 = lane == i2
    onehot = jnp.where(jnp.logical_or(oh1, oh2), 1.0, 0.0).astype(BF16)
    seen = base_ref[...] + _dot(tril_ref[...], onehot)
    rank1 = jnp.sum(jnp.where(oh1, seen, 0.0), axis=-1, keepdims=True) - 1.0
    rank2 = jnp.sum(jnp.where(oh2, seen, 0.0), axis=-1, keepdims=True) - 1.0
    base_ref[...] = seen[-1:, :]
    cnt_ref[...] = jnp.broadcast_to(seen[-1:, :], cnt_ref.shape)

    rec = jnp.zeros_like(logits)
    for k, val in ((META_E, i1 - groups), (META_E + 1, i2 - groups), (META_W, w1),
                   (META_W + 1, w2), (META_RANK, rank1), (META_RANK + 1, rank2)):
        rec = jnp.where(lane == k, val, rec)
    meta_ref[...] = rec


def _moe_router(h, norm_g, w_group, b_group, w_router, b_router):
    t, d = h.shape
    rows = ROUTER_ROWS
    groups = w_group.shape[1]
    ne = w_router.shape[1]
    pad = LANES - groups - ne
    w = jnp.pad(jnp.concatenate([w_group, w_router], axis=1), ((0, 0), (0, pad)))
    b = jnp.pad(jnp.concatenate([b_group, b_router]), (0, pad)).reshape(1, LANES)
    tril = jnp.asarray(np.tril(np.ones((rows, rows), np.float32)), BF16)
    meta, cnt = pl.pallas_call(
        functools.partial(_router_kernel, groups=groups, per_group=ne // groups),
        out_shape=(jax.ShapeDtypeStruct((t, LANES), F32), jax.ShapeDtypeStruct((8, LANES), F32)),
        grid=(t // rows,),
        in_specs=[
            pl.BlockSpec((rows, d), lambda i: (i, 0)),
            pl.BlockSpec((1, d), lambda i: (0, 0)),
            pl.BlockSpec((d, LANES), lambda i: (0, 0)),
            pl.BlockSpec((1, LANES), lambda i: (0, 0)),
            pl.BlockSpec((rows, rows), lambda i: (0, 0)),
        ],
        out_specs=(pl.BlockSpec((rows, LANES), lambda i: (i, 0)),
                   pl.BlockSpec((8, LANES), lambda i: (0, 0))),
        scratch_shapes=[pltpu.VMEM((1, LANES), F32)],
        compiler_params=_params("arbitrary"),
        name="moe_router",
    )(h, norm_g.reshape(1, d), w, b, tril)
    return meta, cnt[0, groups:groups + ne]


def _inverse_kernel(pos0_ref, pos1_ref, src_ref):
    step = pl.program_id(0)
    half = pl.num_programs(0) // 2
    slots = src_ref.shape[0] // INVERSE_STEPS
    tokens = pos0_ref.shape[0] // INVERSE_STEPS

    @pl.when(step < half)
    def _():
        def fill(i, carry):
            for u in range(DMA_UNROLL):
                src_ref[step * slots + i * DMA_UNROLL + u] = 0
            return carry

        lax.fori_loop(0, slots // DMA_UNROLL, fill, 0)

    @pl.when(step >= half)
    def _():
        def place(i, carry):
            for u in range(DMA_UNROLL):
                tok = (step - half) * tokens + i * DMA_UNROLL + u
                src_ref[pos0_ref[tok]] = tok
                src_ref[pos1_ref[tok]] = tok
            return carry

        lax.fori_loop(0, tokens // DMA_UNROLL, place, 0)


def _moe_inverse(pos0, pos1, slots):
    smem = pl.BlockSpec(memory_space=pltpu.SMEM)
    return pl.pallas_call(
        _inverse_kernel,
        out_shape=jax.ShapeDtypeStruct((slots,), jnp.int32),
        grid=(2 * INVERSE_STEPS,),
        in_specs=[smem, smem],
        out_specs=smem,
        compiler_params=_params("arbitrary"),
        name="moe_inverse",
    )(pos0, pos1)


def _row_copy(src_hbm, row, dst, r_tile, r_sub, sem):
    return pltpu.make_async_copy(src_hbm.at[pl.ds(row, 1)], dst.at[r_tile, pl.ds(r_sub, 1)], sem)


def _rows_wait(dst, sem):
    pltpu.make_async_copy(dst, dst, sem).wait()


def _experts_kernel(src_ref, texp_ref, nused_ref, h_hbm, g_ref, wg_ref, wu_ref, wd_ref, y_ref,
                    buf0, buf1, wgb, wub, wdb, sem, *, rows):
    i = pl.program_id(0)
    nused = nused_ref[0]
    half = rows // 2
    d = y_ref.shape[1]
    bufs = (buf0, buf1)

    def start(tile, hf):
        base = tile * rows + hf * half
        for r in range(half):
            _row_copy(h_hbm, src_ref[base + r], bufs[hf], r // SUBLANES, r % SUBLANES,
                      sem.at[hf]).start()

    @pl.when(i == 0)
    def _():
        start(0, 0)

    fresh = jnp.logical_or(i == 0, texp_ref[i] != texp_ref[jnp.maximum(i - 1, 0)])

    @pl.when(jnp.logical_and(fresh, i < nused))
    def _():
        wgb[...] = wg_ref[0].astype(BF16)
        wub[...] = wu_ref[0].astype(BF16)
        wdb[...] = wd_ref[0].astype(BF16)

    @pl.when(i < nused)
    def _():
        for hf in range(2):
            _rows_wait(bufs[hf], sem.at[hf])
            if hf == 0:
                start(i, 1)
            else:
                start(jnp.minimum(i + 1, nused - 1), 0)
            x = bufs[hf][...].reshape(half, d)
            xn = (x * _rms_inv(x) * g_ref[...]).astype(BF16)
            gate = _dot(xn, wgb[...])
            up = _dot(xn, wub[...])
            act = (gate * _sigmoid(gate)) * up
            y_ref[hf * half:(hf + 1) * half, :] = _dot(act.astype(BF16), wdb[...])

    @pl.when(i == nused - 1)
    def _():
        _rows_wait(buf0, sem.at[0])

    @pl.when(i >= nused)
    def _():
        y_ref[...] = jnp.zeros_like(y_ref)


def _moe_experts(h, norm_g, w_gate, w_up, w_down, src, tile_expert, n_used):
    t, d = h.shape
    ne, _, f = w_gate.shape
    rows = EXPERT_ROWS
    slots = src.shape[0]
    wmap = lambda i, src_r, texp_r, nused_r: (texp_r[i], 0, 0)
    gather_buf = pltpu.VMEM((rows // 2 // SUBLANES, SUBLANES, d), F32)
    return pl.pallas_call(
        functools.partial(_experts_kernel, rows=rows),
        out_shape=jax.ShapeDtypeStruct((slots, d), F32),
        grid_spec=pltpu.PrefetchScalarGridSpec(
            num_scalar_prefetch=3,
            grid=(slots // rows,),
            in_specs=[
                pl.BlockSpec(memory_space=pl.ANY),
                pl.BlockSpec((1, d), lambda i, *_: (0, 0)),
                pl.BlockSpec((1, d, f), wmap),
                pl.BlockSpec((1, d, f), wmap),
                pl.BlockSpec((1, f, d), wmap),
            ],
            out_specs=pl.BlockSpec((rows, d), lambda i, *_: (i, 0)),
            scratch_shapes=[gather_buf, gather_buf, pltpu.VMEM((d, f), BF16),
                            pltpu.VMEM((d, f), BF16), pltpu.VMEM((f, d), BF16),
                            pltpu.SemaphoreType.DMA((2,))],
        ),
        compiler_params=_params("arbitrary"),
        name="moe_experts",
    )(src, tile_expert, n_used, h, norm_g.reshape(1, d), w_gate, w_up, w_down)


def _combine_kernel(pos0_ref, pos1_ref, h_ref, meta_ref, y_hbm, o_ref, buf0, buf1, sem, *, rows):
    i = pl.program_id(0)

    def start(tile):
        slot = tile % 2

        def body(g, carry):
            for u in range(SUBLANES):
                tok = tile * rows + g * SUBLANES + u
                _row_copy(y_hbm, pos0_ref[tok], buf0.at[slot], g, u, sem.at[0, slot]).start()
                _row_copy(y_hbm, pos1_ref[tok], buf1.at[slot], g, u, sem.at[1, slot]).start()
            return carry

        lax.fori_loop(0, rows // SUBLANES, body, 0)

    @pl.when(i == 0)
    def _():
        start(0)

    @pl.when(i + 1 < pl.num_programs(0))
    def _():
        start(i + 1)

    slot = i % 2
    _rows_wait(buf0.at[slot], sem.at[0, slot])
    _rows_wait(buf1.at[slot], sem.at[1, slot])
    meta = meta_ref[...]
    w0 = meta[:, META_W:META_W + 1]
    w1 = meta[:, META_W + 1:META_W + 2]
    shape = h_ref.shape
    o_ref[...] = h_ref[...] + w0 * buf0[slot].reshape(shape) + w1 * buf1[slot].reshape(shape)


def _moe_combine(h, meta, y, pos0, pos1):
    t, d = h.shape
    rows = COMBINE_ROWS
    gather_buf = pltpu.VMEM((2, rows // SUBLANES, SUBLANES, d), F32)
    return pl.pallas_call(
        functools.partial(_combine_kernel, rows=rows),
        out_shape=jax.ShapeDtypeStruct((t, d), F32),
        grid_spec=pltpu.PrefetchScalarGridSpec(
            num_scalar_prefetch=2,
            grid=(t // rows,),
            in_specs=[
                pl.BlockSpec((rows, d), lambda i, *_: (i, 0)),
                pl.BlockSpec((rows, LANES), lambda i, *_: (i, 0)),
                pl.BlockSpec(memory_space=pl.ANY),
            ],
            out_specs=pl.BlockSpec((rows, d), lambda i, *_: (i, 0)),
            scratch_shapes=[gather_buf, gather_buf, pltpu.SemaphoreType.DMA((2, 2))],
        ),
        compiler_params=_params("arbitrary"),
        name="moe_combine",
    )(pos0, pos1, h, meta, y)


def _moe_layer(h, norm_g, w_group, b_group, w_router, b_router, w_gate, w_up, w_down):
    t, _ = h.shape
    ne = w_router.shape[1]
    rows = EXPERT_ROWS
    meta, counts = _moe_router(h, norm_g, w_group, b_group, w_router, b_router)
    counts = counts.astype(jnp.int32)
    padded = (counts + rows - 1) // rows * rows
    ends = jnp.cumsum(padded)
    starts = ends - padded
    expert = meta[:, META_E:META_E + 2].astype(jnp.int32)
    rank = meta[:, META_RANK:META_RANK + 2].astype(jnp.int32)
    pos = starts[expert] + rank
    slots = 2 * t + ne * rows
    n_tiles = slots // rows
    n_used = (ends[-1] // rows).astype(jnp.int32)
    tile_start = jnp.arange(n_tiles, dtype=jnp.int32) * rows
    tile_start = jnp.minimum(tile_start, ends[-1] - 1)
    tile_expert = jnp.sum(tile_start[:, None] >= ends[None, :], axis=1, dtype=jnp.int32)
    tile_expert = jnp.minimum(tile_expert, ne - 1)
    pos0 = pos[:, 0]
    pos1 = pos[:, 1]
    src = _moe_inverse(pos0, pos1, slots)
    y = _moe_experts(h, norm_g, w_gate, w_up, w_down, src, tile_expert, n_used.reshape(1))
    return _moe_combine(h, meta, y, pos0, pos1)


def _swa_proj_kernel(x_ref, gq_ref, gkv_ref, wq_ref, wkv_ref, kg_ref, q_ref, k_ref, v_ref):
    x = x_ref[...]
    xs = x * _rms_inv(x)
    q_ref[...] = _dot((xs * gq_ref[...]).astype(BF16), wq_ref[...]).astype(BF16)
    kv = _dot((xs * gkv_ref[...]).astype(BF16), wkv_ref[...])
    half = kv.shape[1] // 2
    v_ref[...] = kv[:, half:].astype(BF16)
    for hd in range(half // LANES):
        sl = slice(hd * LANES, (hd + 1) * LANES)
        k = kv[:, sl]
        k_ref[:, sl] = (k * _rms_inv(k) * kg_ref[...]).astype(BF16)


def _swa_proj(h, q_norm_g, kv_norm_g, w_q, w_kv, k_norm):
    t, d = h.shape
    hd = SWA_HEAD_DIM
    kvh = w_kv.shape[1] // (2 * hd)
    rows = SWA_PROJ_ROWS
    rep = LANES // hd
    w_dup = jnp.broadcast_to(w_kv.astype(BF16).reshape(d, 2 * kvh, 1, hd), (d, 2 * kvh, rep, hd))
    w_dup = w_dup.reshape(d, 2 * kvh * LANES)
    kg = jnp.tile(k_norm, rep).reshape(1, LANES)
    const = lambda shape: pl.BlockSpec(shape, lambda i: (0, 0))
    return pl.pallas_call(
        _swa_proj_kernel,
        out_shape=(
            jax.ShapeDtypeStruct((t, w_q.shape[1]), BF16),
            jax.ShapeDtypeStruct((t, kvh * LANES), BF16),
            jax.ShapeDtypeStruct((t, kvh * LANES), BF16),
        ),
        grid=(t // rows,),
        in_specs=[
            pl.BlockSpec((rows, d), lambda i: (i, 0)),
            const((1, d)), const((1, d)),
            const((d, w_q.shape[1])), const((d, 2 * kvh * LANES)), const((1, LANES)),
        ],
        out_specs=(
            pl.BlockSpec((rows, w_q.shape[1]), lambda i: (i, 0)),
            pl.BlockSpec((rows, kvh * LANES), lambda i: (i, 0)),
            pl.BlockSpec((rows, kvh * LANES), lambda i: (i, 0)),
        ),
        compiler_params=_params("parallel"),
        name="swa_proj",
    )(h, q_norm_g.reshape(1, d), kv_norm_g.reshape(1, d), w_q.astype(BF16), w_dup, kg)


def _swa_attn_kernel(q_ref, kp_ref, kc_ref, vp_ref, vc_ref, bias_ref, sink_ref, qg_ref, blk_ref,
                     o_ref, *, kvh, pairs, window):
    first_block = pl.program_id(1) == 0
    hd = LANES // 2
    lane = lax.broadcasted_iota(jnp.int32, (window, LANES), 1)
    low = lane < hd
    width = 2 * window
    scol = lax.broadcasted_iota(jnp.int32, (pairs * window, 2 * width), 1)
    masked_prev = jnp.logical_and((scol & window) == 0, first_block)
    olane = lax.broadcasted_iota(jnp.int32, (pairs * window, LANES), 1)
    zero = jnp.zeros((), BF16)

    def block_diag(prev, cur):
        return jnp.concatenate([jnp.where(low, prev, zero), jnp.where(low, cur, zero),
                                jnp.where(low, zero, prev), jnp.where(low, zero, cur)], axis=0)

    for h in range(kvh):
        ksl = slice(h * LANES, (h + 1) * LANES)
        kk = block_diag(kp_ref[:, ksl], kc_ref[:, ksl])
        vv = block_diag(vp_ref[:, ksl], vc_ref[:, ksl])
        base = h * pairs * LANES
        q2 = jnp.concatenate([q_ref[:, base + p * LANES:base + (p + 1) * LANES]
                              for p in range(pairs)], axis=0).astype(F32)
        ms = _dot((q2 * q2).astype(BF16), blk_ref[...])
        qn = (q2 * lax.rsqrt(ms + RMS_EPS) * qg_ref[...]).astype(BF16)
        s = _dot_nt(qn, kk) + bias_ref[h]
        s = jnp.where(masked_prev, -jnp.inf, s)
        sink = sink_ref[h]
        probs = []
        rden = []
        for half in range(2):
            sh = s[:, half * width:(half + 1) * width]
            sk = sink[:, half:half + 1]
            m = jnp.maximum(jnp.max(sh, axis=-1, keepdims=True), sk)
            p = jnp.exp(sh - m)
            probs.append(p.astype(BF16))
            rden.append(1.0 / (jnp.sum(p, axis=-1, keepdims=True) + jnp.exp(sk - m)))
        o2 = _dot(jnp.concatenate(probs, axis=1), vv) * jnp.where(olane < hd, rden[0], rden[1])
        for p in range(pairs):
            o_ref[:, base + p * LANES:base + (p + 1) * LANES] = (
                o2[p * window:(p + 1) * window].astype(BF16))


def _t5_bucket(dist):
    max_exact = NUM_BUCKETS // 2
    n = np.maximum(dist, 0)
    large = max_exact + (np.log(np.maximum(n, max_exact) / max_exact)
                         / math.log(REL_MAX_DISTANCE / max_exact)
                         * (NUM_BUCKETS - max_exact)).astype(np.int32)
    return np.where(n < max_exact, n, np.minimum(large, NUM_BUCKETS - 1)).astype(np.int32)


def _swa_attn(q, kd, vd, rel_bias, q_norm, sinks, *, batch):
    t, dq = q.shape
    kvh = kd.shape[1] // LANES
    hq = dq // SWA_HEAD_DIM
    pairs = hq // kvh // 2
    w = WINDOW
    nblk = t // batch // w
    row = np.arange(w)[:, None]
    col = np.arange(2 * w)[None, :]
    dist = w + row - col
    band = (dist >= 0) & (dist < w)
    bias = rel_bias.astype(F32)[_t5_bucket(dist)]
    bias = jnp.where(jnp.asarray(band)[..., None], bias, -jnp.inf)
    bias = bias.transpose(2, 0, 1).reshape(kvh, pairs, 2, w, 2 * w).transpose(0, 1, 3, 2, 4)
    bias = bias.reshape(kvh, pairs * w, 4 * w)
    sink = jnp.broadcast_to(sinks.astype(F32).reshape(kvh, pairs, 1, 2), (kvh, pairs, w, 2))
    sink = sink.reshape(kvh, pairs * w, 2)
    qg = (jnp.tile(q_norm, LANES // SWA_HEAD_DIM) * SWA_HEAD_DIM ** -0.5).reshape(1, LANES)
    lane = np.arange(LANES)
    blk = ((lane[:, None] // SWA_HEAD_DIM) == (lane[None, :] // SWA_HEAD_DIM)) / SWA_HEAD_DIM
    cur = lambda b, n: (b * nblk + n, 0)
    prev = lambda b, n: (b * nblk + jnp.maximum(n - 1, 0), 0)
    const3 = lambda shape: pl.BlockSpec(shape, lambda b, n: (0, 0, 0))
    return pl.pallas_call(
        functools.partial(_swa_attn_kernel, kvh=kvh, pairs=pairs, window=w),
        out_shape=jax.ShapeDtypeStruct((t, dq), BF16),
        grid=(batch, nblk),
        in_specs=[
            pl.BlockSpec((w, dq), cur),
            pl.BlockSpec((w, kvh * LANES), prev), pl.BlockSpec((w, kvh * LANES), cur),
            pl.BlockSpec((w, kvh * LANES), prev), pl.BlockSpec((w, kvh * LANES), cur),
            const3((kvh, pairs * w, 4 * w)), const3((kvh, pairs * w, 2)),
            pl.BlockSpec((1, LANES), lambda b, n: (0, 0)),
            pl.BlockSpec((LANES, LANES), lambda b, n: (0, 0)),
        ],
        out_specs=pl.BlockSpec((w, dq), cur),
        compiler_params=_params("parallel", "parallel"),
        name="swa_attn",
    )(q, kd, kd, vd, vd, bias, sink, qg, jnp.asarray(blk, BF16))


def _swa_layer(h, batch, kv_norm, w_kv, k_norm, rel_bias, norm_g, w_q, q_norm, sinks, w_out):
    q, kd, vd = _swa_proj(h, norm_g, kv_norm, w_q, w_kv, k_norm)
    o = _swa_attn(q, kd, vd, rel_bias, q_norm, sinks, batch=batch)
    return _matmul_residual(o, w_out.astype(BF16), h, rows=PROJ_ROWS)


def kernel(x, gla_norm, gla_w_in, gla_w_gate_up, gla_b_gate, gla_head_norm, gla_w_out, kv_norm, w_kv, k_norm, rel_bias, swa_norm, swa_w_q, swa_q_norm, swa_sinks, swa_w_out, moe_norm, moe_w_group, moe_b_group, moe_w_router, moe_b_router, moe_w_gate, moe_w_up, moe_w_down):
    batch, seq, d = x.shape
    assert gla_norm.shape[0] == 1 and swa_norm.shape[0] == 1 and moe_norm.shape[0] == 2
    h = x.reshape(batch * seq, d)
    h = _gla_layer(h, batch, gla_norm[0], gla_w_in[0], gla_w_gate_up[0], gla_b_gate[0],
                   gla_head_norm[0], gla_w_out[0])
    h = _moe_layer(h, moe_norm[0], moe_w_group[0], moe_b_group[0], moe_w_router[0],
                   moe_b_router[0], moe_w_gate[0], moe_w_up[0], moe_w_down[0])
    h = _swa_layer(h, batch, kv_norm, w_kv, k_norm, rel_bias, swa_norm[0], swa_w_q[0],
                   swa_q_norm[0], swa_sinks[0], swa_w_out[0])
    h = _moe_layer(h, moe_norm[1], moe_w_group[1], moe_b_group[1], moe_w_router[1],
                   moe_b_router[1], moe_w_gate[1], moe_w_up[1], moe_w_down[1])
    return h.reshape(batch, seq, d)
```

```python
import functools
import math

import jax
import jax.numpy as jnp
import numpy as np
from jax import lax
from jax.experimental import pallas as pl
from jax.experimental.pallas import tpu as pltpu

F32 = jnp.float32
BF16 = jnp.bfloat16

RMS_EPS = 1e-6
GLA_HEADS = 4
GLA_GATE_RANK = 16
GLA_GATE_NORMALIZER = 16.0
GLA_LOG_GATE_MIN = -1.0
GLA_CHUNK = 64
SWA_HEAD_DIM = 64
SWA_GROUPS = 8
WINDOW = 128
NUM_BUCKETS = 32
REL_MAX_DISTANCE = 128
MOE_GROUPS = 4
MOE_EXPERTS_PER_GROUP = 8

LANES = 128
SUBLANES = 8
VMEM_LIMIT_BYTES = 56 * 1024 * 1024

PREP_ROWS = 256
PROJ_ROWS = 512
ROUTER_ROWS = 512
EXPERT_ROWS = 512
COMBINE_ROWS = 256
SWA_PROJ_ROWS = 256
LOG2_E = math.log2(math.e)
DMA_UNROLL = 8
INVERSE_STEPS = 64


def _params(*semantics):
    return pltpu.CompilerParams(dimension_semantics=semantics, vmem_limit_bytes=VMEM_LIMIT_BYTES)


def _dot(a, b):
    return jnp.dot(a, b, preferred_element_type=F32)


def _dot_nt(a, b):
    return lax.dot_general(a, b, (((1,), (1,)), ((), ())), preferred_element_type=F32)


def _split(x):
    hi = x.astype(BF16)
    lo = (x - hi.astype(F32)).astype(BF16)
    return hi, lo


def _dot3(a, b, dot=_dot):
    ah, al = _split(a)
    bh, bl = _split(b)
    return dot(ah, bh) + dot(al, bh) + dot(ah, bl)


def _rms_inv(x):
    return lax.rsqrt(jnp.mean(x * x, axis=-1, keepdims=True) + RMS_EPS)


def _sigmoid(x):
    return 1.0 / (1.0 + jnp.exp(-x))


def _log_sigmoid(x):
    return jnp.minimum(x, 0.0) - jnp.log1p(jnp.exp(-jnp.abs(x)))


def _norm_matmul_kernel(x_ref, g_ref, w_ref, o_ref, xn_ref):
    @pl.when(pl.program_id(1) == 0)
    def _():
        x = x_ref[...]
        xn_ref[...] = (x * _rms_inv(x) * g_ref[...]).astype(BF16)

    o_ref[...] = _dot(xn_ref[...], w_ref[...]).astype(o_ref.dtype)


def _norm_matmul(x, g, w, *, rows, cols, out_dtype):
    t, d = x.shape
    n = w.shape[1]
    return pl.pallas_call(
        _norm_matmul_kernel,
        out_shape=jax.ShapeDtypeStruct((t, n), out_dtype),
        grid=(t // rows, n // cols),
        in_specs=[
            pl.BlockSpec((rows, d), lambda i, j: (i, 0)),
            pl.BlockSpec((1, d), lambda i, j: (0, 0)),
            pl.BlockSpec((d, cols), lambda i, j: (0, j)),
        ],
        out_specs=pl.BlockSpec((rows, cols), lambda i, j: (i, j)),
        scratch_shapes=[pltpu.VMEM((rows, d), BF16)],
        compiler_params=_params("parallel", "arbitrary"),
        name="norm_matmul",
    )(x, g.reshape(1, d), w)


def _matmul_residual_kernel(a_ref, w_ref, res_ref, o_ref):
    o_ref[...] = res_ref[...] + _dot(a_ref[...], w_ref[...])


def _matmul_residual(a, w, res, *, rows):
    t, k = a.shape
    n = w.shape[1]
    return pl.pallas_call(
        _matmul_residual_kernel,
        out_shape=jax.ShapeDtypeStruct((t, n), F32),
        grid=(t // rows,),
        in_specs=[
            pl.BlockSpec((rows, k), lambda i: (i, 0)),
            pl.BlockSpec((k, n), lambda i: (0, 0)),
            pl.BlockSpec((rows, n), lambda i: (i, 0)),
        ],
        out_specs=pl.BlockSpec((rows, n), lambda i: (i, 0)),
        compiler_params=_params("parallel"),
        name="matmul_residual",
    )(a, w, res)


def _pair_constants(chunk):
    pair = 2 * chunk
    i = np.arange(pair)[:, None]
    j = np.arange(pair)[None, :]
    tri = (j <= i).astype(np.float32)
    sel = ((i >= chunk) & (j < chunk)).astype(np.float32)
    blk = ((i // chunk) == (j // chunk)).astype(np.float32)
    ones = np.ones((pair, pair), np.float32)
    c_row = np.concatenate([tri, sel], axis=0)
    c_col = np.concatenate([tri.T, sel.T, blk, ones], axis=1)
    return jnp.asarray(c_row, BF16), jnp.asarray(c_col, BF16)


def _gla_prep_kernel(x_ref, g_ref, wq_ref, wkt_ref, wg_ref, wgu_ref, wgut_ref, bgr_ref, bgc_ref,
                     crow_ref, ccol_ref, qe_ref, qes_ref, ket_ref, klt_ref, ktt_ref, dect_ref,
                     *, scale, pair):
    x = x_ref[...]
    xn = (x * _rms_inv(x) * g_ref[...]).astype(BF16)
    q = _dot(xn, wq_ref[...]) * scale
    kt = _dot_nt(wkt_ref[...], xn)
    glr = _dot(xn, wg_ref[...])
    z = _dot3(glr, wgu_ref[...]) + bgr_ref[...]
    zt = _dot3(wgut_ref[...], glr, dot=_dot_nt) + bgc_ref[...]
    la = jnp.maximum(_log_sigmoid(z) / GLA_GATE_NORMALIZER, GLA_LOG_GATE_MIN)
    lat = jnp.maximum(_log_sigmoid(zt) / GLA_GATE_NORMALIZER, GLA_LOG_GATE_MIN)
    crow = crow_ref[...]
    ccol = ccol_ref[...]
    for p in range(x.shape[0] // pair):
        rows = slice(p * pair, (p + 1) * pair)
        hi, lo = _split(la[rows])
        cr = _dot(crow, hi) + _dot(crow, lo)
        b_pair = cr[:pair]
        b_chunk = b_pair - cr[pair:]
        qp = q[rows]
        qe_ref[rows, :] = (qp * jnp.exp(b_chunk)).astype(BF16)
        qes_ref[rows, :] = (qp * jnp.exp(b_pair)).astype(BF16)
        hi, lo = _split(lat[:, rows])
        cc = _dot(hi, ccol) + _dot(lo, ccol)
        bt_pair = cc[:, :pair]
        bt_chunk = bt_pair - cc[:, pair:2 * pair]
        end_chunk = cc[:, 2 * pair:3 * pair]
        end_pair = cc[:, 3 * pair:]
        ktp = kt[:, rows]
        ket_ref[:, rows] = (ktp * jnp.exp(-bt_chunk)).astype(BF16)
        klt_ref[:, rows] = (ktp * jnp.exp(end_chunk - bt_chunk)).astype(BF16)
        ktt_ref[:, rows] = (ktp * jnp.exp(end_pair - bt_pair)).astype(BF16)
        dect_ref[:, rows] = jnp.exp(end_pair)


def _gla_prep(h, norm_g, w_in, w_gate_up, b_gate):
    t, d = h.shape
    hk = d // 2
    rank = w_gate_up.shape[0]
    pair = 2 * GLA_CHUNK
    rows = PREP_ROWS
    dk = hk // GLA_HEADS
    wq = w_in[:, :hk].astype(BF16)
    wkt = w_in[:, hk:2 * hk].T.astype(BF16)
    wg = jnp.pad(w_in[:, 3 * d:], ((0, 0), (0, LANES - rank))).astype(BF16)
    wgu = jnp.pad(w_gate_up, ((0, LANES - rank), (0, 0)))
    c_row, c_col = _pair_constants(GLA_CHUNK)
    const = lambda shape: pl.BlockSpec(shape, lambda i: (0, 0))
    row_out = pl.BlockSpec((rows, hk), lambda i: (i, 0))
    col_out = pl.BlockSpec((hk, rows), lambda i: (0, i))
    return pl.pallas_call(
        functools.partial(_gla_prep_kernel, scale=dk ** -0.5, pair=pair),
        out_shape=(
            jax.ShapeDtypeStruct((t, hk), BF16),
            jax.ShapeDtypeStruct((t, hk), BF16),
            jax.ShapeDtypeStruct((hk, t), BF16),
            jax.ShapeDtypeStruct((hk, t), BF16),
            jax.ShapeDtypeStruct((hk, t), BF16),
            jax.ShapeDtypeStruct((hk, t), F32),
        ),
        grid=(t // rows,),
        in_specs=[
            pl.BlockSpec((rows, d), lambda i: (i, 0)),
            const((1, d)),
            const((d, hk)),
            const((hk, d)),
            const((d, LANES)),
            const((LANES, hk)),
            const((hk, LANES)),
            const((1, hk)),
            const((hk, 1)),
            const((2 * pair, pair)),
            const((pair, 4 * pair)),
        ],
        out_specs=(row_out, row_out, col_out, col_out, col_out, col_out),
        compiler_params=_params("parallel"),
        name="gla_prep",
    )(h, norm_g.reshape(1, d), wq, wkt, wg, wgu, wgu.T, b_gate.reshape(1, hk),
      b_gate.reshape(hk, 1), c_row, c_col)


def _gla_core_kernel(qe_ref, qes_ref, ket_ref, klt_ref, ktt_ref, dect_ref, v_ref, r_ref, hg_ref,
                     o_ref, s_ref, *, heads, chunk):
    @pl.when(pl.program_id(1) == 0)
    def _():
        s_ref[...] = jnp.zeros_like(s_ref)

    pair = 2 * chunk
    dk = qe_ref.shape[1] // heads
    dv = v_ref.shape[1] // heads
    row = lax.broadcasted_iota(jnp.int32, (pair, pair), 0)
    col = lax.broadcasted_iota(jnp.int32, (pair, pair), 1)
    second = row >= chunk
    first_keys = col < chunk
    m_intra = jnp.logical_and(col <= row, jnp.logical_not(jnp.logical_xor(second, col >= chunk)))
    m_cross = jnp.logical_and(second, first_keys)
    for h in range(heads):
        ks = slice(h * dk, (h + 1) * dk)
        vs = slice(h * dv, (h + 1) * dv)
        qe = qe_ref[:, ks]
        v = v_ref[:, vs]
        att = jnp.where(m_intra, _dot(qe, ket_ref[ks, :]),
                        jnp.where(m_cross, _dot(qe, klt_ref[ks, :]), 0.0))
        state = s_ref[h]
        o = _dot(att.astype(BF16), v) + _dot(qes_ref[:, ks], state.astype(BF16))
        dec = dect_ref[ks, :]
        dec = jnp.concatenate([dec] * (dv // pair), axis=1) if dv > pair else dec[:, :dv]
        s_ref[h] = dec * state + _dot(ktt_ref[ks, :], v)
        on = o * _rms_inv(o) * hg_ref[...]
        r = r_ref[:, vs].astype(F32)
        o_ref[:, vs] = (on * (r * _sigmoid(r))).astype(BF16)


def _gla_core(qe, qes, ket, klt, ktt, dect, vr, head_g, *, batch):
    t, hk = qe.shape
    hv = vr.shape[1] // 2
    pair = 2 * GLA_CHUNK
    npair = t // batch // pair
    dk = hk // GLA_HEADS
    dv = hv // GLA_HEADS
    row_k = pl.BlockSpec((pair, hk), lambda b, p: (b * npair + p, 0))
    col_k = pl.BlockSpec((hk, pair), lambda b, p: (0, b * npair + p))
    return pl.pallas_call(
        functools.partial(_gla_core_kernel, heads=GLA_HEADS, chunk=GLA_CHUNK),
        out_shape=jax.ShapeDtypeStruct((t, hv), BF16),
        grid=(batch, npair),
        in_specs=[
            row_k, row_k, col_k, col_k, col_k, col_k,
            pl.BlockSpec((pair, hv), lambda b, p: (b * npair + p, 0)),
            pl.BlockSpec((pair, hv), lambda b, p: (b * npair + p, 1)),
            pl.BlockSpec((1, dv), lambda b, p: (0, 0)),
        ],
        out_specs=pl.BlockSpec((pair, hv), lambda b, p: (b * npair + p, 0)),
        scratch_shapes=[pltpu.VMEM((GLA_HEADS, dk, dv), F32)],
        compiler_params=_params("parallel", "arbitrary"),
        name="gla_core",
    )(qe, qes, ket, klt, ktt, dect, vr, vr, head_g.reshape(1, dv))


def _gla_layer(h, batch, norm_g, w_in, w_gate_up, b_gate, head_g, w_out):
    d = h.shape[1]
    hk = d // 2
    qe, qes, ket, klt, ktt, dect = _gla_prep(h, norm_g, w_in, w_gate_up, b_gate)
    vr = _norm_matmul(h, norm_g, w_in[:, 2 * hk:2 * hk + 2 * d].astype(BF16),
                      rows=PROJ_ROWS, cols=d, out_dtype=BF16)
    og = _gla_core(qe, qes, ket, klt, ktt, dect, vr, head_g, batch=batch)
    return _matmul_residual(og, w_out.astype(BF16), h, rows=PROJ_ROWS)


META_E, META_W, META_RANK = 0, 2, 4


def _router_kernel(h_ref, g_ref, w_ref, b_ref, tril_ref, meta_ref, cnt_ref, base_ref,
                   *, groups, per_group):
    @pl.when(pl.program_id(0) == 0)
    def _():
        base_ref[...] = jnp.zeros_like(base_ref)

    x = h_ref[...]
    xn = x * _rms_inv(x) * g_ref[...]
    logits = _dot3(xn, w_ref[...]) + b_ref[...]
    lane = lax.broadcasted_iota(jnp.int32, logits.shape, 1).astype(F32)
    neg = -jnp.inf
    far = float(LANES)

    def first_max(vals):
        m = jnp.max(vals, axis=-1, keepdims=True)
        return m, jnp.min(jnp.where(vals == m, lane, far), axis=-1, keepdims=True)

    gl = jnp.where(lane < groups, logits, neg)
    gmax, gidx = first_max(gl)
    p_group = 1.0 / jnp.sum(jnp.exp(gl - gmax), axis=-1, keepdims=True)
    lo = groups + per_group * gidx
    el = jnp.where(jnp.logical_and(lane >= lo, lane < lo + per_group), logits, neg)
    v1, i1 = first_max(el)
    v2, i2 = first_max(jnp.where(lane == i1, neg, el))
    t = jnp.exp(v2 - v1)
    w1 = p_group / (1.0 + t)
    w2 = p_group * t / (1.0 + t)

    oh1 = lane == i1
    oh2 = lane == i2
    onehot = jnp.where(jnp.logical_or(oh1, oh2), 1.0, 0.0).astype(BF16)
    seen = base_ref[...] + _dot(tril_ref[...], onehot)
    rank1 = jnp.sum(jnp.where(oh1, seen, 0.0), axis=-1, keepdims=True) - 1.0
    rank2 = jnp.sum(jnp.where(oh2, seen, 0.0), axis=-1, keepdims=True) - 1.0
    base_ref[...] = seen[-1:, :]
    cnt_ref[...] = jnp.broadcast_to(seen[-1:, :], cnt_ref.shape)

    rec = jnp.zeros_like(logits)
    for k, val in ((META_E, i1 - groups), (META_E + 1, i2 - groups), (META_W, w1),
                   (META_W + 1, w2), (META_RANK, rank1), (META_RANK + 1, rank2)):
        rec = jnp.where(lane == k, val, rec)
    meta_ref[...] = rec


def _moe_router(h, norm_g, w_group, b_group, w_router, b_router):
    t, d = h.shape
    rows = ROUTER_ROWS
    groups = w_group.shape[1]
    ne = w_router.shape[1]
    pad = LANES - groups - ne
    w = jnp.pad(jnp.concatenate([w_group, w_router], axis=1), ((0, 0), (0, pad)))
    b = jnp.pad(jnp.concatenate([b_group, b_router]), (0, pad)).reshape(1, LANES)
    tril = jnp.asarray(np.tril(np.ones((rows, rows), np.float32)), BF16)
    meta, cnt = pl.pallas_call(
        functools.partial(_router_kernel, groups=groups, per_group=ne // groups),
        out_shape=(jax.ShapeDtypeStruct((t, LANES), F32), jax.ShapeDtypeStruct((8, LANES), F32)),
        grid=(t // rows,),
        in_specs=[
            pl.BlockSpec((rows, d), lambda i: (i, 0)),
            pl.BlockSpec((1, d), lambda i: (0, 0)),
            pl.BlockSpec((d, LANES), lambda i: (0, 0)),
            pl.BlockSpec((1, LANES), lambda i: (0, 0)),
            pl.BlockSpec((rows, rows), lambda i: (0, 0)),
        ],
        out_specs=(pl.BlockSpec((rows, LANES), lambda i: (i, 0)),
                   pl.BlockSpec((8, LANES), lambda i: (0, 0))),
        scratch_shapes=[pltpu.VMEM((1, LANES), F32)],
        compiler_params=_params("arbitrary"),
        name="moe_router",
    )(h, norm_g.reshape(1, d), w, b, tril)
    return meta, cnt[0, groups:groups + ne]


def _inverse_kernel(pos0_ref, pos1_ref, src_ref):
    step = pl.program_id(0)
    half = pl.num_programs(0) // 2
    slots = src_ref.shape[0] // INVERSE_STEPS
    tokens = pos0_ref.shape[0] // INVERSE_STEPS

    @pl.when(step < half)
    def _():
        def fill(i, carry):
            for u in range(DMA_UNROLL):
                src_ref[step * slots + i * DMA_UNROLL + u] = 0
            return carry

        lax.fori_loop(0, slots // DMA_UNROLL, fill, 0)

    @pl.when(step >= half)
    def _():
        def place(i, carry):
            for u in range(DMA_UNROLL):
                tok = (step - half) * tokens + i * DMA_UNROLL + u
                src_ref[pos0_ref[tok]] = tok
                src_ref[pos1_ref[tok]] = tok
            return carry

        lax.fori_loop(0, tokens // DMA_UNROLL, place, 0)


def _moe_inverse(pos0, pos1, slots):
    smem = pl.BlockSpec(memory_space=pltpu.SMEM)
    return pl.pallas_call(
        _inverse_kernel,
        out_shape=jax.ShapeDtypeStruct((slots,), jnp.int32),
        grid=(2 * INVERSE_STEPS,),
        in_specs=[smem, smem],
        out_specs=smem,
        compiler_params=_params("arbitrary"),
        name="moe_inverse",
    )(pos0, pos1)


def _row_copy(src_hbm, row, dst, r_tile, r_sub, sem):
    return pltpu.make_async_copy(src_hbm.at[pl.ds(row, 1)], dst.at[r_tile, pl.ds(r_sub, 1)], sem)


def _rows_wait(dst, sem):
    pltpu.make_async_copy(dst, dst, sem).wait()


def _experts_kernel(src_ref, texp_ref, nused_ref, h_hbm, g_ref, wg_ref, wu_ref, wd_ref, y_ref,
                    xbuf, xn_ref, wgb, wub, wdb, sem, *, rows):
    i = pl.program_id(0)
    nused = nused_ref[0]

    def start(tile, slot):
        base = tile * rows
        for r in range(rows):
            _row_copy(h_hbm, src_ref[base + r], xbuf.at[slot], r // SUBLANES, r % SUBLANES,
                      sem.at[slot]).start()

    @pl.when(i == 0)
    def _():
        start(0, 0)

    fresh = jnp.logical_or(i == 0, texp_ref[i] != texp_ref[jnp.maximum(i - 1, 0)])

    @pl.when(jnp.logical_and(fresh, i < nused))
    def _():
        wgb[...] = (wg_ref[0, 0] * g_ref[...]).astype(BF16)
        wub[...] = (wu_ref[0, 0] * g_ref[...]).astype(BF16)
        wdb[...] = wd_ref[0, 0].astype(BF16)

    @pl.when(i < nused)
    def _():
        slot = i % 2
        _rows_wait(xbuf.at[slot], sem.at[slot])
        x = xbuf[slot].reshape(xn_ref.shape)
        xn_ref[...] = x.astype(BF16)
        inv = _rms_inv(x)
        start(jnp.minimum(i + 1, nused - 1), 1 - slot)
        xn = xn_ref[...]
        gate = _dot(xn, wgb[...]) * inv
        up = _dot(xn, wub[...]) * inv
        act = (gate * _sigmoid(gate)) * up
        y_ref[...] = _dot(act.astype(BF16), wdb[...])

    @pl.when(i == nused - 1)
    def _():
        slot = nused % 2
        _rows_wait(xbuf.at[slot], sem.at[slot])

    @pl.when(i >= nused)
    def _():
        y_ref[...] = jnp.zeros_like(y_ref)


def _moe_experts(h, norm_g, layer, w_gate, w_up, w_down, src, tile_expert, n_used):
    t, d = h.shape
    _, ne, _, f = w_gate.shape
    rows = EXPERT_ROWS
    slots = src.shape[0]
    wmap = lambda i, src_r, texp_r, nused_r: (layer, texp_r[i], 0, 0)
    return pl.pallas_call(
        functools.partial(_experts_kernel, rows=rows),
        out_shape=jax.ShapeDtypeStruct((slots, d), F32),
        grid_spec=pltpu.PrefetchScalarGridSpec(
            num_scalar_prefetch=3,
            grid=(slots // rows,),
            in_specs=[
                pl.BlockSpec(memory_space=pl.ANY),
                pl.BlockSpec((d, 1), lambda i, *_: (0, 0)),
                pl.BlockSpec((1, 1, d, f), wmap),
                pl.BlockSpec((1, 1, d, f), wmap),
                pl.BlockSpec((1, 1, f, d), wmap),
            ],
            out_specs=pl.BlockSpec((rows, d), lambda i, *_: (i, 0)),
            scratch_shapes=[pltpu.VMEM((2, rows // SUBLANES, SUBLANES, d), F32),
                            pltpu.VMEM((rows, d), BF16), pltpu.VMEM((d, f), BF16),
                            pltpu.VMEM((d, f), BF16), pltpu.VMEM((f, d), BF16),
                            pltpu.SemaphoreType.DMA((2,))],
        ),
        compiler_params=_params("arbitrary"),
        name="moe_experts",
    )(src, tile_expert, n_used, h, norm_g.reshape(d, 1), w_gate, w_up, w_down)


def _combine_kernel(pos0_ref, pos1_ref, h_ref, meta_ref, y_hbm, o_ref, buf0, buf1, sem, *, rows):
    i = pl.program_id(0)

    def start(tile):
        slot = tile % 2

        def body(g, carry):
            for u in range(SUBLANES):
                tok = tile * rows + g * SUBLANES + u
                _row_copy(y_hbm, pos0_ref[tok], buf0.at[slot], g, u, sem.at[0, slot]).start()
                _row_copy(y_hbm, pos1_ref[tok], buf1.at[slot], g, u, sem.at[1, slot]).start()
            return carry

        lax.fori_loop(0, rows // SUBLANES, body, 0)

    @pl.when(i == 0)
    def _():
        start(0)

    @pl.when(i + 1 < pl.num_programs(0))
    def _():
        start(i + 1)

    slot = i % 2
    _rows_wait(buf0.at[slot], sem.at[0, slot])
    _rows_wait(buf1.at[slot], sem.at[1, slot])
    meta = meta_ref[...]
    w0 = meta[:, META_W:META_W + 1]
    w1 = meta[:, META_W + 1:META_W + 2]
    shape = h_ref.shape
    o_ref[...] = h_ref[...] + w0 * buf0[slot].reshape(shape) + w1 * buf1[slot].reshape(shape)


def _moe_combine(h, meta, y, pos0, pos1):
    t, d = h.shape
    rows = COMBINE_ROWS
    gather_buf = pltpu.VMEM((2, rows // SUBLANES, SUBLANES, d), F32)
    return pl.pallas_call(
        functools.partial(_combine_kernel, rows=rows),
        out_shape=jax.ShapeDtypeStruct((t, d), F32),
        grid_spec=pltpu.PrefetchScalarGridSpec(
            num_scalar_prefetch=2,
            grid=(t // rows,),
            in_specs=[
                pl.BlockSpec((rows, d), lambda i, *_: (i, 0)),
                pl.BlockSpec((rows, LANES), lambda i, *_: (i, 0)),
                pl.BlockSpec(memory_space=pl.ANY),
            ],
            out_specs=pl.BlockSpec((rows, d), lambda i, *_: (i, 0)),
            scratch_shapes=[gather_buf, gather_buf, pltpu.SemaphoreType.DMA((2, 2))],
        ),
        compiler_params=_params("arbitrary"),
        name="moe_combine",
    )(pos0, pos1, h, meta, y)


def _moe_layer(h, layer, norm_g, w_group, b_group, w_router, b_router, w_gate, w_up, w_down):
    t, _ = h.shape
    ne = w_router.shape[1]
    rows = EXPERT_ROWS
    meta, counts = _moe_router(h, norm_g, w_group, b_group, w_router, b_router)
    counts = counts.astype(jnp.int32)
    padded = (counts + rows - 1) // rows * rows
    ends = jnp.cumsum(padded)
    starts = ends - padded
    expert = meta[:, META_E:META_E + 2].astype(jnp.int32)
    rank = meta[:, META_RANK:META_RANK + 2].astype(jnp.int32)
    pos = starts[expert] + rank
    slots = 2 * t + ne * rows
    n_tiles = slots // rows
    n_used = (ends[-1] // rows).astype(jnp.int32)
    tile_start = jnp.arange(n_tiles, dtype=jnp.int32) * rows
    tile_start = jnp.minimum(tile_start, ends[-1] - 1)
    tile_expert = jnp.sum(tile_start[:, None] >= ends[None, :], axis=1, dtype=jnp.int32)
    tile_expert = jnp.minimum(tile_expert, ne - 1)
    pos0 = pos[:, 0]
    pos1 = pos[:, 1]
    src = _moe_inverse(pos0, pos1, slots)
    y = _moe_experts(h, norm_g, layer, w_gate, w_up, w_down, src, tile_expert, n_used.reshape(1))
    return _moe_combine(h, meta, y, pos0, pos1)


def _swa_proj_kernel(x_ref, gq_ref, gkv_ref, wq_ref, wkv_ref, kg_ref, q_ref, k_ref, v_ref):
    x = x_ref[...]
    xs = x * _rms_inv(x)
    q_ref[...] = _dot((xs * gq_ref[...]).astype(BF16), wq_ref[...]).astype(BF16)
    kv = _dot((xs * gkv_ref[...]).astype(BF16), wkv_ref[...])
    half = kv.shape[1] // 2
    v_ref[...] = kv[:, half:].astype(BF16)
    for hd in range(half // LANES):
        sl = slice(hd * LANES, (hd + 1) * LANES)
        k = kv[:, sl]
        k_ref[:, sl] = (k * _rms_inv(k) * kg_ref[...]).astype(BF16)


def _swa_proj(h, q_norm_g, kv_norm_g, w_q, w_kv, k_norm):
    t, d = h.shape
    hd = SWA_HEAD_DIM
    kvh = w_kv.shape[1] // (2 * hd)
    rows = SWA_PROJ_ROWS
    rep = LANES // hd
    w_dup = jnp.broadcast_to(w_kv.astype(BF16).reshape(d, 2 * kvh, 1, hd), (d, 2 * kvh, rep, hd))
    w_dup = w_dup.reshape(d, 2 * kvh * LANES)
    kg = jnp.tile(k_norm, rep).reshape(1, LANES)
    const = lambda shape: pl.BlockSpec(shape, lambda i: (0, 0))
    return pl.pallas_call(
        _swa_proj_kernel,
        out_shape=(
            jax.ShapeDtypeStruct((t, w_q.shape[1]), BF16),
            jax.ShapeDtypeStruct((t, kvh * LANES), BF16),
            jax.ShapeDtypeStruct((t, kvh * LANES), BF16),
        ),
        grid=(t // rows,),
        in_specs=[
            pl.BlockSpec((rows, d), lambda i: (i, 0)),
            const((1, d)), const((1, d)),
            const((d, w_q.shape[1])), const((d, 2 * kvh * LANES)), const((1, LANES)),
        ],
        out_specs=(
            pl.BlockSpec((rows, w_q.shape[1]), lambda i: (i, 0)),
            pl.BlockSpec((rows, kvh * LANES), lambda i: (i, 0)),
            pl.BlockSpec((rows, kvh * LANES), lambda i: (i, 0)),
        ),
        compiler_params=_params("parallel"),
        name="swa_proj",
    )(h, q_norm_g.reshape(1, d), kv_norm_g.reshape(1, d), w_q.astype(BF16), w_dup, kg)


def _swa_attn_kernel(q_ref, kp_ref, kc_ref, vp_ref, vc_ref, bias_ref, sinkw_ref, qg_ref,
                     blk_ref, ones_ref, o_ref, *, kvh, pairs, window):
    hd = LANES // 2
    lane = lax.broadcasted_iota(jnp.int32, (window, LANES), 1)
    low = lane < hd
    width = 2 * window
    olow = lax.broadcasted_iota(jnp.int32, (pairs * window, LANES), 1) < hd
    zero = jnp.zeros((), BF16)

    def block_diag(prev, cur):
        return jnp.concatenate([jnp.where(low, prev, zero), jnp.where(low, cur, zero),
                                jnp.where(low, zero, prev), jnp.where(low, zero, cur)], axis=0)

    for h in range(kvh):
        ksl = slice(h * LANES, (h + 1) * LANES)
        kk = block_diag(kp_ref[:, ksl], kc_ref[:, ksl])
        vv = block_diag(vp_ref[:, ksl], vc_ref[:, ksl])
        base = h * pairs * LANES
        q2 = jnp.concatenate([q_ref[:, base + p * LANES:base + (p + 1) * LANES]
                              for p in range(pairs)], axis=0).astype(F32)
        ms = _dot((q2 * q2).astype(BF16), blk_ref[...])
        qn = (q2 * lax.rsqrt(ms + RMS_EPS) * qg_ref[...]).astype(BF16)
        s = _dot_nt(qn, kk) + bias_ref[0, h]
        probs = []
        row_max = []
        for half in range(2):
            sh = s[:, half * width:(half + 1) * width]
            m = jnp.max(sh, axis=-1, keepdims=True)
            probs.append(jnp.exp2(sh - m).astype(BF16))
            row_max.append(m)
        mixed = _dot(jnp.concatenate(probs, axis=1), jnp.concatenate([vv, ones_ref[...]], axis=1))
        den = mixed[:, LANES:] + jnp.exp2(sinkw_ref[h] - jnp.where(olow, row_max[0], row_max[1]))
        o2 = mixed[:, :LANES] * (1.0 / den)
        for p in range(pairs):
            o_ref[:, base + p * LANES:base + (p + 1) * LANES] = (
                o2[p * window:(p + 1) * window].astype(BF16))


def _t5_bucket(dist):
    max_exact = NUM_BUCKETS // 2
    n = np.maximum(dist, 0)
    large = max_exact + (np.log(np.maximum(n, max_exact) / max_exact)
                         / math.log(REL_MAX_DISTANCE / max_exact)
                         * (NUM_BUCKETS - max_exact)).astype(np.int32)
    return np.where(n < max_exact, n, np.minimum(large, NUM_BUCKETS - 1)).astype(np.int32)


def _swa_attn(q, kd, vd, rel_bias, q_norm, sinks, *, batch):
    t, dq = q.shape
    kvh = kd.shape[1] // LANES
    hq = dq // SWA_HEAD_DIM
    pairs = hq // kvh // 2
    w = WINDOW
    nblk = t // batch // w
    row = np.arange(w)[:, None]
    col = np.arange(2 * w)[None, :]
    dist = w + row - col
    band = (dist >= 0) & (dist < w)
    onehot = (_t5_bucket(dist)[..., None] == np.arange(NUM_BUCKETS)).astype(np.float32)
    bias = jnp.dot(jnp.asarray(onehot), rel_bias.astype(F32), precision=lax.Precision.HIGHEST)
    bias = jnp.where(jnp.asarray(band)[..., None], bias, -jnp.inf)
    bias = bias.transpose(2, 0, 1).reshape(kvh, pairs, 2, w, 2 * w).transpose(0, 1, 3, 2, 4)
    bias = bias.reshape(kvh, pairs * w, 4 * w) * LOG2_E
    prev_cols = jnp.asarray((np.arange(4 * w) // w) % 2 == 0)
    bias = jnp.stack([bias, jnp.where(prev_cols, -jnp.inf, bias)])
    sink = sinks.astype(F32).reshape(kvh, pairs, 1, 2, 1) * LOG2_E
    sink_wide = jnp.broadcast_to(sink, (kvh, pairs, w, 2, SWA_HEAD_DIM))
    sink_wide = sink_wide.reshape(kvh, pairs * w, LANES)
    qg = jnp.tile(q_norm, LANES // SWA_HEAD_DIM) * (SWA_HEAD_DIM ** -0.5 * LOG2_E)
    lane = np.arange(LANES)
    blk = ((lane[:, None] // SWA_HEAD_DIM) == (lane[None, :] // SWA_HEAD_DIM)) / SWA_HEAD_DIM
    ones = (np.arange(4 * w)[:, None] // (2 * w)) == (lane[None, :] // SWA_HEAD_DIM)
    cur = lambda b, n: (b * nblk + n, 0)
    prev = lambda b, n: (b * nblk + jnp.maximum(n - 1, 0), 0)
    const2 = lambda shape: pl.BlockSpec(shape, lambda b, n: (0, 0))
    const3 = lambda shape: pl.BlockSpec(shape, lambda b, n: (0, 0, 0))
    return pl.pallas_call(
        functools.partial(_swa_attn_kernel, kvh=kvh, pairs=pairs, window=w),
        out_shape=jax.ShapeDtypeStruct((t, dq), BF16),
        grid=(batch, nblk),
        in_specs=[
            pl.BlockSpec((w, dq), cur),
            pl.BlockSpec((w, kvh * LANES), prev), pl.BlockSpec((w, kvh * LANES), cur),
            pl.BlockSpec((w, kvh * LANES), prev), pl.BlockSpec((w, kvh * LANES), cur),
            pl.BlockSpec((1, kvh, pairs * w, 4 * w), lambda b, n: (jnp.where(n == 0, 1, 0), 0, 0, 0)),
            const3((kvh, pairs * w, LANES)),
            const2((1, LANES)), const2((LANES, LANES)), const2((4 * w, LANES)),
        ],
        out_specs=pl.BlockSpec((w, dq), cur),
        compiler_params=_params("parallel", "parallel"),
        name="swa_attn",
    )(q, kd, kd, vd, vd, bias, sink_wide, qg.reshape(1, LANES), jnp.asarray(blk, BF16),
      jnp.asarray(ones, BF16))


def _swa_layer(h, batch, kv_norm, w_kv, k_norm, rel_bias, norm_g, w_q, q_norm, sinks, w_out):
    q, kd, vd = _swa_proj(h, norm_g, kv_norm, w_q, w_kv, k_norm)
    o = _swa_attn(q, kd, vd, rel_bias, q_norm, sinks, batch=batch)
    return _matmul_residual(o, w_out.astype(BF16), h, rows=PROJ_ROWS)


def kernel(x, gla_norm, gla_w_in, gla_w_gate_up, gla_b_gate, gla_head_norm, gla_w_out, kv_norm, w_kv, k_norm, rel_bias, swa_norm, swa_w_q, swa_q_norm, swa_sinks, swa_w_out, moe_norm, moe_w_group, moe_b_group, moe_w_router, moe_b_router, moe_w_gate, moe_w_up, moe_w_down):
    batch, seq, d = x.shape
    assert gla_norm.shape[0] == 1 and swa_norm.shape[0] == 1 and moe_norm.shape[0] == 2
    h = x.reshape(batch * seq, d)
    h = _gla_layer(h, batch, gla_norm[0], gla_w_in[0], gla_w_gate_up[0], gla_b_gate[0],
                   gla_head_norm[0], gla_w_out[0])
    h = _moe_layer(h, 0, moe_norm[0], moe_w_group[0], moe_b_group[0], moe_w_router[0],
                   moe_b_router[0], moe_w_gate, moe_w_up, moe_w_down)
    h = _swa_layer(h, batch, kv_norm, w_kv, k_norm, rel_bias, swa_norm[0], swa_w_q[0],
                   swa_q_norm[0], swa_sinks[0], swa_w_out[0])
    h = _moe_layer(h, 1, moe_norm[1], moe_w_group[1], moe_b_group[1], moe_w_router[1],
                   moe_b_router[1], moe_w_gate, moe_w_up, moe_w_down)
    return h.reshape(batch, seq, d)
```

```python
import functools
import math

import jax
import jax.numpy as jnp
import numpy as np
from jax import lax
from jax.experimental import pallas as pl
from jax.experimental.pallas import tpu as pltpu

F32 = jnp.float32
BF16 = jnp.bfloat16

RMS_EPS = 1e-6
GLA_HEADS = 4
GLA_GATE_RANK = 16
GLA_GATE_NORMALIZER = 16.0
GLA_LOG_GATE_MIN = -1.0
GLA_CHUNK = 64
SWA_HEAD_DIM = 64
SWA_GROUPS = 8
WINDOW = 128
NUM_BUCKETS = 32
REL_MAX_DISTANCE = 128
MOE_GROUPS = 4
MOE_EXPERTS_PER_GROUP = 8

LANES = 128
SUBLANES = 8
VMEM_LIMIT_BYTES = 56 * 1024 * 1024

PREP_ROWS = 256
PROJ_ROWS = 512
ROUTER_ROWS = 512
EXPERT_ROWS = 512
COMBINE_ROWS = 256
SWA_PROJ_ROWS = 256
LOG2_E = math.log2(math.e)
DMA_UNROLL = 8
INVERSE_STEPS = 64


def _params(*semantics):
    return pltpu.CompilerParams(dimension_semantics=semantics, vmem_limit_bytes=VMEM_LIMIT_BYTES)


def _dot(a, b):
    return jnp.dot(a, b, preferred_element_type=F32)


def _dot_nt(a, b):
    return lax.dot_general(a, b, (((1,), (1,)), ((), ())), preferred_element_type=F32)


def _split(x):
    hi = x.astype(BF16)
    lo = (x - hi.astype(F32)).astype(BF16)
    return hi, lo


def _dot3(a, b, dot=_dot):
    ah, al = _split(a)
    bh, bl = _split(b)
    return dot(ah, bh) + dot(al, bh) + dot(ah, bl)


def _rms_inv(x):
    return lax.rsqrt(jnp.mean(x * x, axis=-1, keepdims=True) + RMS_EPS)


def _sigmoid(x):
    return 1.0 / (1.0 + jnp.exp(-x))


def _log_sigmoid(x):
    return jnp.minimum(x, 0.0) - jnp.log1p(jnp.exp(-jnp.abs(x)))


def _norm_matmul_kernel(x_ref, g_ref, w_ref, o_ref, xn_ref):
    @pl.when(pl.program_id(1) == 0)
    def _():
        x = x_ref[...]
        xn_ref[...] = (x * _rms_inv(x) * g_ref[...]).astype(BF16)

    o_ref[...] = _dot(xn_ref[...], w_ref[...]).astype(o_ref.dtype)


def _norm_matmul(x, g, w, *, rows, cols, out_dtype):
    t, d = x.shape
    n = w.shape[1]
    return pl.pallas_call(
        _norm_matmul_kernel,
        out_shape=jax.ShapeDtypeStruct((t, n), out_dtype),
        grid=(t // rows, n // cols),
        in_specs=[
            pl.BlockSpec((rows, d), lambda i, j: (i, 0)),
            pl.BlockSpec((1, d), lambda i, j: (0, 0)),
            pl.BlockSpec((d, cols), lambda i, j: (0, j)),
        ],
        out_specs=pl.BlockSpec((rows, cols), lambda i, j: (i, j)),
        scratch_shapes=[pltpu.VMEM((rows, d), BF16)],
        compiler_params=_params("parallel", "arbitrary"),
        name="norm_matmul",
    )(x, g.reshape(1, d), w)


def _matmul_residual_kernel(a_ref, w_ref, res_ref, o_ref):
    o_ref[...] = res_ref[...] + _dot(a_ref[...], w_ref[...])


def _matmul_residual(a, w, res, *, rows):
    t, k = a.shape
    n = w.shape[1]
    return pl.pallas_call(
        _matmul_residual_kernel,
        out_shape=jax.ShapeDtypeStruct((t, n), F32),
        grid=(t // rows,),
        in_specs=[
            pl.BlockSpec((rows, k), lambda i: (i, 0)),
            pl.BlockSpec((k, n), lambda i: (0, 0)),
            pl.BlockSpec((rows, n), lambda i: (i, 0)),
        ],
        out_specs=pl.BlockSpec((rows, n), lambda i: (i, 0)),
        compiler_params=_params("parallel"),
        name="matmul_residual",
    )(a, w, res)


def _pair_constants(chunk):
    pair = 2 * chunk
    i = np.arange(pair)[:, None]
    j = np.arange(pair)[None, :]
    tri = (j <= i).astype(np.float32)
    sel = ((i >= chunk) & (j < chunk)).astype(np.float32)
    blk = ((i // chunk) == (j // chunk)).astype(np.float32)
    ones = np.ones((pair, pair), np.float32)
    c_row = np.concatenate([tri, sel], axis=0)
    c_col = np.concatenate([tri.T, sel.T, blk, ones], axis=1)
    return jnp.asarray(c_row, BF16), jnp.asarray(c_col, BF16)


def _gla_prep_kernel(x_ref, g_ref, wq_ref, wkt_ref, wg_ref, wgu_ref, wgut_ref, bgr_ref, bgc_ref,
                     crow_ref, ccol_ref, qe_ref, qes_ref, ket_ref, klt_ref, ktt_ref, dect_ref,
                     *, scale, pair):
    x = x_ref[...]
    xn = (x * _rms_inv(x) * g_ref[...]).astype(BF16)
    q = _dot(xn, wq_ref[...]) * scale
    kt = _dot_nt(wkt_ref[...], xn)
    glr = _dot(xn, wg_ref[...])
    z = _dot3(glr, wgu_ref[...]) + bgr_ref[...]
    zt = _dot3(wgut_ref[...], glr, dot=_dot_nt) + bgc_ref[...]
    la = jnp.maximum(_log_sigmoid(z) / GLA_GATE_NORMALIZER, GLA_LOG_GATE_MIN)
    lat = jnp.maximum(_log_sigmoid(zt) / GLA_GATE_NORMALIZER, GLA_LOG_GATE_MIN)
    crow = crow_ref[...]
    ccol = ccol_ref[...]
    for p in range(x.shape[0] // pair):
        rows = slice(p * pair, (p + 1) * pair)
        hi, lo = _split(la[rows])
        cr = _dot(crow, hi) + _dot(crow, lo)
        b_pair = cr[:pair]
        b_chunk = b_pair - cr[pair:]
        qp = q[rows]
        qe_ref[rows, :] = (qp * jnp.exp(b_chunk)).astype(BF16)
        qes_ref[rows, :] = (qp * jnp.exp(b_pair)).astype(BF16)
        hi, lo = _split(lat[:, rows])
        cc = _dot(hi, ccol) + _dot(lo, ccol)
        bt_pair = cc[:, :pair]
        bt_chunk = bt_pair - cc[:, pair:2 * pair]
        end_chunk = cc[:, 2 * pair:3 * pair]
        end_pair = cc[:, 3 * pair:]
        ktp = kt[:, rows]
        ket_ref[:, rows] = (ktp * jnp.exp(-bt_chunk)).astype(BF16)
        klt_ref[:, rows] = (ktp * jnp.exp(end_chunk - bt_chunk)).astype(BF16)
        ktt_ref[:, rows] = (ktp * jnp.exp(end_pair - bt_pair)).astype(BF16)
        dect_ref[:, rows] = jnp.exp(end_pair)


def _gla_prep(h, norm_g, w_in, w_gate_up, b_gate):
    t, d = h.shape
    hk = d // 2
    rank = w_gate_up.shape[0]
    pair = 2 * GLA_CHUNK
    rows = PREP_ROWS
    dk = hk // GLA_HEADS
    wq = w_in[:, :hk].astype(BF16)
    wkt = w_in[:, hk:2 * hk].T.astype(BF16)
    wg = jnp.pad(w_in[:, 3 * d:], ((0, 0), (0, LANES - rank))).astype(BF16)
    wgu = jnp.pad(w_gate_up, ((0, LANES - rank), (0, 0)))
    c_row, c_col = _pair_constants(GLA_CHUNK)
    const = lambda shape: pl.BlockSpec(shape, lambda i: (0, 0))
    row_out = pl.BlockSpec((rows, hk), lambda i: (i, 0))
    col_out = pl.BlockSpec((hk, rows), lambda i: (0, i))
    return pl.pallas_call(
        functools.partial(_gla_prep_kernel, scale=dk ** -0.5, pair=pair),
        out_shape=(
            jax.ShapeDtypeStruct((t, hk), BF16),
            jax.ShapeDtypeStruct((t, hk), BF16),
            jax.ShapeDtypeStruct((hk, t), BF16),
            jax.ShapeDtypeStruct((hk, t), BF16),
            jax.ShapeDtypeStruct((hk, t), BF16),
            jax.ShapeDtypeStruct((hk, t), F32),
        ),
        grid=(t // rows,),
        in_specs=[
            pl.BlockSpec((rows, d), lambda i: (i, 0)),
            const((1, d)),
            const((d, hk)),
            const((hk, d)),
            const((d, LANES)),
            const((LANES, hk)),
            const((hk, LANES)),
            const((1, hk)),
            const((hk, 1)),
            const((2 * pair, pair)),
            const((pair, 4 * pair)),
        ],
        out_specs=(row_out, row_out, col_out, col_out, col_out, col_out),
        compiler_params=_params("parallel"),
        name="gla_prep",
    )(h, norm_g.reshape(1, d), wq, wkt, wg, wgu, wgu.T, b_gate.reshape(1, hk),
      b_gate.reshape(hk, 1), c_row, c_col)


def _gla_core_kernel(qe_ref, qes_ref, ket_ref, klt_ref, ktt_ref, dect_ref, v_ref, r_ref, hg_ref,
                     o_ref, s_ref, *, heads, chunk):
    @pl.when(pl.program_id(1) == 0)
    def _():
        s_ref[...] = jnp.zeros_like(s_ref)

    pair = 2 * chunk
    dk = qe_ref.shape[1] // heads
    dv = v_ref.shape[1] // heads
    row = lax.broadcasted_iota(jnp.int32, (pair, pair), 0)
    col = lax.broadcasted_iota(jnp.int32, (pair, pair), 1)
    second = row >= chunk
    first_keys = col < chunk
    m_intra = jnp.logical_and(col <= row, jnp.logical_not(jnp.logical_xor(second, col >= chunk)))
    m_cross = jnp.logical_and(second, first_keys)
    for h in range(heads):
        ks = slice(h * dk, (h + 1) * dk)
        vs = slice(h * dv, (h + 1) * dv)
        qe = qe_ref[:, ks]
        v = v_ref[:, vs]
        att = jnp.where(m_intra, _dot(qe, ket_ref[ks, :]),
                        jnp.where(m_cross, _dot(qe, klt_ref[ks, :]), 0.0))
        state = s_ref[h]
        o = _dot(att.astype(BF16), v) + _dot(qes_ref[:, ks], state.astype(BF16))
        dec = dect_ref[ks, :]
        dec = jnp.concatenate([dec] * (dv // pair), axis=1) if dv > pair else dec[:, :dv]
        s_ref[h] = dec * state + _dot(ktt_ref[ks, :], v)
        on = o * _rms_inv(o) * hg_ref[...]
        r = r_ref[:, vs].astype(F32)
        o_ref[:, vs] = (on * (r * _sigmoid(r))).astype(BF16)


def _gla_core(qe, qes, ket, klt, ktt, dect, vr, head_g, *, batch):
    t, hk = qe.shape
    hv = vr.shape[1] // 2
    pair = 2 * GLA_CHUNK
    npair = t // batch // pair
    dk = hk // GLA_HEADS
    dv = hv // GLA_HEADS
    row_k = pl.BlockSpec((pair, hk), lambda b, p: (b * npair + p, 0))
    col_k = pl.BlockSpec((hk, pair), lambda b, p: (0, b * npair + p))
    return pl.pallas_call(
        functools.partial(_gla_core_kernel, heads=GLA_HEADS, chunk=GLA_CHUNK),
        out_shape=jax.ShapeDtypeStruct((t, hv), BF16),
        grid=(batch, npair),
        in_specs=[
            row_k, row_k, col_k, col_k, col_k, col_k,
            pl.BlockSpec((pair, hv), lambda b, p: (b * npair + p, 0)),
            pl.BlockSpec((pair, hv), lambda b, p: (b * npair + p, 1)),
            pl.BlockSpec((1, dv), lambda b, p: (0, 0)),
        ],
        out_specs=pl.BlockSpec((pair, hv), lambda b, p: (b * npair + p, 0)),
        scratch_shapes=[pltpu.VMEM((GLA_HEADS, dk, dv), F32)],
        compiler_params=_params("parallel", "arbitrary"),
        name="gla_core",
    )(qe, qes, ket, klt, ktt, dect, vr, vr, head_g.reshape(1, dv))


def _gla_layer(h, batch, norm_g, w_in, w_gate_up, b_gate, head_g, w_out):
    d = h.shape[1]
    hk = d // 2
    qe, qes, ket, klt, ktt, dect = _gla_prep(h, norm_g, w_in, w_gate_up, b_gate)
    vr = _norm_matmul(h, norm_g, w_in[:, 2 * hk:2 * hk + 2 * d].astype(BF16),
                      rows=PROJ_ROWS, cols=d, out_dtype=BF16)
    og = _gla_core(qe, qes, ket, klt, ktt, dect, vr, head_g, batch=batch)
    return _matmul_residual(og, w_out.astype(BF16), h, rows=PROJ_ROWS)


META_E, META_W, META_RANK = 0, 2, 4


def _router_kernel(h_ref, g_ref, w_ref, b_ref, tril_ref, meta_ref, cnt_ref, base_ref,
                   *, groups, per_group):
    @pl.when(pl.program_id(0) == 0)
    def _():
        base_ref[...] = jnp.zeros_like(base_ref)

    x = h_ref[...]
    xn = x * _rms_inv(x) * g_ref[...]
    logits = _dot3(xn, w_ref[...]) + b_ref[...]
    lane = lax.broadcasted_iota(jnp.int32, logits.shape, 1).astype(F32)
    neg = -jnp.inf
    far = float(LANES)

    def first_max(vals):
        m = jnp.max(vals, axis=-1, keepdims=True)
        return m, jnp.min(jnp.where(vals == m, lane, far), axis=-1, keepdims=True)

    gl = jnp.where(lane < groups, logits, neg)
    gmax, gidx = first_max(gl)
    p_group = 1.0 / jnp.sum(jnp.exp(gl - gmax), axis=-1, keepdims=True)
    lo = groups + per_group * gidx
    el = jnp.where(jnp.logical_and(lane >= lo, lane < lo + per_group), logits, neg)
    v1, i1 = first_max(el)
    v2, i2 = first_max(jnp.where(lane == i1, neg, el))
    t = jnp.exp(v2 - v1)
    w1 = p_group / (1.0 + t)
    w2 = p_group * t / (1.0 + t)

    oh1 = lane == i1
    oh2 = lane == i2
    onehot = jnp.where(jnp.logical_or(oh1, oh2), 1.0, 0.0).astype(BF16)
    seen = base_ref[...] + _dot(tril_ref[...], onehot)
    rank1 = jnp.sum(jnp.where(oh1, seen, 0.0), axis=-1, keepdims=True) - 1.0
    rank2 = jnp.sum(jnp.where(oh2, seen, 0.0), axis=-1, keepdims=True) - 1.0
    base_ref[...] = seen[-1:, :]
    cnt_ref[...] = jnp.broadcast_to(seen[-1:, :], cnt_ref.shape)

    rec = jnp.zeros_like(logits)
    for k, val in ((META_E, i1 - groups), (META_E + 1, i2 - groups), (META_W, w1),
                   (META_W + 1, w2), (META_RANK, rank1), (META_RANK + 1, rank2)):
        rec = jnp.where(lane == k, val, rec)
    meta_ref[...] = rec


def _moe_router(h, norm_g, w_group, b_group, w_router, b_router):
    t, d = h.shape
    rows = ROUTER_ROWS
    groups = w_group.shape[1]
    ne = w_router.shape[1]
    pad = LANES - groups - ne
    w = jnp.pad(jnp.concatenate([w_group, w_router], axis=1), ((0, 0), (0, pad)))
    b = jnp.pad(jnp.concatenate([b_group, b_router]), (0, pad)).reshape(1, LANES)
    tril = jnp.asarray(np.tril(np.ones((rows, rows), np.float32)), BF16)
    meta, cnt = pl.pallas_call(
        functools.partial(_router_kernel, groups=groups, per_group=ne // groups),
        out_shape=(jax.ShapeDtypeStruct((t, LANES), F32), jax.ShapeDtypeStruct((8, LANES), F32)),
        grid=(t // rows,),
        in_specs=[
            pl.BlockSpec((rows, d), lambda i: (i, 0)),
            pl.BlockSpec((1, d), lambda i: (0, 0)),
            pl.BlockSpec((d, LANES), lambda i: (0, 0)),
            pl.BlockSpec((1, LANES), lambda i: (0, 0)),
            pl.BlockSpec((rows, rows), lambda i: (0, 0)),
        ],
        out_specs=(pl.BlockSpec((rows, LANES), lambda i: (i, 0)),
                   pl.BlockSpec((8, LANES), lambda i: (0, 0))),
        scratch_shapes=[pltpu.VMEM((1, LANES), F32)],
        compiler_params=_params("arbitrary"),
        name="moe_router",
    )(h, norm_g.reshape(1, d), w, b, tril)
    return meta, cnt[0, groups:groups + ne]


def _inverse_kernel(pos0_ref, pos1_ref, src_ref):
    step = pl.program_id(0)
    half = pl.num_programs(0) // 2
    slots = src_ref.shape[0] // INVERSE_STEPS
    tokens = pos0_ref.shape[0] // INVERSE_STEPS

    @pl.when(step < half)
    def _():
        def fill(i, carry):
            for u in range(DMA_UNROLL):
                src_ref[step * slots + i * DMA_UNROLL + u] = 0
            return carry

        lax.fori_loop(0, slots // DMA_UNROLL, fill, 0)

    @pl.when(step >= half)
    def _():
        def place(i, carry):
            for u in range(DMA_UNROLL):
                tok = (step - half) * tokens + i * DMA_UNROLL + u
                src_ref[pos0_ref[tok]] = tok
                src_ref[pos1_ref[tok]] = tok
            return carry

        lax.fori_loop(0, tokens // DMA_UNROLL, place, 0)


def _moe_inverse(pos0, pos1, slots):
    smem = pl.BlockSpec(memory_space=pltpu.SMEM)
    return pl.pallas_call(
        _inverse_kernel,
        out_shape=jax.ShapeDtypeStruct((slots,), jnp.int32),
        grid=(2 * INVERSE_STEPS,),
        in_specs=[smem, smem],
        out_specs=smem,
        compiler_params=_params("arbitrary"),
        name="moe_inverse",
    )(pos0, pos1)


def _row_copy(src_hbm, row, dst, r_tile, r_sub, sem):
    return pltpu.make_async_copy(src_hbm.at[pl.ds(row, 1)], dst.at[r_tile, pl.ds(r_sub, 1)], sem)


def _rows_wait(dst, sem):
    pltpu.make_async_copy(dst, dst, sem).wait()


def _experts_kernel(src_ref, texp_ref, nused_ref, h_hbm, g_ref, wg_hbm, wu_hbm, wd_hbm, y_ref,
                    xbuf, xn_ref, wgs, wus, wds, wgb, wub, wdb, sem, wsem, *, rows, layer):
    i = pl.program_id(0)
    nused = nused_ref[0]
    expert = texp_ref[i]

    def start(tile, slot):
        base = tile * rows
        for r in range(rows):
            _row_copy(h_hbm, src_ref[base + r], xbuf.at[slot], r // SUBLANES, r % SUBLANES,
                      sem.at[slot]).start()

    def weight_copies(e):
        return (pltpu.make_async_copy(wg_hbm.at[layer, e], wgs, wsem.at[0]),
                pltpu.make_async_copy(wu_hbm.at[layer, e], wus, wsem.at[1]),
                pltpu.make_async_copy(wd_hbm.at[layer, e], wds, wsem.at[2]))

    @pl.when(i == 0)
    def _():
        for c in weight_copies(expert):
            c.start()
        start(0, 0)

    fresh = jnp.logical_or(i == 0, expert != texp_ref[jnp.maximum(i - 1, 0)])

    @pl.when(jnp.logical_and(fresh, i < nused))
    def _():
        for c in weight_copies(expert):
            c.wait()
        wgb[...] = (wgs[...] * g_ref[...]).astype(BF16)
        wub[...] = (wus[...] * g_ref[...]).astype(BF16)
        wdb[...] = wds[...].astype(BF16)

    nxt = texp_ref[jnp.minimum(i + 1, pl.num_programs(0) - 1)]

    @pl.when(jnp.logical_and(i + 1 < nused, nxt != expert))
    def _():
        for c in weight_copies(nxt):
            c.start()

    @pl.when(i < nused)
    def _():
        slot = i % 2
        _rows_wait(xbuf.at[slot], sem.at[slot])
        x = xbuf[slot].reshape(xn_ref.shape)
        xn_ref[...] = x.astype(BF16)
        inv = _rms_inv(x)
        start(jnp.minimum(i + 1, nused - 1), 1 - slot)
        xn = xn_ref[...]
        gate = _dot(xn, wgb[...]) * inv
        up = _dot(xn, wub[...]) * inv
        act = (gate * _sigmoid(gate)) * up
        y_ref[...] = _dot(act.astype(BF16), wdb[...])

    @pl.when(i == nused - 1)
    def _():
        slot = nused % 2
        _rows_wait(xbuf.at[slot], sem.at[slot])

    @pl.when(i >= nused)
    def _():
        y_ref[...] = jnp.zeros_like(y_ref)


def _moe_experts(h, norm_g, layer, w_gate, w_up, w_down, src, tile_expert, n_used):
    t, d = h.shape
    _, ne, _, f = w_gate.shape
    rows = EXPERT_ROWS
    slots = src.shape[0]
    hbm = pl.BlockSpec(memory_space=pl.ANY)
    return pl.pallas_call(
        functools.partial(_experts_kernel, rows=rows, layer=layer),
        out_shape=jax.ShapeDtypeStruct((slots, d), F32),
        grid_spec=pltpu.PrefetchScalarGridSpec(
            num_scalar_prefetch=3,
            grid=(slots // rows,),
            in_specs=[hbm, pl.BlockSpec((d, 1), lambda i, *_: (0, 0)), hbm, hbm, hbm],
            out_specs=pl.BlockSpec((rows, d), lambda i, *_: (i, 0)),
            scratch_shapes=[pltpu.VMEM((2, rows // SUBLANES, SUBLANES, d), F32),
                            pltpu.VMEM((rows, d), BF16),
                            pltpu.VMEM((d, f), F32), pltpu.VMEM((d, f), F32),
                            pltpu.VMEM((f, d), F32),
                            pltpu.VMEM((d, f), BF16), pltpu.VMEM((d, f), BF16),
                            pltpu.VMEM((f, d), BF16),
                            pltpu.SemaphoreType.DMA((2,)), pltpu.SemaphoreType.DMA((3,))],
        ),
        compiler_params=_params("arbitrary"),
        name="moe_experts",
    )(src, tile_expert, n_used, h, norm_g.reshape(d, 1), w_gate, w_up, w_down)


def _combine_kernel(pos0_ref, pos1_ref, h_ref, meta_ref, y_hbm, o_ref, buf0, buf1, sem, *, rows):
    i = pl.program_id(0)

    def start(tile):
        slot = tile % 2

        def body(g, carry):
            for u in range(SUBLANES):
                tok = tile * rows + g * SUBLANES + u
                _row_copy(y_hbm, pos0_ref[tok], buf0.at[slot], g, u, sem.at[0, slot]).start()
                _row_copy(y_hbm, pos1_ref[tok], buf1.at[slot], g, u, sem.at[1, slot]).start()
            return carry

        lax.fori_loop(0, rows // SUBLANES, body, 0)

    @pl.when(i == 0)
    def _():
        start(0)

    @pl.when(i + 1 < pl.num_programs(0))
    def _():
        start(i + 1)

    slot = i % 2
    _rows_wait(buf0.at[slot], sem.at[0, slot])
    _rows_wait(buf1.at[slot], sem.at[1, slot])
    meta = meta_ref[...]
    w0 = meta[:, META_W:META_W + 1]
    w1 = meta[:, META_W + 1:META_W + 2]
    shape = h_ref.shape
    o_ref[...] = h_ref[...] + w0 * buf0[slot].reshape(shape) + w1 * buf1[slot].reshape(shape)


def _moe_combine(h, meta, y, pos0, pos1):
    t, d = h.shape
    rows = COMBINE_ROWS
    gather_buf = pltpu.VMEM((2, rows // SUBLANES, SUBLANES, d), F32)
    return pl.pallas_call(
        functools.partial(_combine_kernel, rows=rows),
        out_shape=jax.ShapeDtypeStruct((t, d), F32),
        grid_spec=pltpu.PrefetchScalarGridSpec(
            num_scalar_prefetch=2,
            grid=(t // rows,),
            in_specs=[
                pl.BlockSpec((rows, d), lambda i, *_: (i, 0)),
                pl.BlockSpec((rows, LANES), lambda i, *_: (i, 0)),
                pl.BlockSpec(memory_space=pl.ANY),
            ],
            out_specs=pl.BlockSpec((rows, d), lambda i, *_: (i, 0)),
            scratch_shapes=[gather_buf, gather_buf, pltpu.SemaphoreType.DMA((2, 2))],
        ),
        compiler_params=_params("arbitrary"),
        name="moe_combine",
    )(pos0, pos1, h, meta, y)


def _moe_layer(h, layer, norm_g, w_group, b_group, w_router, b_router, w_gate, w_up, w_down):
    t, _ = h.shape
    ne = w_router.shape[1]
    rows = EXPERT_ROWS
    meta, counts = _moe_router(h, norm_g, w_group, b_group, w_router, b_router)
    counts = counts.astype(jnp.int32)
    padded = (counts + rows - 1) // rows * rows
    ends = jnp.cumsum(padded)
    starts = ends - padded
    expert = meta[:, META_E:META_E + 2].astype(jnp.int32)
    rank = meta[:, META_RANK:META_RANK + 2].astype(jnp.int32)
    pos = starts[expert] + rank
    slots = 2 * t + ne * rows
    n_tiles = slots // rows
    n_used = (ends[-1] // rows).astype(jnp.int32)
    tile_start = jnp.arange(n_tiles, dtype=jnp.int32) * rows
    tile_start = jnp.minimum(tile_start, ends[-1] - 1)
    tile_expert = jnp.sum(tile_start[:, None] >= ends[None, :], axis=1, dtype=jnp.int32)
    tile_expert = jnp.minimum(tile_expert, ne - 1)
    pos0 = pos[:, 0]
    pos1 = pos[:, 1]
    src = _moe_inverse(pos0, pos1, slots)
    y = _moe_experts(h, norm_g, layer, w_gate, w_up, w_down, src, tile_expert, n_used.reshape(1))
    return _moe_combine(h, meta, y, pos0, pos1)


def _swa_proj_kernel(x_ref, gq_ref, gkv_ref, wq_ref, wkv_ref, kg_ref, q_ref, k_ref, v_ref):
    x = x_ref[...]
    xs = x * _rms_inv(x)
    q_ref[...] = _dot((xs * gq_ref[...]).astype(BF16), wq_ref[...]).astype(BF16)
    kv = _dot((xs * gkv_ref[...]).astype(BF16), wkv_ref[...])
    half = kv.shape[1] // 2
    v_ref[...] = kv[:, half:].astype(BF16)
    for hd in range(half // LANES):
        sl = slice(hd * LANES, (hd + 1) * LANES)
        k = kv[:, sl]
        k_ref[:, sl] = (k * _rms_inv(k) * kg_ref[...]).astype(BF16)


def _swa_proj(h, q_norm_g, kv_norm_g, w_q, w_kv, k_norm):
    t, d = h.shape
    hd = SWA_HEAD_DIM
    kvh = w_kv.shape[1] // (2 * hd)
    rows = SWA_PROJ_ROWS
    rep = LANES // hd
    w_dup = jnp.broadcast_to(w_kv.astype(BF16).reshape(d, 2 * kvh, 1, hd), (d, 2 * kvh, rep, hd))
    w_dup = w_dup.reshape(d, 2 * kvh * LANES)
    kg = jnp.tile(k_norm, rep).reshape(1, LANES)
    const = lambda shape: pl.BlockSpec(shape, lambda i: (0, 0))
    return pl.pallas_call(
        _swa_proj_kernel,
        out_shape=(
            jax.ShapeDtypeStruct((t, w_q.shape[1]), BF16),
            jax.ShapeDtypeStruct((t, kvh * LANES), BF16),
            jax.ShapeDtypeStruct((t, kvh * LANES), BF16),
        ),
        grid=(t // rows,),
        in_specs=[
            pl.BlockSpec((rows, d), lambda i: (i, 0)),
            const((1, d)), const((1, d)),
            const((d, w_q.shape[1])), const((d, 2 * kvh * LANES)), const((1, LANES)),
        ],
        out_specs=(
            pl.BlockSpec((rows, w_q.shape[1]), lambda i: (i, 0)),
            pl.BlockSpec((rows, kvh * LANES), lambda i: (i, 0)),
            pl.BlockSpec((rows, kvh * LANES), lambda i: (i, 0)),
        ),
        compiler_params=_params("parallel"),
        name="swa_proj",
    )(h, q_norm_g.reshape(1, d), kv_norm_g.reshape(1, d), w_q.astype(BF16), w_dup, kg)


def _swa_attn_kernel(q_ref, kp_ref, kc_ref, vp_ref, vc_ref, bias_ref, sinkw_ref, qg_ref,
                     blk_ref, ones_ref, o_ref, *, kvh, pairs, window):
    hd = LANES // 2
    lane = lax.broadcasted_iota(jnp.int32, (window, LANES), 1)
    low = lane < hd
    width = 2 * window
    olow = lax.broadcasted_iota(jnp.int32, (pairs * window, LANES), 1) < hd
    zero = jnp.zeros((), BF16)

    def block_diag(prev, cur):
        return jnp.concatenate([jnp.where(low, prev, zero), jnp.where(low, cur, zero),
                                jnp.where(low, zero, prev), jnp.where(low, zero, cur)], axis=0)

    for h in range(kvh):
        ksl = slice(h * LANES, (h + 1) * LANES)
        kk = block_diag(kp_ref[:, ksl], kc_ref[:, ksl])
        vv = block_diag(vp_ref[:, ksl], vc_ref[:, ksl])
        base = h * pairs * LANES
        q2 = jnp.concatenate([q_ref[:, base + p * LANES:base + (p + 1) * LANES]
                              for p in range(pairs)], axis=0).astype(F32)
        ms = _dot((q2 * q2).astype(BF16), blk_ref[...])
        qn = (q2 * lax.rsqrt(ms + RMS_EPS) * qg_ref[...]).astype(BF16)
        s = _dot_nt(qn, kk) + bias_ref[0, h]
        probs = []
        row_max = []
        for half in range(2):
            sh = s[:, half * width:(half + 1) * width]
            m = jnp.max(sh, axis=-1, keepdims=True)
            probs.append(jnp.exp2(sh - m).astype(BF16))
            row_max.append(m)
        mixed = _dot(jnp.concatenate(probs, axis=1), jnp.concatenate([vv, ones_ref[...]], axis=1))
        den = mixed[:, LANES:] + jnp.exp2(sinkw_ref[h] - jnp.where(olow, row_max[0], row_max[1]))
        o2 = mixed[:, :LANES] * (1.0 / den)
        for p in range(pairs):
            o_ref[:, base + p * LANES:base + (p + 1) * LANES] = (
                o2[p * window:(p + 1) * window].astype(BF16))


def _t5_bucket(dist):
    max_exact = NUM_BUCKETS // 2
    n = np.maximum(dist, 0)
    large = max_exact + (np.log(np.maximum(n, max_exact) / max_exact)
                         / math.log(REL_MAX_DISTANCE / max_exact)
                         * (NUM_BUCKETS - max_exact)).astype(np.int32)
    return np.where(n < max_exact, n, np.minimum(large, NUM_BUCKETS - 1)).astype(np.int32)


def _swa_attn(q, kd, vd, rel_bias, q_norm, sinks, *, batch):
    t, dq = q.shape
    kvh = kd.shape[1] // LANES
    hq = dq // SWA_HEAD_DIM
    pairs = hq // kvh // 2
    w = WINDOW
    nblk = t // batch // w
    row = np.arange(w)[:, None]
    col = np.arange(2 * w)[None, :]
    dist = w + row - col
    band = (dist >= 0) & (dist < w)
    onehot = (_t5_bucket(dist)[..., None] == np.arange(NUM_BUCKETS)).astype(np.float32)
    bias = jnp.dot(jnp.asarray(onehot), rel_bias.astype(F32), precision=lax.Precision.HIGHEST)
    bias = jnp.where(jnp.asarray(band)[..., None], bias, -jnp.inf)
    bias = bias.transpose(2, 0, 1).reshape(kvh, pairs, 2, w, 2 * w).transpose(0, 1, 3, 2, 4)
    bias = bias.reshape(kvh, pairs * w, 4 * w) * LOG2_E
    prev_cols = jnp.asarray((np.arange(4 * w) // w) % 2 == 0)
    bias = jnp.stack([bias, jnp.where(prev_cols, -jnp.inf, bias)])
    sink = sinks.astype(F32).reshape(kvh, pairs, 1, 2, 1) * LOG2_E
    sink_wide = jnp.broadcast_to(sink, (kvh, pairs, w, 2, SWA_HEAD_DIM))
    sink_wide = sink_wide.reshape(kvh, pairs * w, LANES)
    qg = jnp.tile(q_norm, LANES // SWA_HEAD_DIM) * (SWA_HEAD_DIM ** -0.5 * LOG2_E)
    lane = np.arange(LANES)
    blk = ((lane[:, None] // SWA_HEAD_DIM) == (lane[None, :] // SWA_HEAD_DIM)) / SWA_HEAD_DIM
    ones = (np.arange(4 * w)[:, None] // (2 * w)) == (lane[None, :] // SWA_HEAD_DIM)
    cur = lambda b, n: (b * nblk + n, 0)
    prev = lambda b, n: (b * nblk + jnp.maximum(n - 1, 0), 0)
    const2 = lambda shape: pl.BlockSpec(shape, lambda b, n: (0, 0))
    const3 = lambda shape: pl.BlockSpec(shape, lambda b, n: (0, 0, 0))
    return pl.pallas_call(
        functools.partial(_swa_attn_kernel, kvh=kvh, pairs=pairs, window=w),
        out_shape=jax.ShapeDtypeStruct((t, dq), BF16),
        grid=(batch, nblk),
        in_specs=[
            pl.BlockSpec((w, dq), cur),
            pl.BlockSpec((w, kvh * LANES), prev), pl.BlockSpec((w, kvh * LANES), cur),
            pl.BlockSpec((w, kvh * LANES), prev), pl.BlockSpec((w, kvh * LANES), cur),
            pl.BlockSpec((1, kvh, pairs * w, 4 * w), lambda b, n: (jnp.where(n == 0, 1, 0), 0, 0, 0)),
            const3((kvh, pairs * w, LANES)),
            const2((1, LANES)), const2((LANES, LANES)), const2((4 * w, LANES)),
        ],
        out_specs=pl.BlockSpec((w, dq), cur),
        compiler_params=_params("parallel", "parallel"),
        name="swa_attn",
    )(q, kd, kd, vd, vd, bias, sink_wide, qg.reshape(1, LANES), jnp.asarray(blk, BF16),
      jnp.asarray(ones, BF16))


def _swa_layer(h, batch, kv_norm, w_kv, k_norm, rel_bias, norm_g, w_q, q_norm, sinks, w_out):
    q, kd, vd = _swa_proj(h, norm_g, kv_norm, w_q, w_kv, k_norm)
    o = _swa_attn(q, kd, vd, rel_bias, q_norm, sinks, batch=batch)
    return _matmul_residual(o, w_out.astype(BF16), h, rows=PROJ_ROWS)


def kernel(x, gla_norm, gla_w_in, gla_w_gate_up, gla_b_gate, gla_head_norm, gla_w_out, kv_norm, w_kv, k_norm, rel_bias, swa_norm, swa_w_q, swa_q_norm, swa_sinks, swa_w_out, moe_norm, moe_w_group, moe_b_group, moe_w_router, moe_b_router, moe_w_gate, moe_w_up, moe_w_down):
    batch, seq, d = x.shape
    assert gla_norm.shape[0] == 1 and swa_norm.shape[0] == 1 and moe_norm.shape[0] == 2
    h = x.reshape(batch * seq, d)
    h = _gla_layer(h, batch, gla_norm[0], gla_w_in[0], gla_w_gate_up[0], gla_b_gate[0],
                   gla_head_norm[0], gla_w_out[0])
    h = _moe_layer(h, 0, moe_norm[0], moe_w_group[0], moe_b_group[0], moe_w_router[0],
                   moe_b_router[0], moe_w_gate, moe_w_up, moe_w_down)
    h = _swa_layer(h, batch, kv_norm, w_kv, k_norm, rel_bias, swa_norm[0], swa_w_q[0],
                   swa_q_norm[0], swa_sinks[0], swa_w_out[0])
    h = _moe_layer(h, 1, moe_norm[1], moe_w_group[1], moe_b_group[1], moe_w_router[1],
                   moe_b_router[1], moe_w_gate, moe_w_up, moe_w_down)
    return h.reshape(batch, seq, d)
```

```python
import functools
import math

import jax
import jax.numpy as jnp
import numpy as np
from jax import lax
from jax.experimental import pallas as pl
from jax.experimental.pallas import tpu as pltpu

F32 = jnp.float32
BF16 = jnp.bfloat16

RMS_EPS = 1e-6
GLA_HEADS = 4
GLA_GATE_RANK = 16
GLA_GATE_NORMALIZER = 16.0
GLA_LOG_GATE_MIN = -1.0
GLA_CHUNK = 64
SWA_HEAD_DIM = 64
SWA_GROUPS = 8
WINDOW = 128
NUM_BUCKETS = 32
REL_MAX_DISTANCE = 128
MOE_GROUPS = 4
MOE_EXPERTS_PER_GROUP = 8

LANES = 128
SUBLANES = 8
VMEM_LIMIT_BYTES = 56 * 1024 * 1024

PREP_ROWS = 256
PROJ_ROWS = 512
ROUTER_ROWS = 512
EXPERT_ROWS = 512
COMBINE_ROWS = 256
SWA_PROJ_ROWS = 256
LOG2_E = math.log2(math.e)
DMA_UNROLL = 8
GATHER_DEPTH = 3
INVERSE_STEPS = 64


def _params(*semantics):
    return pltpu.CompilerParams(dimension_semantics=semantics, vmem_limit_bytes=VMEM_LIMIT_BYTES)


def _dot(a, b):
    return jnp.dot(a, b, preferred_element_type=F32)


def _dot_nt(a, b):
    return lax.dot_general(a, b, (((1,), (1,)), ((), ())), preferred_element_type=F32)


def _split(x):
    hi = x.astype(BF16)
    lo = (x - hi.astype(F32)).astype(BF16)
    return hi, lo


def _dot3(a, b, dot=_dot):
    ah, al = _split(a)
    bh, bl = _split(b)
    return dot(ah, bh) + dot(al, bh) + dot(ah, bl)


def _rms_inv(x):
    return lax.rsqrt(jnp.mean(x * x, axis=-1, keepdims=True) + RMS_EPS)


def _sigmoid(x):
    return 1.0 / (1.0 + jnp.exp(-x))


def _log_sigmoid(x):
    return jnp.minimum(x, 0.0) - jnp.log1p(jnp.exp(-jnp.abs(x)))


def _norm_matmul_kernel(x_ref, w_ref, o_ref, xb_ref, inv_ref):
    @pl.when(pl.program_id(1) == 0)
    def _():
        x = x_ref[...]
        xb_ref[...] = x.astype(BF16)
        inv_ref[...] = _rms_inv(x)

    o_ref[...] = (_dot(xb_ref[...], w_ref[...]) * inv_ref[...]).astype(o_ref.dtype)


def _norm_matmul(x, g, w, *, rows, cols, out_dtype):
    t, d = x.shape
    n = w.shape[1]
    return pl.pallas_call(
        _norm_matmul_kernel,
        out_shape=jax.ShapeDtypeStruct((t, n), out_dtype),
        grid=(t // rows, n // cols),
        in_specs=[
            pl.BlockSpec((rows, d), lambda i, j: (i, 0)),
            pl.BlockSpec((d, cols), lambda i, j: (0, j)),
        ],
        out_specs=pl.BlockSpec((rows, cols), lambda i, j: (i, j)),
        scratch_shapes=[pltpu.VMEM((rows, d), BF16), pltpu.VMEM((rows, 1), F32)],
        compiler_params=_params("parallel", "arbitrary"),
        name="norm_matmul",
    )(x, (g[:, None] * w).astype(BF16))


def _matmul_residual_kernel(a_ref, w_ref, res_ref, o_ref):
    o_ref[...] = res_ref[...] + _dot(a_ref[...], w_ref[...])


def _matmul_residual(a, w, res, *, rows):
    t, k = a.shape
    n = w.shape[1]
    return pl.pallas_call(
        _matmul_residual_kernel,
        out_shape=jax.ShapeDtypeStruct((t, n), F32),
        grid=(t // rows,),
        in_specs=[
            pl.BlockSpec((rows, k), lambda i: (i, 0)),
            pl.BlockSpec((k, n), lambda i: (0, 0)),
            pl.BlockSpec((rows, n), lambda i: (i, 0)),
        ],
        out_specs=pl.BlockSpec((rows, n), lambda i: (i, 0)),
        compiler_params=_params("parallel"),
        name="matmul_residual",
    )(a, w, res)


def _pair_constants(chunk):
    pair = 2 * chunk
    i = np.arange(pair)[:, None]
    j = np.arange(pair)[None, :]
    tri = (j <= i).astype(np.float32)
    sel = ((i >= chunk) & (j < chunk)).astype(np.float32)
    blk = ((i // chunk) == (j // chunk)).astype(np.float32)
    ones = np.ones((pair, pair), np.float32)
    c_row = np.concatenate([tri, sel], axis=0)
    c_col = np.concatenate([tri.T, sel.T, blk, ones], axis=1)
    return jnp.asarray(c_row, BF16), jnp.asarray(c_col, BF16)


def _gla_prep_kernel(x_ref, g_ref, wq_ref, wkt_ref, wg_ref, wgu_ref, wgut_ref, bgr_ref, bgc_ref,
                     crow_ref, ccol_ref, qe_ref, qes_ref, ket_ref, klt_ref, ktt_ref, dect_ref,
                     *, scale, pair):
    x = x_ref[...]
    xn = (x * _rms_inv(x) * g_ref[...]).astype(BF16)
    q = _dot(xn, wq_ref[...]) * scale
    kt = _dot_nt(wkt_ref[...], xn)
    glr = _dot(xn, wg_ref[...])
    z = _dot3(glr, wgu_ref[...]) + bgr_ref[...]
    zt = _dot3(wgut_ref[...], glr, dot=_dot_nt) + bgc_ref[...]
    la = jnp.maximum(_log_sigmoid(z) / GLA_GATE_NORMALIZER, GLA_LOG_GATE_MIN)
    lat = jnp.maximum(_log_sigmoid(zt) / GLA_GATE_NORMALIZER, GLA_LOG_GATE_MIN)
    crow = crow_ref[...]
    ccol = ccol_ref[...]
    for p in range(x.shape[0] // pair):
        rows = slice(p * pair, (p + 1) * pair)
        hi, lo = _split(la[rows])
        cr = _dot(crow, hi) + _dot(crow, lo)
        b_pair = cr[:pair]
        b_chunk = b_pair - cr[pair:]
        qp = q[rows]
        qe_ref[rows, :] = (qp * jnp.exp(b_chunk)).astype(BF16)
        qes_ref[rows, :] = (qp * jnp.exp(b_pair)).astype(BF16)
        hi, lo = _split(lat[:, rows])
        cc = _dot(hi, ccol) + _dot(lo, ccol)
        bt_pair = cc[:, :pair]
        bt_chunk = bt_pair - cc[:, pair:2 * pair]
        end_chunk = cc[:, 2 * pair:3 * pair]
        end_pair = cc[:, 3 * pair:]
        ktp = kt[:, rows]
        ket_ref[:, rows] = (ktp * jnp.exp(-bt_chunk)).astype(BF16)
        klt_ref[:, rows] = (ktp * jnp.exp(end_chunk - bt_chunk)).astype(BF16)
        ktt_ref[:, rows] = (ktp * jnp.exp(end_pair - bt_pair)).astype(BF16)
        dect_ref[:, rows] = jnp.exp(end_pair)


def _gla_prep(h, norm_g, w_in, w_gate_up, b_gate):
    t, d = h.shape
    hk = d // 2
    rank = w_gate_up.shape[0]
    pair = 2 * GLA_CHUNK
    rows = PREP_ROWS
    dk = hk // GLA_HEADS
    wq = w_in[:, :hk].astype(BF16)
    wkt = w_in[:, hk:2 * hk].T.astype(BF16)
    wg = jnp.pad(w_in[:, 3 * d:], ((0, 0), (0, LANES - rank))).astype(BF16)
    wgu = jnp.pad(w_gate_up, ((0, LANES - rank), (0, 0)))
    c_row, c_col = _pair_constants(GLA_CHUNK)
    const = lambda shape: pl.BlockSpec(shape, lambda i: (0, 0))
    row_out = pl.BlockSpec((rows, hk), lambda i: (i, 0))
    col_out = pl.BlockSpec((hk, rows), lambda i: (0, i))
    return pl.pallas_call(
        functools.partial(_gla_prep_kernel, scale=dk ** -0.5, pair=pair),
        out_shape=(
            jax.ShapeDtypeStruct((t, hk), BF16),
            jax.ShapeDtypeStruct((t, hk), BF16),
            jax.ShapeDtypeStruct((hk, t), BF16),
            jax.ShapeDtypeStruct((hk, t), BF16),
            jax.ShapeDtypeStruct((hk, t), BF16),
            jax.ShapeDtypeStruct((hk, t), F32),
        ),
        grid=(t // rows,),
        in_specs=[
            pl.BlockSpec((rows, d), lambda i: (i, 0)),
            const((1, d)),
            const((d, hk)),
            const((hk, d)),
            const((d, LANES)),
            const((LANES, hk)),
            const((hk, LANES)),
            const((1, hk)),
            const((hk, 1)),
            const((2 * pair, pair)),
            const((pair, 4 * pair)),
        ],
        out_specs=(row_out, row_out, col_out, col_out, col_out, col_out),
        compiler_params=_params("parallel"),
        name="gla_prep",
    )(h, norm_g.reshape(1, d), wq, wkt, wg, wgu, wgu.T, b_gate.reshape(1, hk),
      b_gate.reshape(hk, 1), c_row, c_col)


def _gla_core_kernel(qe_ref, qes_ref, ket_ref, klt_ref, ktt_ref, dect_ref, v_ref, r_ref, hg_ref,
                     o_ref, s_ref, *, heads, chunk):
    @pl.when(pl.program_id(1) == 0)
    def _():
        s_ref[...] = jnp.zeros_like(s_ref)

    pair = 2 * chunk
    dk = qe_ref.shape[1] // heads
    dv = v_ref.shape[1] // heads
    row = lax.broadcasted_iota(jnp.int32, (pair, pair), 0)
    col = lax.broadcasted_iota(jnp.int32, (pair, pair), 1)
    second = row >= chunk
    first_keys = col < chunk
    m_intra = jnp.logical_and(col <= row, jnp.logical_not(jnp.logical_xor(second, col >= chunk)))
    m_cross = jnp.logical_and(second, first_keys)
    for h in range(heads):
        ks = slice(h * dk, (h + 1) * dk)
        vs = slice(h * dv, (h + 1) * dv)
        qe = qe_ref[:, ks]
        v = v_ref[:, vs]
        att = jnp.where(m_intra, _dot(qe, ket_ref[ks, :]),
                        jnp.where(m_cross, _dot(qe, klt_ref[ks, :]), 0.0))
        state = s_ref[h]
        o = _dot(att.astype(BF16), v) + _dot(qes_ref[:, ks], state.astype(BF16))
        dec = dect_ref[ks, :]
        dec = jnp.concatenate([dec] * (dv // pair), axis=1) if dv > pair else dec[:, :dv]
        s_ref[h] = dec * state + _dot(ktt_ref[ks, :], v)
        on = o * _rms_inv(o) * hg_ref[...]
        r = r_ref[:, vs].astype(F32)
        o_ref[:, vs] = (on * (r * _sigmoid(r))).astype(BF16)


def _gla_core(qe, qes, ket, klt, ktt, dect, vr, head_g, *, batch):
    t, hk = qe.shape
    hv = vr.shape[1] // 2
    pair = 2 * GLA_CHUNK
    npair = t // batch // pair
    dk = hk // GLA_HEADS
    dv = hv // GLA_HEADS
    row_k = pl.BlockSpec((pair, hk), lambda b, p: (b * npair + p, 0))
    col_k = pl.BlockSpec((hk, pair), lambda b, p: (0, b * npair + p))
    return pl.pallas_call(
        functools.partial(_gla_core_kernel, heads=GLA_HEADS, chunk=GLA_CHUNK),
        out_shape=jax.ShapeDtypeStruct((t, hv), BF16),
        grid=(batch, npair),
        in_specs=[
            row_k, row_k, col_k, col_k, col_k, col_k,
            pl.BlockSpec((pair, hv), lambda b, p: (b * npair + p, 0)),
            pl.BlockSpec((pair, hv), lambda b, p: (b * npair + p, 1)),
            pl.BlockSpec((1, dv), lambda b, p: (0, 0)),
        ],
        out_specs=pl.BlockSpec((pair, hv), lambda b, p: (b * npair + p, 0)),
        scratch_shapes=[pltpu.VMEM((GLA_HEADS, dk, dv), F32)],
        compiler_params=_params("parallel", "arbitrary"),
        name="gla_core",
    )(qe, qes, ket, klt, ktt, dect, vr, vr, head_g.reshape(1, dv))


def _gla_layer(h, batch, norm_g, w_in, w_gate_up, b_gate, head_g, w_out):
    d = h.shape[1]
    hk = d // 2
    qe, qes, ket, klt, ktt, dect = _gla_prep(h, norm_g, w_in, w_gate_up, b_gate)
    vr = _norm_matmul(h, norm_g, w_in[:, 2 * hk:2 * hk + 2 * d], rows=PROJ_ROWS, cols=d,
                      out_dtype=BF16)
    og = _gla_core(qe, qes, ket, klt, ktt, dect, vr, head_g, batch=batch)
    return _matmul_residual(og, w_out.astype(BF16), h, rows=PROJ_ROWS)


META_E, META_W, META_RANK = 0, 2, 4


def _router_kernel(h_ref, g_ref, w_ref, b_ref, tril_ref, meta_ref, cnt_ref, base_ref,
                   *, groups, per_group):
    @pl.when(pl.program_id(0) == 0)
    def _():
        base_ref[...] = jnp.zeros_like(base_ref)

    x = h_ref[...]
    xn = x * _rms_inv(x) * g_ref[...]
    logits = _dot3(xn, w_ref[...]) + b_ref[...]
    lane = lax.broadcasted_iota(jnp.int32, logits.shape, 1).astype(F32)
    neg = -jnp.inf
    far = float(LANES)

    def first_max(vals):
        m = jnp.max(vals, axis=-1, keepdims=True)
        return m, jnp.min(jnp.where(vals == m, lane, far), axis=-1, keepdims=True)

    gl = jnp.where(lane < groups, logits, neg)
    gmax, gidx = first_max(gl)
    p_group = 1.0 / jnp.sum(jnp.exp(gl - gmax), axis=-1, keepdims=True)
    lo = groups + per_group * gidx
    el = jnp.where(jnp.logical_and(lane >= lo, lane < lo + per_group), logits, neg)
    v1, i1 = first_max(el)
    v2, i2 = first_max(jnp.where(lane == i1, neg, el))
    t = jnp.exp(v2 - v1)
    w1 = p_group / (1.0 + t)
    w2 = p_group * t / (1.0 + t)

    oh1 = lane == i1
    oh2 = lane == i2
    onehot = jnp.where(jnp.logical_or(oh1, oh2), 1.0, 0.0).astype(BF16)
    seen = base_ref[...] + _dot(tril_ref[...], onehot)
    rank1 = jnp.sum(jnp.where(oh1, seen, 0.0), axis=-1, keepdims=True) - 1.0
    rank2 = jnp.sum(jnp.where(oh2, seen, 0.0), axis=-1, keepdims=True) - 1.0
    base_ref[...] = seen[-1:, :]
    cnt_ref[...] = jnp.broadcast_to(seen[-1:, :], cnt_ref.shape)

    rec = jnp.zeros_like(logits)
    for k, val in ((META_E, i1 - groups), (META_E + 1, i2 - groups), (META_W, w1),
                   (META_W + 1, w2), (META_RANK, rank1), (META_RANK + 1, rank2)):
        rec = jnp.where(lane == k, val, rec)
    meta_ref[...] = rec


def _moe_router(h, norm_g, w_group, b_group, w_router, b_router):
    t, d = h.shape
    rows = ROUTER_ROWS
    groups = w_group.shape[1]
    ne = w_router.shape[1]
    pad = LANES - groups - ne
    w = jnp.pad(jnp.concatenate([w_group, w_router], axis=1), ((0, 0), (0, pad)))
    b = jnp.pad(jnp.concatenate([b_group, b_router]), (0, pad)).reshape(1, LANES)
    tril = jnp.asarray(np.tril(np.ones((rows, rows), np.float32)), BF16)
    meta, cnt = pl.pallas_call(
        functools.partial(_router_kernel, groups=groups, per_group=ne // groups),
        out_shape=(jax.ShapeDtypeStruct((t, LANES), F32), jax.ShapeDtypeStruct((8, LANES), F32)),
        grid=(t // rows,),
        in_specs=[
            pl.BlockSpec((rows, d), lambda i: (i, 0)),
            pl.BlockSpec((1, d), lambda i: (0, 0)),
            pl.BlockSpec((d, LANES), lambda i: (0, 0)),
            pl.BlockSpec((1, LANES), lambda i: (0, 0)),
            pl.BlockSpec((rows, rows), lambda i: (0, 0)),
        ],
        out_specs=(pl.BlockSpec((rows, LANES), lambda i: (i, 0)),
                   pl.BlockSpec((8, LANES), lambda i: (0, 0))),
        scratch_shapes=[pltpu.VMEM((1, LANES), F32)],
        compiler_params=_params("arbitrary"),
        name="moe_router",
    )(h, norm_g.reshape(1, d), w, b, tril)
    return meta, cnt[0, groups:groups + ne]


def _inverse_kernel(pos0_ref, pos1_ref, src_ref):
    step = pl.program_id(0)
    half = pl.num_programs(0) // 2
    slots = src_ref.shape[0] // INVERSE_STEPS
    tokens = pos0_ref.shape[0] // INVERSE_STEPS

    @pl.when(step < half)
    def _():
        def fill(i, carry):
            for u in range(DMA_UNROLL):
                src_ref[step * slots + i * DMA_UNROLL + u] = 0
            return carry

        lax.fori_loop(0, slots // DMA_UNROLL, fill, 0)

    @pl.when(step >= half)
    def _():
        def place(i, carry):
            for u in range(DMA_UNROLL):
                tok = (step - half) * tokens + i * DMA_UNROLL + u
                src_ref[pos0_ref[tok]] = tok
                src_ref[pos1_ref[tok]] = tok
            return carry

        lax.fori_loop(0, tokens // DMA_UNROLL, place, 0)


def _moe_inverse(pos0, pos1, slots):
    smem = pl.BlockSpec(memory_space=pltpu.SMEM)
    return pl.pallas_call(
        _inverse_kernel,
        out_shape=jax.ShapeDtypeStruct((slots,), jnp.int32),
        grid=(2 * INVERSE_STEPS,),
        in_specs=[smem, smem],
        out_specs=smem,
        compiler_params=_params("arbitrary"),
        name="moe_inverse",
    )(pos0, pos1)


def _row_copy(src_hbm, row, dst, r_tile, r_sub, sem):
    return pltpu.make_async_copy(src_hbm.at[pl.ds(row, 1)], dst.at[r_tile, pl.ds(r_sub, 1)], sem)


def _rows_wait(dst, sem):
    pltpu.make_async_copy(dst, dst, sem).wait()


def _experts_kernel(src_ref, texp_ref, nused_ref, h_hbm, g_ref, wg_hbm, wu_hbm, wd_hbm, y_ref,
                    xbuf, xn_ref, wgs, wus, wds, wgb, wub, wdb, sem, wsem, *, rows, layer):
    i = pl.program_id(0)
    nused = nused_ref[0]
    expert = texp_ref[i]

    def start(tile, slot):
        base = tile * rows
        for r in range(rows):
            _row_copy(h_hbm, src_ref[base + r], xbuf.at[slot], r // SUBLANES, r % SUBLANES,
                      sem.at[slot]).start()

    def weight_copies(e):
        return (pltpu.make_async_copy(wg_hbm.at[layer, e], wgs, wsem.at[0]),
                pltpu.make_async_copy(wu_hbm.at[layer, e], wus, wsem.at[1]),
                pltpu.make_async_copy(wd_hbm.at[layer, e], wds, wsem.at[2]))

    last = nused - 1

    @pl.when(i == 0)
    def _():
        for c in weight_copies(expert):
            c.start()
        for j in range(GATHER_DEPTH - 1):
            start(jnp.minimum(j, last), j)

    fresh = jnp.logical_or(i == 0, expert != texp_ref[jnp.maximum(i - 1, 0)])

    @pl.when(jnp.logical_and(fresh, i < nused))
    def _():
        for c in weight_copies(expert):
            c.wait()
        wgb[...] = (wgs[...] * g_ref[...]).astype(BF16)
        wub[...] = (wus[...] * g_ref[...]).astype(BF16)
        wdb[...] = wds[...].astype(BF16)

    nxt = texp_ref[jnp.minimum(i + 1, pl.num_programs(0) - 1)]

    @pl.when(jnp.logical_and(i + 1 < nused, nxt != expert))
    def _():
        for c in weight_copies(nxt):
            c.start()

    @pl.when(i < nused)
    def _():
        slot = i % GATHER_DEPTH
        _rows_wait(xbuf.at[slot], sem.at[slot])
        x = xbuf[slot].reshape(xn_ref.shape)
        xn_ref[...] = x.astype(BF16)
        inv = _rms_inv(x)
        ahead = i + GATHER_DEPTH - 1
        start(jnp.minimum(ahead, last), ahead % GATHER_DEPTH)
        xn = xn_ref[...]
        gate = _dot(xn, wgb[...]) * inv
        up = _dot(xn, wub[...]) * inv
        act = (gate * _sigmoid(gate)) * up
        y_ref[...] = _dot(act.astype(BF16), wdb[...])

    @pl.when(i == last)
    def _():
        for j in range(1, GATHER_DEPTH):
            slot = (i + j) % GATHER_DEPTH
            _rows_wait(xbuf.at[slot], sem.at[slot])

    @pl.when(i >= nused)
    def _():
        y_ref[...] = jnp.zeros_like(y_ref)


def _moe_experts(h, norm_g, layer, w_gate, w_up, w_down, src, tile_expert, n_used):
    t, d = h.shape
    _, ne, _, f = w_gate.shape
    rows = EXPERT_ROWS
    slots = src.shape[0]
    hbm = pl.BlockSpec(memory_space=pl.ANY)
    return pl.pallas_call(
        functools.partial(_experts_kernel, rows=rows, layer=layer),
        out_shape=jax.ShapeDtypeStruct((slots, d), F32),
        grid_spec=pltpu.PrefetchScalarGridSpec(
            num_scalar_prefetch=3,
            grid=(slots // rows,),
            in_specs=[hbm, pl.BlockSpec((d, 1), lambda i, *_: (0, 0)), hbm, hbm, hbm],
            out_specs=pl.BlockSpec((rows, d), lambda i, *_: (i, 0)),
            scratch_shapes=[pltpu.VMEM((GATHER_DEPTH, rows // SUBLANES, SUBLANES, d), F32),
                            pltpu.VMEM((rows, d), BF16),
                            pltpu.VMEM((d, f), F32), pltpu.VMEM((d, f), F32),
                            pltpu.VMEM((f, d), F32),
                            pltpu.VMEM((d, f), BF16), pltpu.VMEM((d, f), BF16),
                            pltpu.VMEM((f, d), BF16),
                            pltpu.SemaphoreType.DMA((GATHER_DEPTH,)),
                            pltpu.SemaphoreType.DMA((3,))],
        ),
        compiler_params=_params("arbitrary"),
        name="moe_experts",
    )(src, tile_expert, n_used, h, norm_g.reshape(d, 1), w_gate, w_up, w_down)


def _combine_kernel(pos0_ref, pos1_ref, h_ref, meta_ref, y_hbm, o_ref, buf0, buf1, sem, *, rows):
    i = pl.program_id(0)

    def start(tile):
        slot = tile % 2

        def body(g, carry):
            for u in range(SUBLANES):
                tok = tile * rows + g * SUBLANES + u
                _row_copy(y_hbm, pos0_ref[tok], buf0.at[slot], g, u, sem.at[0, slot]).start()
                _row_copy(y_hbm, pos1_ref[tok], buf1.at[slot], g, u, sem.at[1, slot]).start()
            return carry

        lax.fori_loop(0, rows // SUBLANES, body, 0)

    @pl.when(i == 0)
    def _():
        start(0)

    @pl.when(i + 1 < pl.num_programs(0))
    def _():
        start(i + 1)

    slot = i % 2
    _rows_wait(buf0.at[slot], sem.at[0, slot])
    _rows_wait(buf1.at[slot], sem.at[1, slot])
    meta = meta_ref[...]
    w0 = meta[:, META_W:META_W + 1]
    w1 = meta[:, META_W + 1:META_W + 2]
    shape = h_ref.shape
    o_ref[...] = h_ref[...] + w0 * buf0[slot].reshape(shape) + w1 * buf1[slot].reshape(shape)


def _moe_combine(h, meta, y, pos0, pos1):
    t, d = h.shape
    rows = COMBINE_ROWS
    gather_buf = pltpu.VMEM((2, rows // SUBLANES, SUBLANES, d), F32)
    return pl.pallas_call(
        functools.partial(_combine_kernel, rows=rows),
        out_shape=jax.ShapeDtypeStruct((t, d), F32),
        grid_spec=pltpu.PrefetchScalarGridSpec(
            num_scalar_prefetch=2,
            grid=(t // rows,),
            in_specs=[
                pl.BlockSpec((rows, d), lambda i, *_: (i, 0)),
                pl.BlockSpec((rows, LANES), lambda i, *_: (i, 0)),
                pl.BlockSpec(memory_space=pl.ANY),
            ],
            out_specs=pl.BlockSpec((rows, d), lambda i, *_: (i, 0)),
            scratch_shapes=[gather_buf, gather_buf, pltpu.SemaphoreType.DMA((2, 2))],
        ),
        compiler_params=_params("arbitrary"),
        name="moe_combine",
    )(pos0, pos1, h, meta, y)


def _moe_layer(h, layer, norm_g, w_group, b_group, w_router, b_router, w_gate, w_up, w_down):
    t, _ = h.shape
    ne = w_router.shape[1]
    rows = EXPERT_ROWS
    meta, counts = _moe_router(h, norm_g, w_group, b_group, w_router, b_router)
    counts = counts.astype(jnp.int32)
    padded = (counts + rows - 1) // rows * rows
    ends = jnp.cumsum(padded)
    starts = ends - padded
    expert = meta[:, META_E:META_E + 2].astype(jnp.int32)
    rank = meta[:, META_RANK:META_RANK + 2].astype(jnp.int32)
    pos = starts[expert] + rank
    slots = 2 * t + ne * rows
    n_tiles = slots // rows
    n_used = (ends[-1] // rows).astype(jnp.int32)
    tile_start = jnp.arange(n_tiles, dtype=jnp.int32) * rows
    tile_start = jnp.minimum(tile_start, ends[-1] - 1)
    tile_expert = jnp.sum(tile_start[:, None] >= ends[None, :], axis=1, dtype=jnp.int32)
    tile_expert = jnp.minimum(tile_expert, ne - 1)
    pos0 = pos[:, 0]
    pos1 = pos[:, 1]
    src = _moe_inverse(pos0, pos1, slots)
    y = _moe_experts(h, norm_g, layer, w_gate, w_up, w_down, src, tile_expert, n_used.reshape(1))
    return _moe_combine(h, meta, y, pos0, pos1)


def _swa_proj_kernel(x_ref, gq_ref, gkv_ref, wq_ref, wkv_ref, kg_ref, q_ref, k_ref, v_ref):
    x = x_ref[...]
    xs = x * _rms_inv(x)
    q_ref[...] = _dot((xs * gq_ref[...]).astype(BF16), wq_ref[...]).astype(BF16)
    kv = _dot((xs * gkv_ref[...]).astype(BF16), wkv_ref[...])
    half = kv.shape[1] // 2
    v_ref[...] = kv[:, half:].astype(BF16)
    for hd in range(half // LANES):
        sl = slice(hd * LANES, (hd + 1) * LANES)
        k = kv[:, sl]
        k_ref[:, sl] = (k * _rms_inv(k) * kg_ref[...]).astype(BF16)


def _swa_proj(h, q_norm_g, kv_norm_g, w_q, w_kv, k_norm):
    t, d = h.shape
    hd = SWA_HEAD_DIM
    kvh = w_kv.shape[1] // (2 * hd)
    rows = SWA_PROJ_ROWS
    rep = LANES // hd
    w_dup = jnp.broadcast_to(w_kv.astype(BF16).reshape(d, 2 * kvh, 1, hd), (d, 2 * kvh, rep, hd))
    w_dup = w_dup.reshape(d, 2 * kvh * LANES)
    kg = jnp.tile(k_norm, rep).reshape(1, LANES)
    const = lambda shape: pl.BlockSpec(shape, lambda i: (0, 0))
    return pl.pallas_call(
        _swa_proj_kernel,
        out_shape=(
            jax.ShapeDtypeStruct((t, w_q.shape[1]), BF16),
            jax.ShapeDtypeStruct((t, kvh * LANES), BF16),
            jax.ShapeDtypeStruct((t, kvh * LANES), BF16),
        ),
        grid=(t // rows,),
        in_specs=[
            pl.BlockSpec((rows, d), lambda i: (i, 0)),
            const((1, d)), const((1, d)),
            const((d, w_q.shape[1])), const((d, 2 * kvh * LANES)), const((1, LANES)),
        ],
        out_specs=(
            pl.BlockSpec((rows, w_q.shape[1]), lambda i: (i, 0)),
            pl.BlockSpec((rows, kvh * LANES), lambda i: (i, 0)),
            pl.BlockSpec((rows, kvh * LANES), lambda i: (i, 0)),
        ),
        compiler_params=_params("parallel"),
        name="swa_proj",
    )(h, q_norm_g.reshape(1, d), kv_norm_g.reshape(1, d), w_q.astype(BF16), w_dup, kg)


def _swa_attn_kernel(q_ref, kp_ref, kc_ref, vp_ref, vc_ref, bias_ref, sinkw_ref, qg_ref,
                     blk_ref, ones_ref, o_ref, *, kvh, pairs, window):
    hd = LANES // 2
    lane = lax.broadcasted_iota(jnp.int32, (window, LANES), 1)
    low = lane < hd
    width = 2 * window
    olow = lax.broadcasted_iota(jnp.int32, (pairs * window, LANES), 1) < hd
    zero = jnp.zeros((), BF16)

    def block_diag(prev, cur):
        return jnp.concatenate([jnp.where(low, prev, zero), jnp.where(low, cur, zero),
                                jnp.where(low, zero, prev), jnp.where(low, zero, cur)], axis=0)

    for h in range(kvh):
        ksl = slice(h * LANES, (h + 1) * LANES)
        kk = block_diag(kp_ref[:, ksl], kc_ref[:, ksl])
        vv = block_diag(vp_ref[:, ksl], vc_ref[:, ksl])
        base = h * pairs * LANES
        q2 = jnp.concatenate([q_ref[:, base + p * LANES:base + (p + 1) * LANES]
                              for p in range(pairs)], axis=0).astype(F32)
        ms = _dot((q2 * q2).astype(BF16), blk_ref[...])
        qn = (q2 * lax.rsqrt(ms + RMS_EPS) * qg_ref[...]).astype(BF16)
        s = _dot_nt(qn, kk) + bias_ref[0, h]
        probs = []
        row_max = []
        for half in range(2):
            sh = s[:, half * width:(half + 1) * width]
            m = jnp.max(sh, axis=-1, keepdims=True)
            probs.append(jnp.exp2(sh - m).astype(BF16))
            row_max.append(m)
        mixed = _dot(jnp.concatenate(probs, axis=1), jnp.concatenate([vv, ones_ref[...]], axis=1))
        den = mixed[:, LANES:] + jnp.exp2(sinkw_ref[h] - jnp.where(olow, row_max[0], row_max[1]))
        o2 = mixed[:, :LANES] * (1.0 / den)
        for p in range(pairs):
            o_ref[:, base + p * LANES:base + (p + 1) * LANES] = (
                o2[p * window:(p + 1) * window].astype(BF16))


def _t5_bucket(dist):
    max_exact = NUM_BUCKETS // 2
    n = np.maximum(dist, 0)
    large = max_exact + (np.log(np.maximum(n, max_exact) / max_exact)
                         / math.log(REL_MAX_DISTANCE / max_exact)
                         * (NUM_BUCKETS - max_exact)).astype(np.int32)
    return np.where(n < max_exact, n, np.minimum(large, NUM_BUCKETS - 1)).astype(np.int32)


def _swa_attn(q, kd, vd, rel_bias, q_norm, sinks, *, batch):
    t, dq = q.shape
    kvh = kd.shape[1] // LANES
    hq = dq // SWA_HEAD_DIM
    pairs = hq // kvh // 2
    w = WINDOW
    nblk = t // batch // w
    row = np.arange(w)[:, None]
    col = np.arange(2 * w)[None, :]
    dist = w + row - col
    band = (dist >= 0) & (dist < w)
    onehot = (_t5_bucket(dist)[..., None] == np.arange(NUM_BUCKETS)).astype(np.float32)
    bias = jnp.dot(jnp.asarray(onehot), rel_bias.astype(F32), precision=lax.Precision.HIGHEST)
    bias = jnp.where(jnp.asarray(band)[..., None], bias, -jnp.inf)
    bias = bias.transpose(2, 0, 1).reshape(kvh, pairs, 2, w, 2 * w).transpose(0, 1, 3, 2, 4)
    bias = bias.reshape(kvh, pairs * w, 4 * w) * LOG2_E
    prev_cols = jnp.asarray((np.arange(4 * w) // w) % 2 == 0)
    bias = jnp.stack([bias, jnp.where(prev_cols, -jnp.inf, bias)])
    sink = sinks.astype(F32).reshape(kvh, pairs, 1, 2, 1) * LOG2_E
    sink_wide = jnp.broadcast_to(sink, (kvh, pairs, w, 2, SWA_HEAD_DIM))
    sink_wide = sink_wide.reshape(kvh, pairs * w, LANES)
    qg = jnp.tile(q_norm, LANES // SWA_HEAD_DIM) * (SWA_HEAD_DIM ** -0.5 * LOG2_E)
    lane = np.arange(LANES)
    blk = ((lane[:, None] // SWA_HEAD_DIM) == (lane[None, :] // SWA_HEAD_DIM)) / SWA_HEAD_DIM
    ones = (np.arange(4 * w)[:, None] // (2 * w)) == (lane[None, :] // SWA_HEAD_DIM)
    cur = lambda b, n: (b * nblk + n, 0)
    prev = lambda b, n: (b * nblk + jnp.maximum(n - 1, 0), 0)
    const2 = lambda shape: pl.BlockSpec(shape, lambda b, n: (0, 0))
    const3 = lambda shape: pl.BlockSpec(shape, lambda b, n: (0, 0, 0))
    return pl.pallas_call(
        functools.partial(_swa_attn_kernel, kvh=kvh, pairs=pairs, window=w),
        out_shape=jax.ShapeDtypeStruct((t, dq), BF16),
        grid=(batch, nblk),
        in_specs=[
            pl.BlockSpec((w, dq), cur),
            pl.BlockSpec((w, kvh * LANES), prev), pl.BlockSpec((w, kvh * LANES), cur),
            pl.BlockSpec((w, kvh * LANES), prev), pl.BlockSpec((w, kvh * LANES), cur),
            pl.BlockSpec((1, kvh, pairs * w, 4 * w), lambda b, n: (jnp.where(n == 0, 1, 0), 0, 0, 0)),
            const3((kvh, pairs * w, LANES)),
            const2((1, LANES)), const2((LANES, LANES)), const2((4 * w, LANES)),
        ],
        out_specs=pl.BlockSpec((w, dq), cur),
        compiler_params=_params("parallel", "parallel"),
        name="swa_attn",
    )(q, kd, kd, vd, vd, bias, sink_wide, qg.reshape(1, LANES), jnp.asarray(blk, BF16),
      jnp.asarray(ones, BF16))


def _swa_layer(h, batch, kv_norm, w_kv, k_norm, rel_bias, norm_g, w_q, q_norm, sinks, w_out):
    q, kd, vd = _swa_proj(h, norm_g, kv_norm, w_q, w_kv, k_norm)
    o = _swa_attn(q, kd, vd, rel_bias, q_norm, sinks, batch=batch)
    return _matmul_residual(o, w_out.astype(BF16), h, rows=PROJ_ROWS)


def kernel(x, gla_norm, gla_w_in, gla_w_gate_up, gla_b_gate, gla_head_norm, gla_w_out, kv_norm, w_kv, k_norm, rel_bias, swa_norm, swa_w_q, swa_q_norm, swa_sinks, swa_w_out, moe_norm, moe_w_group, moe_b_group, moe_w_router, moe_b_router, moe_w_gate, moe_w_up, moe_w_down):
    batch, seq, d = x.shape
    assert gla_norm.shape[0] == 1 and swa_norm.shape[0] == 1 and moe_norm.shape[0] == 2
    h = x.reshape(batch * seq, d)
    h = _gla_layer(h, batch, gla_norm[0], gla_w_in[0], gla_w_gate_up[0], gla_b_gate[0],
                   gla_head_norm[0], gla_w_out[0])
    h = _moe_layer(h, 0, moe_norm[0], moe_w_group[0], moe_b_group[0], moe_w_router[0],
                   moe_b_router[0], moe_w_gate, moe_w_up, moe_w_down)
    h = _swa_layer(h, batch, kv_norm, w_kv, k_norm, rel_bias, swa_norm[0], swa_w_q[0],
                   swa_q_norm[0], swa_sinks[0], swa_w_out[0])
    h = _moe_layer(h, 1, moe_norm[1], moe_w_group[1], moe_b_group[1], moe_w_router[1],
                   moe_b_router[1], moe_w_gate, moe_w_up, moe_w_down)
    return h.reshape(batch, seq, d)
```

```python
import functools
import math

import jax
import jax.numpy as jnp
import numpy as np
from jax import lax
from jax.experimental import pallas as pl
from jax.experimental.pallas import tpu as pltpu

F32 = jnp.float32
BF16 = jnp.bfloat16

RMS_EPS = 1e-6
GLA_HEADS = 4
GLA_GATE_RANK = 16
GLA_GATE_NORMALIZER = 16.0
GLA_LOG_GATE_MIN = -1.0
GLA_CHUNK = 64
SWA_HEAD_DIM = 64
SWA_GROUPS = 8
WINDOW = 128
NUM_BUCKETS = 32
REL_MAX_DISTANCE = 128
MOE_GROUPS = 4
MOE_EXPERTS_PER_GROUP = 8

LANES = 128
SUBLANES = 8
VMEM_LIMIT_BYTES = 56 * 1024 * 1024

PREP_ROWS = 256
PROJ_ROWS = 512
ROUTER_ROWS = 512
EXPERT_ROWS = 512
COMBINE_ROWS = 256
SWA_PROJ_ROWS = 256
LOG2_E = math.log2(math.e)
DMA_UNROLL = 8
GATHER_DEPTH = 3
INVERSE_STEPS = 64


def _params(*semantics):
    return pltpu.CompilerParams(dimension_semantics=semantics, vmem_limit_bytes=VMEM_LIMIT_BYTES)


def _dot(a, b):
    return jnp.dot(a, b, preferred_element_type=F32)


def _dot_nt(a, b):
    return lax.dot_general(a, b, (((1,), (1,)), ((), ())), preferred_element_type=F32)


def _split(x):
    hi = x.astype(BF16)
    lo = (x - hi.astype(F32)).astype(BF16)
    return hi, lo


def _dot3(a, b, dot=_dot):
    ah, al = _split(a)
    bh, bl = _split(b)
    return dot(ah, bh) + dot(al, bh) + dot(ah, bl)


def _rms_inv(x):
    return lax.rsqrt(jnp.mean(x * x, axis=-1, keepdims=True) + RMS_EPS)


def _sigmoid(x):
    return 1.0 / (1.0 + jnp.exp(-x))


def _log_sigmoid(x):
    return jnp.minimum(x, 0.0) - jnp.log1p(jnp.exp(-jnp.abs(x)))


def _norm_matmul_kernel(x_ref, w_ref, o_ref, xb_ref, inv_ref):
    @pl.when(pl.program_id(1) == 0)
    def _():
        x = x_ref[...]
        xb_ref[...] = x.astype(BF16)
        inv_ref[...] = _rms_inv(x)

    o_ref[...] = (_dot(xb_ref[...], w_ref[...]) * inv_ref[...]).astype(o_ref.dtype)


def _norm_matmul(x, g, w, *, rows, cols, out_dtype):
    t, d = x.shape
    n = w.shape[1]
    return pl.pallas_call(
        _norm_matmul_kernel,
        out_shape=jax.ShapeDtypeStruct((t, n), out_dtype),
        grid=(t // rows, n // cols),
        in_specs=[
            pl.BlockSpec((rows, d), lambda i, j: (i, 0)),
            pl.BlockSpec((d, cols), lambda i, j: (0, j)),
        ],
        out_specs=pl.BlockSpec((rows, cols), lambda i, j: (i, j)),
        scratch_shapes=[pltpu.VMEM((rows, d), BF16), pltpu.VMEM((rows, 1), F32)],
        compiler_params=_params("parallel", "arbitrary"),
        name="norm_matmul",
    )(x, (g[:, None] * w).astype(BF16))


def _matmul_residual_kernel(a_ref, w_ref, res_ref, o_ref):
    o_ref[...] = res_ref[...] + _dot(a_ref[...], w_ref[...])


def _matmul_residual(a, w, res, *, rows):
    t, k = a.shape
    n = w.shape[1]
    return pl.pallas_call(
        _matmul_residual_kernel,
        out_shape=jax.ShapeDtypeStruct((t, n), F32),
        grid=(t // rows,),
        in_specs=[
            pl.BlockSpec((rows, k), lambda i: (i, 0)),
            pl.BlockSpec((k, n), lambda i: (0, 0)),
            pl.BlockSpec((rows, n), lambda i: (i, 0)),
        ],
        out_specs=pl.BlockSpec((rows, n), lambda i: (i, 0)),
        compiler_params=_params("parallel"),
        name="matmul_residual",
    )(a, w, res)


def _pair_constants(chunk):
    pair = 2 * chunk
    i = np.arange(pair)[:, None]
    j = np.arange(pair)[None, :]
    tri = (j <= i).astype(np.float32)
    sel = ((i >= chunk) & (j < chunk)).astype(np.float32)
    blk = ((i // chunk) == (j // chunk)).astype(np.float32)
    ones = np.ones((pair, pair), np.float32)
    c_row = np.concatenate([tri, sel], axis=0)
    c_col = np.concatenate([tri.T, sel.T, blk, ones], axis=1)
    return jnp.asarray(c_row, BF16), jnp.asarray(c_col, BF16)


def _gla_prep_kernel(x_ref, g_ref, wq_ref, wkt_ref, wg_ref, wgu_ref, wgut_ref, bgr_ref, bgc_ref,
                     crow_ref, ccol_ref, qe_ref, qes_ref, ket_ref, klt_ref, ktt_ref, dect_ref,
                     *, scale, pair):
    x = x_ref[...]
    xn = (x * _rms_inv(x) * g_ref[...]).astype(BF16)
    q = _dot(xn, wq_ref[...]) * scale
    kt = _dot_nt(wkt_ref[...], xn)
    glr = _dot(xn, wg_ref[...]).astype(BF16)
    z = _dot(glr, wgu_ref[...]) + bgr_ref[...]
    zt = _dot_nt(wgut_ref[...], glr) + bgc_ref[...]
    la = jnp.maximum(_log_sigmoid(z) / GLA_GATE_NORMALIZER, GLA_LOG_GATE_MIN)
    lat = jnp.maximum(_log_sigmoid(zt) / GLA_GATE_NORMALIZER, GLA_LOG_GATE_MIN)
    crow = crow_ref[...]
    ccol = ccol_ref[...]
    for p in range(x.shape[0] // pair):
        rows = slice(p * pair, (p + 1) * pair)
        hi, lo = _split(la[rows])
        cr = _dot(crow, hi) + _dot(crow, lo)
        b_pair = cr[:pair]
        b_chunk = b_pair - cr[pair:]
        qp = q[rows]
        qe_ref[rows, :] = (qp * jnp.exp(b_chunk)).astype(BF16)
        qes_ref[rows, :] = (qp * jnp.exp(b_pair)).astype(BF16)
        hi, lo = _split(lat[:, rows])
        cc = _dot(hi, ccol) + _dot(lo, ccol)
        bt_pair = cc[:, :pair]
        bt_chunk = bt_pair - cc[:, pair:2 * pair]
        end_chunk = cc[:, 2 * pair:3 * pair]
        end_pair = cc[:, 3 * pair:]
        ktp = kt[:, rows]
        ket_ref[:, rows] = (ktp * jnp.exp(-bt_chunk)).astype(BF16)
        klt_ref[:, rows] = (ktp * jnp.exp(end_chunk - bt_chunk)).astype(BF16)
        ktt_ref[:, rows] = (ktp * jnp.exp(end_pair - bt_pair)).astype(BF16)
        dect_ref[:, rows] = jnp.exp(end_pair)


def _gla_prep(h, norm_g, w_in, w_gate_up, b_gate):
    t, d = h.shape
    hk = d // 2
    rank = w_gate_up.shape[0]
    pair = 2 * GLA_CHUNK
    rows = PREP_ROWS
    dk = hk // GLA_HEADS
    wq = w_in[:, :hk].astype(BF16)
    wkt = w_in[:, hk:2 * hk].T.astype(BF16)
    wg = jnp.pad(w_in[:, 3 * d:], ((0, 0), (0, LANES - rank))).astype(BF16)
    wgu = jnp.pad(w_gate_up, ((0, LANES - rank), (0, 0))).astype(BF16)
    c_row, c_col = _pair_constants(GLA_CHUNK)
    const = lambda shape: pl.BlockSpec(shape, lambda i: (0, 0))
    row_out = pl.BlockSpec((rows, hk), lambda i: (i, 0))
    col_out = pl.BlockSpec((hk, rows), lambda i: (0, i))
    return pl.pallas_call(
        functools.partial(_gla_prep_kernel, scale=dk ** -0.5, pair=pair),
        out_shape=(
            jax.ShapeDtypeStruct((t, hk), BF16),
            jax.ShapeDtypeStruct((t, hk), BF16),
            jax.ShapeDtypeStruct((hk, t), BF16),
            jax.ShapeDtypeStruct((hk, t), BF16),
            jax.ShapeDtypeStruct((hk, t), BF16),
            jax.ShapeDtypeStruct((hk, t), F32),
        ),
        grid=(t // rows,),
        in_specs=[
            pl.BlockSpec((rows, d), lambda i: (i, 0)),
            const((1, d)),
            const((d, hk)),
            const((hk, d)),
            const((d, LANES)),
            const((LANES, hk)),
            const((hk, LANES)),
            const((1, hk)),
            const((hk, 1)),
            const((2 * pair, pair)),
            const((pair, 4 * pair)),
        ],
        out_specs=(row_out, row_out, col_out, col_out, col_out, col_out),
        compiler_params=_params("parallel"),
        name="gla_prep",
    )(h, norm_g.reshape(1, d), wq, wkt, wg, wgu, wgu.T, b_gate.reshape(1, hk),
      b_gate.reshape(hk, 1), c_row, c_col)


def _gla_core_kernel(qe_ref, qes_ref, ket_ref, klt_ref, ktt_ref, dect_ref, v_ref, r_ref, hg_ref,
                     o_ref, s_ref, *, heads, chunk):
    @pl.when(pl.program_id(1) == 0)
    def _():
        s_ref[...] = jnp.zeros_like(s_ref)

    pair = 2 * chunk
    dk = qe_ref.shape[1] // heads
    dv = v_ref.shape[1] // heads
    row = lax.broadcasted_iota(jnp.int32, (pair, pair), 0)
    col = lax.broadcasted_iota(jnp.int32, (pair, pair), 1)
    second = row >= chunk
    first_keys = col < chunk
    m_intra = jnp.logical_and(col <= row, jnp.logical_not(jnp.logical_xor(second, col >= chunk)))
    m_cross = jnp.logical_and(second, first_keys)
    for h in range(heads):
        ks = slice(h * dk, (h + 1) * dk)
        vs = slice(h * dv, (h + 1) * dv)
        qe = qe_ref[:, ks]
        v = v_ref[:, vs]
        att = jnp.where(m_intra, _dot(qe, ket_ref[ks, :]),
                        jnp.where(m_cross, _dot(qe, klt_ref[ks, :]), 0.0))
        state = s_ref[h]
        o = _dot(att.astype(BF16), v) + _dot(qes_ref[:, ks], state.astype(BF16))
        dec = dect_ref[ks, :]
        dec = jnp.concatenate([dec] * (dv // pair), axis=1) if dv > pair else dec[:, :dv]
        s_ref[h] = dec * state + _dot(ktt_ref[ks, :], v)
        on = o * _rms_inv(o) * hg_ref[...]
        r = r_ref[:, vs].astype(F32)
        o_ref[:, vs] = (on * (r * _sigmoid(r))).astype(BF16)


def _gla_core(qe, qes, ket, klt, ktt, dect, vr, head_g, *, batch):
    t, hk = qe.shape
    hv = vr.shape[1] // 2
    pair = 2 * GLA_CHUNK
    npair = t // batch // pair
    dk = hk // GLA_HEADS
    dv = hv // GLA_HEADS
    row_k = pl.BlockSpec((pair, hk), lambda b, p: (b * npair + p, 0))
    col_k = pl.BlockSpec((hk, pair), lambda b, p: (0, b * npair + p))
    return pl.pallas_call(
        functools.partial(_gla_core_kernel, heads=GLA_HEADS, chunk=GLA_CHUNK),
        out_shape=jax.ShapeDtypeStruct((t, hv), BF16),
        grid=(batch, npair),
        in_specs=[
            row_k, row_k, col_k, col_k, col_k, col_k,
            pl.BlockSpec((pair, hv), lambda b, p: (b * npair + p, 0)),
            pl.BlockSpec((pair, hv), lambda b, p: (b * npair + p, 1)),
            pl.BlockSpec((1, dv), lambda b, p: (0, 0)),
        ],
        out_specs=pl.BlockSpec((pair, hv), lambda b, p: (b * npair + p, 0)),
        scratch_shapes=[pltpu.VMEM((GLA_HEADS, dk, dv), F32)],
        compiler_params=_params("parallel", "arbitrary"),
        name="gla_core",
    )(qe, qes, ket, klt, ktt, dect, vr, vr, head_g.reshape(1, dv))


def _gla_layer(h, batch, norm_g, w_in, w_gate_up, b_gate, head_g, w_out):
    d = h.shape[1]
    hk = d // 2
    qe, qes, ket, klt, ktt, dect = _gla_prep(h, norm_g, w_in, w_gate_up, b_gate)
    vr = _norm_matmul(h, norm_g, w_in[:, 2 * hk:2 * hk + 2 * d], rows=PROJ_ROWS, cols=d,
                      out_dtype=BF16)
    og = _gla_core(qe, qes, ket, klt, ktt, dect, vr, head_g, batch=batch)
    return _matmul_residual(og, w_out.astype(BF16), h, rows=PROJ_ROWS)


META_E, META_W, META_RANK = 0, 2, 4


def _router_kernel(h_ref, g_ref, w_ref, b_ref, tril_ref, meta_ref, cnt_ref, base_ref,
                   *, groups, per_group):
    @pl.when(pl.program_id(0) == 0)
    def _():
        base_ref[...] = jnp.zeros_like(base_ref)

    x = h_ref[...]
    xn = x * _rms_inv(x) * g_ref[...]
    logits = _dot3(xn, w_ref[...]) + b_ref[...]
    lane = lax.broadcasted_iota(jnp.int32, logits.shape, 1).astype(F32)
    neg = -jnp.inf
    far = float(LANES)

    def first_max(vals):
        m = jnp.max(vals, axis=-1, keepdims=True)
        return m, jnp.min(jnp.where(vals == m, lane, far), axis=-1, keepdims=True)

    gl = jnp.where(lane < groups, logits, neg)
    gmax, gidx = first_max(gl)
    p_group = 1.0 / jnp.sum(jnp.exp(gl - gmax), axis=-1, keepdims=True)
    lo = groups + per_group * gidx
    el = jnp.where(jnp.logical_and(lane >= lo, lane < lo + per_group), logits, neg)
    v1, i1 = first_max(el)
    v2, i2 = first_max(jnp.where(lane == i1, neg, el))
    t = jnp.exp(v2 - v1)
    w1 = p_group / (1.0 + t)
    w2 = p_group * t / (1.0 + t)

    oh1 = lane == i1
    oh2 = lane == i2
    onehot = jnp.where(jnp.logical_or(oh1, oh2), 1.0, 0.0).astype(BF16)
    seen = base_ref[...] + _dot(tril_ref[...], onehot)
    rank1 = jnp.sum(jnp.where(oh1, seen, 0.0), axis=-1, keepdims=True) - 1.0
    rank2 = jnp.sum(jnp.where(oh2, seen, 0.0), axis=-1, keepdims=True) - 1.0
    base_ref[...] = seen[-1:, :]
    cnt_ref[...] = jnp.broadcast_to(seen[-1:, :], cnt_ref.shape)

    rec = jnp.zeros_like(logits)
    for k, val in ((META_E, i1 - groups), (META_E + 1, i2 - groups), (META_W, w1),
                   (META_W + 1, w2), (META_RANK, rank1), (META_RANK + 1, rank2)):
        rec = jnp.where(lane == k, val, rec)
    meta_ref[...] = rec


def _moe_router(h, norm_g, w_group, b_group, w_router, b_router):
    t, d = h.shape
    rows = ROUTER_ROWS
    groups = w_group.shape[1]
    ne = w_router.shape[1]
    pad = LANES - groups - ne
    w = jnp.pad(jnp.concatenate([w_group, w_router], axis=1), ((0, 0), (0, pad)))
    b = jnp.pad(jnp.concatenate([b_group, b_router]), (0, pad)).reshape(1, LANES)
    tril = jnp.asarray(np.tril(np.ones((rows, rows), np.float32)), BF16)
    meta, cnt = pl.pallas_call(
        functools.partial(_router_kernel, groups=groups, per_group=ne // groups),
        out_shape=(jax.ShapeDtypeStruct((t, LANES), F32), jax.ShapeDtypeStruct((8, LANES), F32)),
        grid=(t // rows,),
        in_specs=[
            pl.BlockSpec((rows, d), lambda i: (i, 0)),
            pl.BlockSpec((1, d), lambda i: (0, 0)),
            pl.BlockSpec((d, LANES), lambda i: (0, 0)),
            pl.BlockSpec((1, LANES), lambda i: (0, 0)),
            pl.BlockSpec((rows, rows), lambda i: (0, 0)),
        ],
        out_specs=(pl.BlockSpec((rows, LANES), lambda i: (i, 0)),
                   pl.BlockSpec((8, LANES), lambda i: (0, 0))),
        scratch_shapes=[pltpu.VMEM((1, LANES), F32)],
        compiler_params=_params("arbitrary"),
        name="moe_router",
    )(h, norm_g.reshape(1, d), w, b, tril)
    return meta, cnt[0, groups:groups + ne]


def _inverse_kernel(pos0_ref, pos1_ref, pad_ref, src_ref):
    step = pl.program_id(0)
    tokens = pos0_ref.shape[0] // INVERSE_STEPS

    @pl.when(step == 0)
    def _():
        def clear(i, carry):
            src_ref[i] = 0
            return carry

        for e in range(pad_ref.shape[0] // 2):
            lax.fori_loop(pad_ref[2 * e], pad_ref[2 * e + 1], clear, 0)

    @pl.when(step > 0)
    def _():
        def place(i, carry):
            for u in range(DMA_UNROLL):
                tok = (step - 1) * tokens + i * DMA_UNROLL + u
                src_ref[pos0_ref[tok]] = tok
                src_ref[pos1_ref[tok]] = tok
            return carry

        lax.fori_loop(0, tokens // DMA_UNROLL, place, 0)


def _moe_inverse(pos0, pos1, pad_ranges, slots):
    smem = pl.BlockSpec(memory_space=pltpu.SMEM)
    return pl.pallas_call(
        _inverse_kernel,
        out_shape=jax.ShapeDtypeStruct((slots,), jnp.int32),
        grid=(INVERSE_STEPS + 1,),
        in_specs=[smem, smem, smem],
        out_specs=smem,
        compiler_params=_params("arbitrary"),
        name="moe_inverse",
    )(pos0, pos1, pad_ranges)


def _row_copy(src_hbm, row, dst, r_tile, r_sub, sem):
    return pltpu.make_async_copy(src_hbm.at[pl.ds(row, 1)], dst.at[r_tile, pl.ds(r_sub, 1)], sem)


def _rows_wait(dst, sem):
    pltpu.make_async_copy(dst, dst, sem).wait()


def _experts_kernel(src_ref, texp_ref, nused_ref, h_hbm, g_ref, wg_hbm, wu_hbm, wd_hbm, y_ref,
                    xbuf, xn_ref, wgs, wus, wds, wgb, wub, wdb, sem, wsem, *, rows, layer):
    i = pl.program_id(0)
    nused = nused_ref[0]
    expert = texp_ref[i]

    def start(tile, slot):
        base = tile * rows
        for r in range(rows):
            _row_copy(h_hbm, src_ref[base + r], xbuf.at[slot], r // SUBLANES, r % SUBLANES,
                      sem.at[slot]).start()

    def weight_copies(e):
        return (pltpu.make_async_copy(wg_hbm.at[layer, e], wgs, wsem.at[0]),
                pltpu.make_async_copy(wu_hbm.at[layer, e], wus, wsem.at[1]),
                pltpu.make_async_copy(wd_hbm.at[layer, e], wds, wsem.at[2]))

    last = nused - 1

    @pl.when(i == 0)
    def _():
        for c in weight_copies(expert):
            c.start()
        for j in range(GATHER_DEPTH - 1):
            start(jnp.minimum(j, last), j)

    fresh = jnp.logical_or(i == 0, expert != texp_ref[jnp.maximum(i - 1, 0)])

    @pl.when(jnp.logical_and(fresh, i < nused))
    def _():
        for c in weight_copies(expert):
            c.wait()
        wgb[...] = (wgs[...] * g_ref[...]).astype(BF16)
        wub[...] = (wus[...] * g_ref[...]).astype(BF16)
        wdb[...] = wds[...].astype(BF16)

    nxt = texp_ref[jnp.minimum(i + 1, pl.num_programs(0) - 1)]

    @pl.when(jnp.logical_and(i + 1 < nused, nxt != expert))
    def _():
        for c in weight_copies(nxt):
            c.start()

    @pl.when(i < nused)
    def _():
        slot = i % GATHER_DEPTH
        _rows_wait(xbuf.at[slot], sem.at[slot])
        x = xbuf[slot].reshape(xn_ref.shape)
        xn_ref[...] = x.astype(BF16)
        inv = _rms_inv(x)
        ahead = i + GATHER_DEPTH - 1
        start(jnp.minimum(ahead, last), ahead % GATHER_DEPTH)
        xn = xn_ref[...]
        gate = _dot(xn, wgb[...]) * inv
        up = _dot(xn, wub[...]) * inv
        act = (gate * _sigmoid(gate)) * up
        y_ref[...] = _dot(act.astype(BF16), wdb[...])

    @pl.when(i == last)
    def _():
        for j in range(1, GATHER_DEPTH):
            slot = (i + j) % GATHER_DEPTH
            _rows_wait(xbuf.at[slot], sem.at[slot])

    @pl.when(i >= nused)
    def _():
        y_ref[...] = jnp.zeros_like(y_ref)


def _moe_experts(h, norm_g, layer, w_gate, w_up, w_down, src, tile_expert, n_used):
    t, d = h.shape
    _, ne, _, f = w_gate.shape
    rows = EXPERT_ROWS
    slots = src.shape[0]
    hbm = pl.BlockSpec(memory_space=pl.ANY)
    return pl.pallas_call(
        functools.partial(_experts_kernel, rows=rows, layer=layer),
        out_shape=jax.ShapeDtypeStruct((slots, d), F32),
        grid_spec=pltpu.PrefetchScalarGridSpec(
            num_scalar_prefetch=3,
            grid=(slots // rows,),
            in_specs=[hbm, pl.BlockSpec((d, 1), lambda i, *_: (0, 0)), hbm, hbm, hbm],
            out_specs=pl.BlockSpec((rows, d), lambda i, *_: (i, 0)),
            scratch_shapes=[pltpu.VMEM((GATHER_DEPTH, rows // SUBLANES, SUBLANES, d), F32),
                            pltpu.VMEM((rows, d), BF16),
                            pltpu.VMEM((d, f), F32), pltpu.VMEM((d, f), F32),
                            pltpu.VMEM((f, d), F32),
                            pltpu.VMEM((d, f), BF16), pltpu.VMEM((d, f), BF16),
                            pltpu.VMEM((f, d), BF16),
                            pltpu.SemaphoreType.DMA((GATHER_DEPTH,)),
                            pltpu.SemaphoreType.DMA((3,))],
        ),
        compiler_params=_params("arbitrary"),
        name="moe_experts",
    )(src, tile_expert, n_used, h, norm_g.reshape(d, 1), w_gate, w_up, w_down)


def _combine_kernel(pos0_ref, pos1_ref, h_ref, meta_ref, y_hbm, o_ref, buf0, buf1, sem, *, rows):
    i = pl.program_id(0)

    def start(tile):
        slot = tile % 2

        def body(g, carry):
            for u in range(SUBLANES):
                tok = tile * rows + g * SUBLANES + u
                _row_copy(y_hbm, pos0_ref[tok], buf0.at[slot], g, u, sem.at[0, slot]).start()
                _row_copy(y_hbm, pos1_ref[tok], buf1.at[slot], g, u, sem.at[1, slot]).start()
            return carry

        lax.fori_loop(0, rows // SUBLANES, body, 0)

    @pl.when(i == 0)
    def _():
        start(0)

    @pl.when(i + 1 < pl.num_programs(0))
    def _():
        start(i + 1)

    slot = i % 2
    _rows_wait(buf0.at[slot], sem.at[0, slot])
    _rows_wait(buf1.at[slot], sem.at[1, slot])
    meta = meta_ref[...]
    w0 = meta[:, META_W:META_W + 1]
    w1 = meta[:, META_W + 1:META_W + 2]
    shape = h_ref.shape
    o_ref[...] = h_ref[...] + w0 * buf0[slot].reshape(shape) + w1 * buf1[slot].reshape(shape)


def _moe_combine(h, meta, y, pos0, pos1):
    t, d = h.shape
    rows = COMBINE_ROWS
    gather_buf = pltpu.VMEM((2, rows // SUBLANES, SUBLANES, d), F32)
    return pl.pallas_call(
        functools.partial(_combine_kernel, rows=rows),
        out_shape=jax.ShapeDtypeStruct((t, d), F32),
        grid_spec=pltpu.PrefetchScalarGridSpec(
            num_scalar_prefetch=2,
            grid=(t // rows,),
            in_specs=[
                pl.BlockSpec((rows, d), lambda i, *_: (i, 0)),
                pl.BlockSpec((rows, LANES), lambda i, *_: (i, 0)),
                pl.BlockSpec(memory_space=pl.ANY),
            ],
            out_specs=pl.BlockSpec((rows, d), lambda i, *_: (i, 0)),
            scratch_shapes=[gather_buf, gather_buf, pltpu.SemaphoreType.DMA((2, 2))],
        ),
        compiler_params=_params("arbitrary"),
        name="moe_combine",
    )(pos0, pos1, h, meta, y)


def _moe_layer(h, layer, norm_g, w_group, b_group, w_router, b_router, w_gate, w_up, w_down):
    t, _ = h.shape
    ne = w_router.shape[1]
    rows = EXPERT_ROWS
    meta, counts = _moe_router(h, norm_g, w_group, b_group, w_router, b_router)
    counts = counts.astype(jnp.int32)
    padded = (counts + rows - 1) // rows * rows
    ends = jnp.cumsum(padded)
    starts = ends - padded
    rec = meta[:, :SUBLANES].T
    expert = rec[META_E:META_E + 2].astype(jnp.int32)
    rank = rec[META_RANK:META_RANK + 2].astype(jnp.int32)
    mine = expert[None] == jnp.arange(ne, dtype=jnp.int32)[:, None, None]
    pos = rank + jnp.sum(jnp.where(mine, starts[:, None, None], 0), axis=0)
    slots = 2 * t + ne * rows
    n_tiles = slots // rows
    n_used = (ends[-1] // rows).astype(jnp.int32)
    tile_start = jnp.arange(n_tiles, dtype=jnp.int32) * rows
    tile_start = jnp.minimum(tile_start, ends[-1] - 1)
    tile_expert = jnp.sum(tile_start[:, None] >= ends[None, :], axis=1, dtype=jnp.int32)
    tile_expert = jnp.minimum(tile_expert, ne - 1)
    pos0 = pos[0]
    pos1 = pos[1]
    pad_ranges = jnp.stack([jnp.append(starts + counts, ends[-1]),
                            jnp.append(ends, slots)], axis=1).reshape(-1).astype(jnp.int32)
    src = _moe_inverse(pos0, pos1, pad_ranges, slots)
    y = _moe_experts(h, norm_g, layer, w_gate, w_up, w_down, src, tile_expert, n_used.reshape(1))
    return _moe_combine(h, meta, y, pos0, pos1)


def _swa_proj_kernel(x_ref, gq_ref, gkv_ref, wq_ref, wkv_ref, kg_ref, q_ref, k_ref, v_ref):
    x = x_ref[...]
    xs = x * _rms_inv(x)
    q_ref[...] = _dot((xs * gq_ref[...]).astype(BF16), wq_ref[...]).astype(BF16)
    kv = _dot((xs * gkv_ref[...]).astype(BF16), wkv_ref[...])
    half = kv.shape[1] // 2
    v_ref[...] = kv[:, half:].astype(BF16)
    for hd in range(half // LANES):
        sl = slice(hd * LANES, (hd + 1) * LANES)
        k = kv[:, sl]
        k_ref[:, sl] = (k * _rms_inv(k) * kg_ref[...]).astype(BF16)


def _swa_proj(h, q_norm_g, kv_norm_g, w_q, w_kv, k_norm):
    t, d = h.shape
    hd = SWA_HEAD_DIM
    kvh = w_kv.shape[1] // (2 * hd)
    rows = SWA_PROJ_ROWS
    rep = LANES // hd
    w_dup = jnp.broadcast_to(w_kv.astype(BF16).reshape(d, 2 * kvh, 1, hd), (d, 2 * kvh, rep, hd))
    w_dup = w_dup.reshape(d, 2 * kvh * LANES)
    kg = jnp.tile(k_norm, rep).reshape(1, LANES)
    const = lambda shape: pl.BlockSpec(shape, lambda i: (0, 0))
    return pl.pallas_call(
        _swa_proj_kernel,
        out_shape=(
            jax.ShapeDtypeStruct((t, w_q.shape[1]), BF16),
            jax.ShapeDtypeStruct((t, kvh * LANES), BF16),
            jax.ShapeDtypeStruct((t, kvh * LANES), BF16),
        ),
        grid=(t // rows,),
        in_specs=[
            pl.BlockSpec((rows, d), lambda i: (i, 0)),
            const((1, d)), const((1, d)),
            const((d, w_q.shape[1])), const((d, 2 * kvh * LANES)), const((1, LANES)),
        ],
        out_specs=(
            pl.BlockSpec((rows, w_q.shape[1]), lambda i: (i, 0)),
            pl.BlockSpec((rows, kvh * LANES), lambda i: (i, 0)),
            pl.BlockSpec((rows, kvh * LANES), lambda i: (i, 0)),
        ),
        compiler_params=_params("parallel"),
        name="swa_proj",
    )(h, q_norm_g.reshape(1, d), kv_norm_g.reshape(1, d), w_q.astype(BF16), w_dup, kg)


def _swa_attn_kernel(q_ref, kp_ref, kc_ref, vp_ref, vc_ref, bias_ref, sinkw_ref, qg_ref,
                     blk_ref, ones_ref, o_ref, *, kvh, pairs, window):
    hd = LANES // 2
    lane = lax.broadcasted_iota(jnp.int32, (window, LANES), 1)
    low = lane < hd
    width = 2 * window
    olow = lax.broadcasted_iota(jnp.int32, (pairs * window, LANES), 1) < hd
    zero = jnp.zeros((), BF16)

    def block_diag(prev, cur):
        return jnp.concatenate([jnp.where(low, prev, zero), jnp.where(low, cur, zero),
                                jnp.where(low, zero, prev), jnp.where(low, zero, cur)], axis=0)

    for h in range(kvh):
        ksl = slice(h * LANES, (h + 1) * LANES)
        kk = block_diag(kp_ref[:, ksl], kc_ref[:, ksl])
        vv = block_diag(vp_ref[:, ksl], vc_ref[:, ksl])
        base = h * pairs * LANES
        q2 = jnp.concatenate([q_ref[:, base + p * LANES:base + (p + 1) * LANES]
                              for p in range(pairs)], axis=0).astype(F32)
        ms = _dot((q2 * q2).astype(BF16), blk_ref[...])
        qn = (q2 * lax.rsqrt(ms + RMS_EPS) * qg_ref[...]).astype(BF16)
        s = _dot_nt(qn, kk) + bias_ref[0, h]
        probs = []
        row_max = []
        for half in range(2):
            sh = s[:, half * width:(half + 1) * width]
            m = jnp.max(sh, axis=-1, keepdims=True)
            probs.append(jnp.exp2(sh - m).astype(BF16))
            row_max.append(m)
        mixed = _dot(jnp.concatenate(probs, axis=1), jnp.concatenate([vv, ones_ref[...]], axis=1))
        den = mixed[:, LANES:] + jnp.exp2(sinkw_ref[h] - jnp.where(olow, row_max[0], row_max[1]))
        o2 = mixed[:, :LANES] * (1.0 / den)
        for p in range(pairs):
            o_ref[:, base + p * LANES:base + (p + 1) * LANES] = (
                o2[p * window:(p + 1) * window].astype(BF16))


def _t5_bucket(dist):
    max_exact = NUM_BUCKETS // 2
    n = np.maximum(dist, 0)
    large = max_exact + (np.log(np.maximum(n, max_exact) / max_exact)
                         / math.log(REL_MAX_DISTANCE / max_exact)
                         * (NUM_BUCKETS - max_exact)).astype(np.int32)
    return np.where(n < max_exact, n, np.minimum(large, NUM_BUCKETS - 1)).astype(np.int32)


def _swa_attn(q, kd, vd, rel_bias, q_norm, sinks, *, batch):
    t, dq = q.shape
    kvh = kd.shape[1] // LANES
    hq = dq // SWA_HEAD_DIM
    pairs = hq // kvh // 2
    w = WINDOW
    nblk = t // batch // w
    row = np.arange(w)[:, None]
    col = np.arange(2 * w)[None, :]
    dist = w + row - col
    band = (dist >= 0) & (dist < w)
    onehot = (_t5_bucket(dist)[..., None] == np.arange(NUM_BUCKETS)).astype(np.float32)
    bias = jnp.dot(jnp.asarray(onehot), rel_bias.astype(F32), precision=lax.Precision.HIGHEST)
    bias = jnp.where(jnp.asarray(band)[..., None], bias, -jnp.inf)
    bias = bias.transpose(2, 0, 1).reshape(kvh, pairs, 2, w, 2 * w).transpose(0, 1, 3, 2, 4)
    bias = bias.reshape(kvh, pairs * w, 4 * w) * LOG2_E
    prev_cols = jnp.asarray((np.arange(4 * w) // w) % 2 == 0)
    bias = jnp.stack([bias, jnp.where(prev_cols, -jnp.inf, bias)])
    sink = sinks.astype(F32).reshape(kvh, pairs, 1, 2, 1) * LOG2_E
    sink_wide = jnp.broadcast_to(sink, (kvh, pairs, w, 2, SWA_HEAD_DIM))
    sink_wide = sink_wide.reshape(kvh, pairs * w, LANES)
    qg = jnp.tile(q_norm, LANES // SWA_HEAD_DIM) * (SWA_HEAD_DIM ** -0.5 * LOG2_E)
    lane = np.arange(LANES)
    blk = ((lane[:, None] // SWA_HEAD_DIM) == (lane[None, :] // SWA_HEAD_DIM)) / SWA_HEAD_DIM
    ones = (np.arange(4 * w)[:, None] // (2 * w)) == (lane[None, :] // SWA_HEAD_DIM)
    cur = lambda b, n: (b * nblk + n, 0)
    prev = lambda b, n: (b * nblk + jnp.maximum(n - 1, 0), 0)
    const2 = lambda shape: pl.BlockSpec(shape, lambda b, n: (0, 0))
    const3 = lambda shape: pl.BlockSpec(shape, lambda b, n: (0, 0, 0))
    return pl.pallas_call(
        functools.partial(_swa_attn_kernel, kvh=kvh, pairs=pairs, window=w),
        out_shape=jax.ShapeDtypeStruct((t, dq), BF16),
        grid=(batch, nblk),
        in_specs=[
            pl.BlockSpec((w, dq), cur),
            pl.BlockSpec((w, kvh * LANES), prev), pl.BlockSpec((w, kvh * LANES), cur),
            pl.BlockSpec((w, kvh * LANES), prev), pl.BlockSpec((w, kvh * LANES), cur),
            pl.BlockSpec((1, kvh, pairs * w, 4 * w), lambda b, n: (jnp.where(n == 0, 1, 0), 0, 0, 0)),
            const3((kvh, pairs * w, LANES)),
            const2((1, LANES)), const2((LANES, LANES)), const2((4 * w, LANES)),
        ],
        out_specs=pl.BlockSpec((w, dq), cur),
        compiler_params=_params("parallel", "parallel"),
        name="swa_attn",
    )(q, kd, kd, vd, vd, bias, sink_wide, qg.reshape(1, LANES), jnp.asarray(blk, BF16),
      jnp.asarray(ones, BF16))


def _swa_layer(h, batch, kv_norm, w_kv, k_norm, rel_bias, norm_g, w_q, q_norm, sinks, w_out):
    q, kd, vd = _swa_proj(h, norm_g, kv_norm, w_q, w_kv, k_norm)
    o = _swa_attn(q, kd, vd, rel_bias, q_norm, sinks, batch=batch)
    return _matmul_residual(o, w_out.astype(BF16), h, rows=PROJ_ROWS)


def kernel(x, gla_norm, gla_w_in, gla_w_gate_up, gla_b_gate, gla_head_norm, gla_w_out, kv_norm, w_kv, k_norm, rel_bias, swa_norm, swa_w_q, swa_q_norm, swa_sinks, swa_w_out, moe_norm, moe_w_group, moe_b_group, moe_w_router, moe_b_router, moe_w_gate, moe_w_up, moe_w_down):
    batch, seq, d = x.shape
    assert gla_norm.shape[0] == 1 and swa_norm.shape[0] == 1 and moe_norm.shape[0] == 2
    h = x.reshape(batch * seq, d)
    h = _gla_layer(h, batch, gla_norm[0], gla_w_in[0], gla_w_gate_up[0], gla_b_gate[0],
                   gla_head_norm[0], gla_w_out[0])
    h = _moe_layer(h, 0, moe_norm[0], moe_w_group[0], moe_b_group[0], moe_w_router[0],
                   moe_b_router[0], moe_w_gate, moe_w_up, moe_w_down)
    h = _swa_layer(h, batch, kv_norm, w_kv, k_norm, rel_bias, swa_norm[0], swa_w_q[0],
                   swa_q_norm[0], swa_sinks[0], swa_w_out[0])
    h = _moe_layer(h, 1, moe_norm[1], moe_w_group[1], moe_b_group[1], moe_w_router[1],
                   moe_b_router[1], moe_w_gate, moe_w_up, moe_w_down)
    return h.reshape(batch, seq, d)
```

```python
import functools
import math

import jax
import jax.numpy as jnp
import numpy as np
from jax import lax
from jax.experimental import pallas as pl
from jax.experimental.pallas import tpu as pltpu

F32 = jnp.float32
BF16 = jnp.bfloat16

RMS_EPS = 1e-6
GLA_HEADS = 4
GLA_GATE_RANK = 16
GLA_GATE_NORMALIZER = 16.0
GLA_LOG_GATE_MIN = -1.0
GLA_CHUNK = 64
SWA_HEAD_DIM = 64
SWA_GROUPS = 8
WINDOW = 128
NUM_BUCKETS = 32
REL_MAX_DISTANCE = 128
MOE_GROUPS = 4
MOE_EXPERTS_PER_GROUP = 8

LANES = 128
SUBLANES = 8
VMEM_LIMIT_BYTES = 56 * 1024 * 1024

PREP_ROWS = 256
PROJ_ROWS = 512
ROUTER_ROWS = 512
EXPERT_ROWS = 512
COMBINE_ROWS = 256
SWA_PROJ_ROWS = 256
LOG2_E = math.log2(math.e)
DMA_UNROLL = 8
GATHER_DEPTH = 3
INVERSE_STEPS = 64


def _params(*semantics):
    return pltpu.CompilerParams(dimension_semantics=semantics, vmem_limit_bytes=VMEM_LIMIT_BYTES)


def _dot(a, b):
    return jnp.dot(a, b, preferred_element_type=F32)


def _dot_nt(a, b):
    return lax.dot_general(a, b, (((1,), (1,)), ((), ())), preferred_element_type=F32)


def _split(x):
    hi = x.astype(BF16)
    lo = (x - hi.astype(F32)).astype(BF16)
    return hi, lo


def _dot3(a, b, dot=_dot):
    ah, al = _split(a)
    bh, bl = _split(b)
    return dot(ah, bh) + dot(al, bh) + dot(ah, bl)


def _rms_inv(x):
    return lax.rsqrt(jnp.mean(x * x, axis=-1, keepdims=True) + RMS_EPS)


def _sigmoid(x):
    return 1.0 / (1.0 + jnp.exp(-x))


def _log_sigmoid(x):
    return jnp.minimum(x, 0.0) - jnp.log1p(jnp.exp(-jnp.abs(x)))


def _norm_matmul_kernel(x_ref, w_ref, o_ref, xb_ref, inv_ref):
    @pl.when(pl.program_id(1) == 0)
    def _():
        x = x_ref[...]
        xb_ref[...] = x.astype(BF16)
        inv_ref[...] = _rms_inv(x)

    o_ref[...] = (_dot(xb_ref[...], w_ref[...]) * inv_ref[...]).astype(o_ref.dtype)


def _norm_matmul(x, g, w, *, rows, cols, out_dtype):
    t, d = x.shape
    n = w.shape[1]
    return pl.pallas_call(
        _norm_matmul_kernel,
        out_shape=jax.ShapeDtypeStruct((t, n), out_dtype),
        grid=(t // rows, n // cols),
        in_specs=[
            pl.BlockSpec((rows, d), lambda i, j: (i, 0)),
            pl.BlockSpec((d, cols), lambda i, j: (0, j)),
        ],
        out_specs=pl.BlockSpec((rows, cols), lambda i, j: (i, j)),
        scratch_shapes=[pltpu.VMEM((rows, d), BF16), pltpu.VMEM((rows, 1), F32)],
        compiler_params=_params("parallel", "arbitrary"),
        name="norm_matmul",
    )(x, (g[:, None] * w).astype(BF16))


def _matmul_residual_kernel(a_ref, w_ref, res_ref, o_ref):
    o_ref[...] = res_ref[...] + _dot(a_ref[...], w_ref[...])


def _matmul_residual(a, w, res, *, rows):
    t, k = a.shape
    n = w.shape[1]
    return pl.pallas_call(
        _matmul_residual_kernel,
        out_shape=jax.ShapeDtypeStruct((t, n), F32),
        grid=(t // rows,),
        in_specs=[
            pl.BlockSpec((rows, k), lambda i: (i, 0)),
            pl.BlockSpec((k, n), lambda i: (0, 0)),
            pl.BlockSpec((rows, n), lambda i: (i, 0)),
        ],
        out_specs=pl.BlockSpec((rows, n), lambda i: (i, 0)),
        compiler_params=_params("parallel"),
        name="matmul_residual",
    )(a, w, res)


def _pair_constants(chunk):
    pair = 2 * chunk
    i = np.arange(pair)[:, None]
    j = np.arange(pair)[None, :]
    tri = (j <= i).astype(np.float32)
    sel = ((i >= chunk) & (j < chunk)).astype(np.float32)
    blk = ((i // chunk) == (j // chunk)).astype(np.float32)
    ones = np.ones((pair, pair), np.float32)
    c_row = np.concatenate([tri, sel], axis=0)
    c_col = np.concatenate([tri.T, sel.T, blk, ones], axis=1)
    return jnp.asarray(c_row, BF16), jnp.asarray(c_col, BF16)


def _gla_prep_kernel(x_ref, g_ref, wq_ref, wkt_ref, wg_ref, wgu_ref, bgr_ref,
                     crow_ref, ccol_ref, qe_ref, qes_ref, ket_ref, klt_ref, ktt_ref, dect_ref,
                     *, scale, pair):
    x = x_ref[...]
    xn = (x * _rms_inv(x) * g_ref[...]).astype(BF16)
    q = _dot(xn, wq_ref[...]) * scale
    kt = _dot_nt(wkt_ref[...], xn)
    glr = _dot(xn, wg_ref[...]).astype(BF16)
    z = _dot(glr, wgu_ref[...]) + bgr_ref[...]
    la = jnp.maximum(_log_sigmoid(z) / GLA_GATE_NORMALIZER, GLA_LOG_GATE_MIN)
    lat = la.T
    crow = crow_ref[...]
    ccol = ccol_ref[...]
    for p in range(x.shape[0] // pair):
        rows = slice(p * pair, (p + 1) * pair)
        hi, lo = _split(la[rows])
        cr = _dot(crow, hi) + _dot(crow, lo)
        b_pair = cr[:pair]
        b_chunk = b_pair - cr[pair:]
        qp = q[rows]
        qe_ref[rows, :] = (qp * jnp.exp(b_chunk)).astype(BF16)
        qes_ref[rows, :] = (qp * jnp.exp(b_pair)).astype(BF16)
        hi, lo = _split(lat[:, rows])
        cc = _dot(hi, ccol) + _dot(lo, ccol)
        bt_pair = cc[:, :pair]
        bt_chunk = bt_pair - cc[:, pair:2 * pair]
        end_chunk = cc[:, 2 * pair:3 * pair]
        end_pair = cc[:, 3 * pair:]
        ktp = kt[:, rows]
        ket_ref[:, rows] = (ktp * jnp.exp(-bt_chunk)).astype(BF16)
        klt_ref[:, rows] = (ktp * jnp.exp(end_chunk - bt_chunk)).astype(BF16)
        ktt_ref[:, rows] = (ktp * jnp.exp(end_pair - bt_pair)).astype(BF16)
        dect_ref[:, rows] = jnp.exp(end_pair)


def _gla_prep(h, norm_g, w_in, w_gate_up, b_gate):
    t, d = h.shape
    hk = d // 2
    rank = w_gate_up.shape[0]
    pair = 2 * GLA_CHUNK
    rows = PREP_ROWS
    dk = hk // GLA_HEADS
    wq = w_in[:, :hk].astype(BF16)
    wkt = w_in[:, hk:2 * hk].T.astype(BF16)
    wg = jnp.pad(w_in[:, 3 * d:], ((0, 0), (0, LANES - rank))).astype(BF16)
    wgu = jnp.pad(w_gate_up, ((0, LANES - rank), (0, 0))).astype(BF16)
    c_row, c_col = _pair_constants(GLA_CHUNK)
    const = lambda shape: pl.BlockSpec(shape, lambda i: (0, 0))
    row_out = pl.BlockSpec((rows, hk), lambda i: (i, 0))
    col_out = pl.BlockSpec((hk, rows), lambda i: (0, i))
    return pl.pallas_call(
        functools.partial(_gla_prep_kernel, scale=dk ** -0.5, pair=pair),
        out_shape=(
            jax.ShapeDtypeStruct((t, hk), BF16),
            jax.ShapeDtypeStruct((t, hk), BF16),
            jax.ShapeDtypeStruct((hk, t), BF16),
            jax.ShapeDtypeStruct((hk, t), BF16),
            jax.ShapeDtypeStruct((hk, t), BF16),
            jax.ShapeDtypeStruct((hk, t), F32),
        ),
        grid=(t // rows,),
        in_specs=[
            pl.BlockSpec((rows, d), lambda i: (i, 0)),
            const((1, d)),
            const((d, hk)),
            const((hk, d)),
            const((d, LANES)),
            const((LANES, hk)),
            const((1, hk)),
            const((2 * pair, pair)),
            const((pair, 4 * pair)),
        ],
        out_specs=(row_out, row_out, col_out, col_out, col_out, col_out),
        compiler_params=_params("parallel"),
        name="gla_prep",
    )(h, norm_g.reshape(1, d), wq, wkt, wg, wgu, b_gate.reshape(1, hk), c_row, c_col)


def _gla_core_kernel(qe_ref, qes_ref, ket_ref, klt_ref, ktt_ref, dect_ref, v_ref, r_ref, hg_ref,
                     o_ref, s_ref, *, heads, chunk):
    @pl.when(pl.program_id(1) == 0)
    def _():
        s_ref[...] = jnp.zeros_like(s_ref)

    pair = 2 * chunk
    dk = qe_ref.shape[1] // heads
    dv = v_ref.shape[1] // heads
    row = lax.broadcasted_iota(jnp.int32, (pair, pair), 0)
    col = lax.broadcasted_iota(jnp.int32, (pair, pair), 1)
    second = row >= chunk
    first_keys = col < chunk
    m_intra = jnp.logical_and(col <= row, jnp.logical_not(jnp.logical_xor(second, col >= chunk)))
    m_cross = jnp.logical_and(second, first_keys)
    for h in range(heads):
        ks = slice(h * dk, (h + 1) * dk)
        vs = slice(h * dv, (h + 1) * dv)
        qe = qe_ref[:, ks]
        v = v_ref[:, vs]
        att = jnp.where(m_intra, _dot(qe, ket_ref[ks, :]),
                        jnp.where(m_cross, _dot(qe, klt_ref[ks, :]), 0.0))
        state = s_ref[h]
        o = _dot(att.astype(BF16), v) + _dot(qes_ref[:, ks], state.astype(BF16))
        dec = dect_ref[ks, :]
        dec = jnp.concatenate([dec] * (dv // pair), axis=1) if dv > pair else dec[:, :dv]
        s_ref[h] = dec * state + _dot(ktt_ref[ks, :], v)
        on = o * _rms_inv(o) * hg_ref[...]
        r = r_ref[:, vs].astype(F32)
        o_ref[:, vs] = (on * (r * _sigmoid(r))).astype(BF16)


def _gla_core(qe, qes, ket, klt, ktt, dect, vr, head_g, *, batch):
    t, hk = qe.shape
    hv = vr.shape[1] // 2
    pair = 2 * GLA_CHUNK
    npair = t // batch // pair
    dk = hk // GLA_HEADS
    dv = hv // GLA_HEADS
    row_k = pl.BlockSpec((pair, hk), lambda b, p: (b * npair + p, 0))
    col_k = pl.BlockSpec((hk, pair), lambda b, p: (0, b * npair + p))
    return pl.pallas_call(
        functools.partial(_gla_core_kernel, heads=GLA_HEADS, chunk=GLA_CHUNK),
        out_shape=jax.ShapeDtypeStruct((t, hv), BF16),
        grid=(batch, npair),
        in_specs=[
            row_k, row_k, col_k, col_k, col_k, col_k,
            pl.BlockSpec((pair, hv), lambda b, p: (b * npair + p, 0)),
            pl.BlockSpec((pair, hv), lambda b, p: (b * npair + p, 1)),
            pl.BlockSpec((1, dv), lambda b, p: (0, 0)),
        ],
        out_specs=pl.BlockSpec((pair, hv), lambda b, p: (b * npair + p, 0)),
        scratch_shapes=[pltpu.VMEM((GLA_HEADS, dk, dv), F32)],
        compiler_params=_params("parallel", "arbitrary"),
        name="gla_core",
    )(qe, qes, ket, klt, ktt, dect, vr, vr, head_g.reshape(1, dv))


def _gla_layer(h, batch, norm_g, w_in, w_gate_up, b_gate, head_g, w_out):
    d = h.shape[1]
    hk = d // 2
    qe, qes, ket, klt, ktt, dect = _gla_prep(h, norm_g, w_in, w_gate_up, b_gate)
    vr = _norm_matmul(h, norm_g, w_in[:, 2 * hk:2 * hk + 2 * d], rows=PROJ_ROWS, cols=d,
                      out_dtype=BF16)
    og = _gla_core(qe, qes, ket, klt, ktt, dect, vr, head_g, batch=batch)
    return _matmul_residual(og, w_out.astype(BF16), h, rows=PROJ_ROWS)


META_E, META_W, META_RANK = 0, 2, 4


def _router_kernel(h_ref, g_ref, w_ref, b_ref, tril_ref, meta_ref, cnt_ref, base_ref,
                   *, groups, per_group):
    @pl.when(pl.program_id(0) == 0)
    def _():
        base_ref[...] = jnp.zeros_like(base_ref)

    x = h_ref[...]
    xn = x * _rms_inv(x) * g_ref[...]
    logits = _dot3(xn, w_ref[...]) + b_ref[...]
    lane = lax.broadcasted_iota(jnp.int32, logits.shape, 1).astype(F32)
    neg = -jnp.inf
    far = float(LANES)

    def first_max(vals):
        m = jnp.max(vals, axis=-1, keepdims=True)
        return m, jnp.min(jnp.where(vals == m, lane, far), axis=-1, keepdims=True)

    gl = jnp.where(lane < groups, logits, neg)
    gmax, gidx = first_max(gl)
    p_group = 1.0 / jnp.sum(jnp.exp(gl - gmax), axis=-1, keepdims=True)
    lo = groups + per_group * gidx
    el = jnp.where(jnp.logical_and(lane >= lo, lane < lo + per_group), logits, neg)
    v1, i1 = first_max(el)
    v2, i2 = first_max(jnp.where(lane == i1, neg, el))
    t = jnp.exp(v2 - v1)
    w1 = p_group / (1.0 + t)
    w2 = p_group * t / (1.0 + t)

    oh1 = lane == i1
    oh2 = lane == i2
    onehot = jnp.where(jnp.logical_or(oh1, oh2), 1.0, 0.0).astype(BF16)
    seen = base_ref[...] + _dot(tril_ref[...], onehot)
    rank1 = jnp.sum(jnp.where(oh1, seen, 0.0), axis=-1, keepdims=True) - 1.0
    rank2 = jnp.sum(jnp.where(oh2, seen, 0.0), axis=-1, keepdims=True) - 1.0
    base_ref[...] = seen[-1:, :]
    cnt_ref[...] = jnp.broadcast_to(seen[-1:, :], cnt_ref.shape)

    rec = jnp.zeros_like(logits)
    for k, val in ((META_E, i1 - groups), (META_E + 1, i2 - groups), (META_W, w1),
                   (META_W + 1, w2), (META_RANK, rank1), (META_RANK + 1, rank2)):
        rec = jnp.where(lane == k, val, rec)
    meta_ref[...] = rec


def _moe_router(h, norm_g, w_group, b_group, w_router, b_router):
    t, d = h.shape
    rows = ROUTER_ROWS
    groups = w_group.shape[1]
    ne = w_router.shape[1]
    pad = LANES - groups - ne
    w = jnp.pad(jnp.concatenate([w_group, w_router], axis=1), ((0, 0), (0, pad)))
    b = jnp.pad(jnp.concatenate([b_group, b_router]), (0, pad)).reshape(1, LANES)
    tril = jnp.asarray(np.tril(np.ones((rows, rows), np.float32)), BF16)
    meta, cnt = pl.pallas_call(
        functools.partial(_router_kernel, groups=groups, per_group=ne // groups),
        out_shape=(jax.ShapeDtypeStruct((t, LANES), F32), jax.ShapeDtypeStruct((8, LANES), F32)),
        grid=(t // rows,),
        in_specs=[
            pl.BlockSpec((rows, d), lambda i: (i, 0)),
            pl.BlockSpec((1, d), lambda i: (0, 0)),
            pl.BlockSpec((d, LANES), lambda i: (0, 0)),
            pl.BlockSpec((1, LANES), lambda i: (0, 0)),
            pl.BlockSpec((rows, rows), lambda i: (0, 0)),
        ],
        out_specs=(pl.BlockSpec((rows, LANES), lambda i: (i, 0)),
                   pl.BlockSpec((8, LANES), lambda i: (0, 0))),
        scratch_shapes=[pltpu.VMEM((1, LANES), F32)],
        compiler_params=_params("arbitrary"),
        name="moe_router",
    )(h, norm_g.reshape(1, d), w, b, tril)
    return meta, cnt[0, groups:groups + ne]


def _inverse_kernel(pos0_ref, pos1_ref, pad_ref, src_ref):
    step = pl.program_id(0)
    tokens = pos0_ref.shape[0] // INVERSE_STEPS

    @pl.when(step == 0)
    def _():
        def clear(i, carry):
            src_ref[i] = 0
            return carry

        for e in range(pad_ref.shape[0] // 2):
            lax.fori_loop(pad_ref[2 * e], pad_ref[2 * e + 1], clear, 0)

    @pl.when(step > 0)
    def _():
        def place(i, carry):
            for u in range(DMA_UNROLL):
                tok = (step - 1) * tokens + i * DMA_UNROLL + u
                src_ref[pos0_ref[tok]] = tok
                src_ref[pos1_ref[tok]] = tok
            return carry

        lax.fori_loop(0, tokens // DMA_UNROLL, place, 0)


def _moe_inverse(pos0, pos1, pad_ranges, slots):
    smem = pl.BlockSpec(memory_space=pltpu.SMEM)
    return pl.pallas_call(
        _inverse_kernel,
        out_shape=jax.ShapeDtypeStruct((slots,), jnp.int32),
        grid=(INVERSE_STEPS + 1,),
        in_specs=[smem, smem, smem],
        out_specs=smem,
        compiler_params=_params("arbitrary"),
        name="moe_inverse",
    )(pos0, pos1, pad_ranges)


def _row_copy(src_hbm, row, dst, r_tile, r_sub, sem):
    return pltpu.make_async_copy(src_hbm.at[pl.ds(row, 1)], dst.at[r_tile, pl.ds(r_sub, 1)], sem)


def _rows_wait(dst, sem):
    pltpu.make_async_copy(dst, dst, sem).wait()


def _experts_kernel(src_ref, texp_ref, nused_ref, h_hbm, g_ref, wg_hbm, wu_hbm, wd_hbm, y_ref,
                    xbuf, xn_ref, wgs, wus, wds, wgb, wub, wdb, sem, wsem, *, rows, layer):
    i = pl.program_id(0)
    nused = nused_ref[0]
    expert = texp_ref[i]

    def start(tile, slot):
        base = tile * rows
        for r in range(rows):
            _row_copy(h_hbm, src_ref[base + r], xbuf.at[slot], r // SUBLANES, r % SUBLANES,
                      sem.at[slot]).start()

    def weight_copies(e):
        return (pltpu.make_async_copy(wg_hbm.at[layer, e], wgs, wsem.at[0]),
                pltpu.make_async_copy(wu_hbm.at[layer, e], wus, wsem.at[1]),
                pltpu.make_async_copy(wd_hbm.at[layer, e], wds, wsem.at[2]))

    last = nused - 1

    @pl.when(i == 0)
    def _():
        for c in weight_copies(expert):
            c.start()
        for j in range(GATHER_DEPTH - 1):
            start(jnp.minimum(j, last), j)

    fresh = jnp.logical_or(i == 0, expert != texp_ref[jnp.maximum(i - 1, 0)])

    @pl.when(jnp.logical_and(fresh, i < nused))
    def _():
        for c in weight_copies(expert):
            c.wait()
        wgb[...] = (wgs[...] * g_ref[...]).astype(BF16)
        wub[...] = (wus[...] * g_ref[...]).astype(BF16)
        wdb[...] = wds[...].astype(BF16)

    nxt = texp_ref[jnp.minimum(i + 1, pl.num_programs(0) - 1)]

    @pl.when(jnp.logical_and(i + 1 < nused, nxt != expert))
    def _():
        for c in weight_copies(nxt):
            c.start()

    @pl.when(i < nused)
    def _():
        slot = i % GATHER_DEPTH
        _rows_wait(xbuf.at[slot], sem.at[slot])
        x = xbuf[slot].reshape(xn_ref.shape)
        xn_ref[...] = x.astype(BF16)
        inv = _rms_inv(x)
        ahead = i + GATHER_DEPTH - 1
        start(jnp.minimum(ahead, last), ahead % GATHER_DEPTH)
        xn = xn_ref[...]
        gate = _dot(xn, wgb[...]) * inv
        up = _dot(xn, wub[...]) * inv
        act = (gate * _sigmoid(gate)) * up
        y_ref[...] = _dot(act.astype(BF16), wdb[...])

    @pl.when(i == last)
    def _():
        for j in range(1, GATHER_DEPTH):
            slot = (i + j) % GATHER_DEPTH
            _rows_wait(xbuf.at[slot], sem.at[slot])

    @pl.when(i >= nused)
    def _():
        y_ref[...] = jnp.zeros_like(y_ref)


def _moe_experts(h, norm_g, layer, w_gate, w_up, w_down, src, tile_expert, n_used):
    t, d = h.shape
    _, ne, _, f = w_gate.shape
    rows = EXPERT_ROWS
    slots = src.shape[0]
    hbm = pl.BlockSpec(memory_space=pl.ANY)
    return pl.pallas_call(
        functools.partial(_experts_kernel, rows=rows, layer=layer),
        out_shape=jax.ShapeDtypeStruct((slots, d), F32),
        grid_spec=pltpu.PrefetchScalarGridSpec(
            num_scalar_prefetch=3,
            grid=(slots // rows,),
            in_specs=[hbm, pl.BlockSpec((d, 1), lambda i, *_: (0, 0)), hbm, hbm, hbm],
            out_specs=pl.BlockSpec((rows, d), lambda i, *_: (i, 0)),
            scratch_shapes=[pltpu.VMEM((GATHER_DEPTH, rows // SUBLANES, SUBLANES, d), F32),
                            pltpu.VMEM((rows, d), BF16),
                            pltpu.VMEM((d, f), F32), pltpu.VMEM((d, f), F32),
                            pltpu.VMEM((f, d), F32),
                            pltpu.VMEM((d, f), BF16), pltpu.VMEM((d, f), BF16),
                            pltpu.VMEM((f, d), BF16),
                            pltpu.SemaphoreType.DMA((GATHER_DEPTH,)),
                            pltpu.SemaphoreType.DMA((3,))],
        ),
        compiler_params=_params("arbitrary"),
        name="moe_experts",
    )(src, tile_expert, n_used, h, norm_g.reshape(d, 1), w_gate, w_up, w_down)


def _combine_kernel(pos0_ref, pos1_ref, h_ref, meta_ref, y_hbm, o_ref, buf0, buf1, sem, *, rows):
    i = pl.program_id(0)

    def start(tile):
        slot = tile % 2

        def body(g, carry):
            for u in range(SUBLANES):
                tok = tile * rows + g * SUBLANES + u
                _row_copy(y_hbm, pos0_ref[tok], buf0.at[slot], g, u, sem.at[0, slot]).start()
                _row_copy(y_hbm, pos1_ref[tok], buf1.at[slot], g, u, sem.at[1, slot]).start()
            return carry

        lax.fori_loop(0, rows // SUBLANES, body, 0)

    @pl.when(i == 0)
    def _():
        start(0)

    @pl.when(i + 1 < pl.num_programs(0))
    def _():
        start(i + 1)

    slot = i % 2
    _rows_wait(buf0.at[slot], sem.at[0, slot])
    _rows_wait(buf1.at[slot], sem.at[1, slot])
    meta = meta_ref[...]
    w0 = meta[:, META_W:META_W + 1]
    w1 = meta[:, META_W + 1:META_W + 2]
    shape = h_ref.shape
    o_ref[...] = h_ref[...] + w0 * buf0[slot].reshape(shape) + w1 * buf1[slot].reshape(shape)


def _moe_combine(h, meta, y, pos0, pos1):
    t, d = h.shape
    rows = COMBINE_ROWS
    gather_buf = pltpu.VMEM((2, rows // SUBLANES, SUBLANES, d), F32)
    return pl.pallas_call(
        functools.partial(_combine_kernel, rows=rows),
        out_shape=jax.ShapeDtypeStruct((t, d), F32),
        grid_spec=pltpu.PrefetchScalarGridSpec(
            num_scalar_prefetch=2,
            grid=(t // rows,),
            in_specs=[
                pl.BlockSpec((rows, d), lambda i, *_: (i, 0)),
                pl.BlockSpec((rows, LANES), lambda i, *_: (i, 0)),
                pl.BlockSpec(memory_space=pl.ANY),
            ],
            out_specs=pl.BlockSpec((rows, d), lambda i, *_: (i, 0)),
            scratch_shapes=[gather_buf, gather_buf, pltpu.SemaphoreType.DMA((2, 2))],
        ),
        compiler_params=_params("arbitrary"),
        name="moe_combine",
    )(pos0, pos1, h, meta, y)


def _moe_layer(h, layer, norm_g, w_group, b_group, w_router, b_router, w_gate, w_up, w_down):
    t, _ = h.shape
    ne = w_router.shape[1]
    rows = EXPERT_ROWS
    meta, counts = _moe_router(h, norm_g, w_group, b_group, w_router, b_router)
    counts = counts.astype(jnp.int32)
    padded = (counts + rows - 1) // rows * rows
    ends = jnp.cumsum(padded)
    starts = ends - padded
    rec = meta[:, :SUBLANES].T
    expert = rec[META_E:META_E + 2].astype(jnp.int32)
    rank = rec[META_RANK:META_RANK + 2].astype(jnp.int32)
    mine = expert[None] == jnp.arange(ne, dtype=jnp.int32)[:, None, None]
    pos = rank + jnp.sum(jnp.where(mine, starts[:, None, None], 0), axis=0)
    slots = 2 * t + ne * rows
    n_tiles = slots // rows
    n_used = (ends[-1] // rows).astype(jnp.int32)
    tile_start = jnp.arange(n_tiles, dtype=jnp.int32) * rows
    tile_start = jnp.minimum(tile_start, ends[-1] - 1)
    tile_expert = jnp.sum(tile_start[:, None] >= ends[None, :], axis=1, dtype=jnp.int32)
    tile_expert = jnp.minimum(tile_expert, ne - 1)
    pos0 = pos[0]
    pos1 = pos[1]
    pad_ranges = jnp.stack([jnp.append(starts + counts, ends[-1]),
                            jnp.append(ends, slots)], axis=1).reshape(-1).astype(jnp.int32)
    src = _moe_inverse(pos0, pos1, pad_ranges, slots)
    y = _moe_experts(h, norm_g, layer, w_gate, w_up, w_down, src, tile_expert, n_used.reshape(1))
    return _moe_combine(h, meta, y, pos0, pos1)


def _swa_proj_kernel(x_ref, wq_ref, wkv_ref, kg_ref, q_ref, k_ref, v_ref):
    x = x_ref[...]
    xb = x.astype(BF16)
    inv = _rms_inv(x)
    q_ref[...] = (_dot(xb, wq_ref[...]) * inv).astype(BF16)
    kv = _dot(xb, wkv_ref[...]) * inv
    half = kv.shape[1] // 2
    v_ref[...] = kv[:, half:].astype(BF16)
    for hd in range(half // LANES):
        sl = slice(hd * LANES, (hd + 1) * LANES)
        k = kv[:, sl]
        k_ref[:, sl] = (k * _rms_inv(k) * kg_ref[...]).astype(BF16)


def _swa_proj(h, q_norm_g, kv_norm_g, w_q, w_kv, k_norm):
    t, d = h.shape
    hd = SWA_HEAD_DIM
    kvh = w_kv.shape[1] // (2 * hd)
    rows = SWA_PROJ_ROWS
    rep = LANES // hd
    wq = (q_norm_g[:, None] * w_q).astype(BF16)
    w_dup = (kv_norm_g[:, None] * w_kv).astype(BF16).reshape(d, 2 * kvh, 1, hd)
    w_dup = jnp.broadcast_to(w_dup, (d, 2 * kvh, rep, hd)).reshape(d, 2 * kvh * LANES)
    kg = jnp.tile(k_norm, rep).reshape(1, LANES)
    const = lambda shape: pl.BlockSpec(shape, lambda i: (0, 0))
    return pl.pallas_call(
        _swa_proj_kernel,
        out_shape=(
            jax.ShapeDtypeStruct((t, w_q.shape[1]), BF16),
            jax.ShapeDtypeStruct((t, kvh * LANES), BF16),
            jax.ShapeDtypeStruct((t, kvh * LANES), BF16),
        ),
        grid=(t // rows,),
        in_specs=[
            pl.BlockSpec((rows, d), lambda i: (i, 0)),
            const((d, w_q.shape[1])), const((d, 2 * kvh * LANES)), const((1, LANES)),
        ],
        out_specs=(
            pl.BlockSpec((rows, w_q.shape[1]), lambda i: (i, 0)),
            pl.BlockSpec((rows, kvh * LANES), lambda i: (i, 0)),
            pl.BlockSpec((rows, kvh * LANES), lambda i: (i, 0)),
        ),
        compiler_params=_params("parallel"),
        name="swa_proj",
    )(h, wq, w_dup, kg)


def _swa_attn_kernel(q_ref, kp_ref, kc_ref, vp_ref, vc_ref, bias_ref, sinkw_ref, qg_ref,
                     blk_ref, ones_ref, o_ref, *, kvh, pairs, window):
    hd = LANES // 2
    lane = lax.broadcasted_iota(jnp.int32, (window, LANES), 1)
    low = lane < hd
    width = 2 * window
    olow = lax.broadcasted_iota(jnp.int32, (pairs * window, LANES), 1) < hd
    zero = jnp.zeros((), BF16)

    def block_diag(prev, cur):
        return jnp.concatenate([jnp.where(low, prev, zero), jnp.where(low, cur, zero),
                                jnp.where(low, zero, prev), jnp.where(low, zero, cur)], axis=0)

    for h in range(kvh):
        ksl = slice(h * LANES, (h + 1) * LANES)
        kk = block_diag(kp_ref[:, ksl], kc_ref[:, ksl])
        vv = block_diag(vp_ref[:, ksl], vc_ref[:, ksl])
        base = h * pairs * LANES
        q2 = jnp.concatenate([q_ref[:, base + p * LANES:base + (p + 1) * LANES]
                              for p in range(pairs)], axis=0).astype(F32)
        ms = _dot((q2 * q2).astype(BF16), blk_ref[...])
        qn = (q2 * lax.rsqrt(ms + RMS_EPS) * qg_ref[...]).astype(BF16)
        s = _dot_nt(qn, kk) + bias_ref[0, h]
        probs = []
        row_max = []
        for half in range(2):
            sh = s[:, half * width:(half + 1) * width]
            m = jnp.max(sh, axis=-1, keepdims=True)
            probs.append(jnp.exp2(sh - m).astype(BF16))
            row_max.append(m)
        mixed = _dot(jnp.concatenate(probs, axis=1), jnp.concatenate([vv, ones_ref[...]], axis=1))
        den = mixed[:, LANES:] + jnp.exp2(sinkw_ref[h] - jnp.where(olow, row_max[0], row_max[1]))
        o2 = mixed[:, :LANES] * (1.0 / den)
        for p in range(pairs):
            o_ref[:, base + p * LANES:base + (p + 1) * LANES] = (
                o2[p * window:(p + 1) * window].astype(BF16))


def _t5_bucket(dist):
    max_exact = NUM_BUCKETS // 2
    n = np.maximum(dist, 0)
    large = max_exact + (np.log(np.maximum(n, max_exact) / max_exact)
                         / math.log(REL_MAX_DISTANCE / max_exact)
                         * (NUM_BUCKETS - max_exact)).astype(np.int32)
    return np.where(n < max_exact, n, np.minimum(large, NUM_BUCKETS - 1)).astype(np.int32)


def _swa_attn(q, kd, vd, rel_bias, q_norm, sinks, *, batch):
    t, dq = q.shape
    kvh = kd.shape[1] // LANES
    hq = dq // SWA_HEAD_DIM
    pairs = hq // kvh // 2
    w = WINDOW
    nblk = t // batch // w
    row = np.arange(w)[:, None]
    col = np.arange(2 * w)[None, :]
    dist = w + row - col
    band = (dist >= 0) & (dist < w)
    onehot = (_t5_bucket(dist)[..., None] == np.arange(NUM_BUCKETS)).astype(np.float32)
    bias = jnp.dot(jnp.asarray(onehot), rel_bias.astype(F32), precision=lax.Precision.HIGHEST)
    bias = jnp.where(jnp.asarray(band)[..., None], bias, -jnp.inf)
    bias = bias.transpose(2, 0, 1).reshape(kvh, pairs, 2, w, 2 * w).transpose(0, 1, 3, 2, 4)
    bias = bias.reshape(kvh, pairs * w, 4 * w) * LOG2_E
    prev_cols = jnp.asarray((np.arange(4 * w) // w) % 2 == 0)
    bias = jnp.stack([bias, jnp.where(prev_cols, -jnp.inf, bias)])
    sink = sinks.astype(F32).reshape(kvh, pairs, 1, 2, 1) * LOG2_E
    sink_wide = jnp.broadcast_to(sink, (kvh, pairs, w, 2, SWA_HEAD_DIM))
    sink_wide = sink_wide.reshape(kvh, pairs * w, LANES)
    qg = jnp.tile(q_norm, LANES // SWA_HEAD_DIM) * (SWA_HEAD_DIM ** -0.5 * LOG2_E)
    lane = np.arange(LANES)
    blk = ((lane[:, None] // SWA_HEAD_DIM) == (lane[None, :] // SWA_HEAD_DIM)) / SWA_HEAD_DIM
    ones = (np.arange(4 * w)[:, None] // (2 * w)) == (lane[None, :] // SWA_HEAD_DIM)
    cur = lambda b, n: (b * nblk + n, 0)
    prev = lambda b, n: (b * nblk + jnp.maximum(n - 1, 0), 0)
    const2 = lambda shape: pl.BlockSpec(shape, lambda b, n: (0, 0))
    const3 = lambda shape: pl.BlockSpec(shape, lambda b, n: (0, 0, 0))
    return pl.pallas_call(
        functools.partial(_swa_attn_kernel, kvh=kvh, pairs=pairs, window=w),
        out_shape=jax.ShapeDtypeStruct((t, dq), BF16),
        grid=(batch, nblk),
        in_specs=[
            pl.BlockSpec((w, dq), cur),
            pl.BlockSpec((w, kvh * LANES), prev), pl.BlockSpec((w, kvh * LANES), cur),
            pl.BlockSpec((w, kvh * LANES), prev), pl.BlockSpec((w, kvh * LANES), cur),
            pl.BlockSpec((1, kvh, pairs * w, 4 * w), lambda b, n: (jnp.where(n == 0, 1, 0), 0, 0, 0)),
            const3((kvh, pairs * w, LANES)),
            const2((1, LANES)), const2((LANES, LANES)), const2((4 * w, LANES)),
        ],
        out_specs=pl.BlockSpec((w, dq), cur),
        compiler_params=_params("parallel", "parallel"),
        name="swa_attn",
    )(q, kd, kd, vd, vd, bias, sink_wide, qg.reshape(1, LANES), jnp.asarray(blk, BF16),
      jnp.asarray(ones, BF16))


def _swa_layer(h, batch, kv_norm, w_kv, k_norm, rel_bias, norm_g, w_q, q_norm, sinks, w_out):
    q, kd, vd = _swa_proj(h, norm_g, kv_norm, w_q, w_kv, k_norm)
    o = _swa_attn(q, kd, vd, rel_bias, q_norm, sinks, batch=batch)
    return _matmul_residual(o, w_out.astype(BF16), h, rows=PROJ_ROWS)


def kernel(x, gla_norm, gla_w_in, gla_w_gate_up, gla_b_gate, gla_head_norm, gla_w_out, kv_norm, w_kv, k_norm, rel_bias, swa_norm, swa_w_q, swa_q_norm, swa_sinks, swa_w_out, moe_norm, moe_w_group, moe_b_group, moe_w_router, moe_b_router, moe_w_gate, moe_w_up, moe_w_down):
    batch, seq, d = x.shape
    assert gla_norm.shape[0] == 1 and swa_norm.shape[0] == 1 and moe_norm.shape[0] == 2
    h = x.reshape(batch * seq, d)
    h = _gla_layer(h, batch, gla_norm[0], gla_w_in[0], gla_w_gate_up[0], gla_b_gate[0],
                   gla_head_norm[0], gla_w_out[0])
    h = _moe_layer(h, 0, moe_norm[0], moe_w_group[0], moe_b_group[0], moe_w_router[0],
                   moe_b_router[0], moe_w_gate, moe_w_up, moe_w_down)
    h = _swa_layer(h, batch, kv_norm, w_kv, k_norm, rel_bias, swa_norm[0], swa_w_q[0],
                   swa_q_norm[0], swa_sinks[0], swa_w_out[0])
    h = _moe_layer(h, 1, moe_norm[1], moe_w_group[1], moe_b_group[1], moe_w_router[1],
                   moe_b_router[1], moe_w_gate, moe_w_up, moe_w_down)
    return h.reshape(batch, seq, d)
```

```python
import functools
import math

import jax
import jax.numpy as jnp
import numpy as np
from jax import lax
from jax.experimental import pallas as pl
from jax.experimental.pallas import tpu as pltpu

F32 = jnp.float32
BF16 = jnp.bfloat16

RMS_EPS = 1e-6
GLA_HEADS = 4
GLA_GATE_RANK = 16
GLA_GATE_NORMALIZER = 16.0
GLA_LOG_GATE_MIN = -1.0
GLA_CHUNK = 64
SWA_HEAD_DIM = 64
SWA_GROUPS = 8
WINDOW = 128
NUM_BUCKETS = 32
REL_MAX_DISTANCE = 128
MOE_GROUPS = 4
MOE_EXPERTS_PER_GROUP = 8

LANES = 128
SUBLANES = 8
VMEM_LIMIT_BYTES = 56 * 1024 * 1024

PREP_ROWS = 256
PROJ_ROWS = 512
ROUTER_ROWS = 512
EXPERT_ROWS = 512
COMBINE_ROWS = 256
SWA_PROJ_ROWS = 256
LOG2_E = math.log2(math.e)
DMA_UNROLL = 8
GATHER_DEPTH = 3
INVERSE_STEPS = 64


def _params(*semantics):
    return pltpu.CompilerParams(dimension_semantics=semantics, vmem_limit_bytes=VMEM_LIMIT_BYTES)


def _dot(a, b):
    return jnp.dot(a, b, preferred_element_type=F32)


def _dot_nt(a, b):
    return lax.dot_general(a, b, (((1,), (1,)), ((), ())), preferred_element_type=F32)


def _split(x):
    hi = x.astype(BF16)
    lo = (x - hi.astype(F32)).astype(BF16)
    return hi, lo


def _dot3(a, b, dot=_dot):
    ah, al = _split(a)
    bh, bl = _split(b)
    return dot(ah, bh) + dot(al, bh) + dot(ah, bl)


def _rms_inv(x):
    return lax.rsqrt(jnp.mean(x * x, axis=-1, keepdims=True) + RMS_EPS)


def _sigmoid(x):
    return 1.0 / (1.0 + jnp.exp(-x))


def _log_sigmoid(x):
    return jnp.minimum(x, 0.0) - jnp.log1p(jnp.exp(-jnp.abs(x)))


def _norm_matmul_kernel(x_ref, w_ref, o_ref, xb_ref, inv_ref):
    @pl.when(pl.program_id(1) == 0)
    def _():
        x = x_ref[...]
        xb_ref[...] = x.astype(BF16)
        inv_ref[...] = _rms_inv(x)

    o_ref[...] = (_dot(xb_ref[...], w_ref[...]) * inv_ref[...]).astype(o_ref.dtype)


def _norm_matmul(x, g, w, *, rows, cols, out_dtype):
    t, d = x.shape
    n = w.shape[1]
    return pl.pallas_call(
        _norm_matmul_kernel,
        out_shape=jax.ShapeDtypeStruct((t, n), out_dtype),
        grid=(t // rows, n // cols),
        in_specs=[
            pl.BlockSpec((rows, d), lambda i, j: (i, 0)),
            pl.BlockSpec((d, cols), lambda i, j: (0, j)),
        ],
        out_specs=pl.BlockSpec((rows, cols), lambda i, j: (i, j)),
        scratch_shapes=[pltpu.VMEM((rows, d), BF16), pltpu.VMEM((rows, 1), F32)],
        compiler_params=_params("parallel", "arbitrary"),
        name="norm_matmul",
    )(x, (g[:, None] * w).astype(BF16))


def _matmul_residual_kernel(a_ref, w_ref, res_ref, o_ref):
    o_ref[...] = res_ref[...] + _dot(a_ref[...], w_ref[...])


def _matmul_residual(a, w, res, *, rows):
    t, k = a.shape
    n = w.shape[1]
    return pl.pallas_call(
        _matmul_residual_kernel,
        out_shape=jax.ShapeDtypeStruct((t, n), F32),
        grid=(t // rows,),
        in_specs=[
            pl.BlockSpec((rows, k), lambda i: (i, 0)),
            pl.BlockSpec((k, n), lambda i: (0, 0)),
            pl.BlockSpec((rows, n), lambda i: (i, 0)),
        ],
        out_specs=pl.BlockSpec((rows, n), lambda i: (i, 0)),
        compiler_params=_params("parallel"),
        name="matmul_residual",
    )(a, w, res)


def _pair_constants(chunk):
    pair = 2 * chunk
    i = np.arange(pair)[:, None]
    j = np.arange(pair)[None, :]
    tri = (j <= i).astype(np.float32)
    sel = ((i >= chunk) & (j < chunk)).astype(np.float32)
    blk = ((i // chunk) == (j // chunk)).astype(np.float32)
    ones = np.ones((pair, pair), np.float32)
    c_row = np.concatenate([tri, sel], axis=0)
    c_col = np.concatenate([tri.T, sel.T, blk, ones], axis=1)
    return jnp.asarray(c_row, BF16), jnp.asarray(c_col, BF16)


def _gla_prep_kernel(x_ref, g_ref, wq_ref, wkt_ref, wg_ref, wgu_ref, bgr_ref,
                     crow_ref, ccol_ref, qe_ref, qes_ref, ket_ref, klt_ref, ktt_ref, dect_ref,
                     *, scale, pair):
    x = x_ref[...]
    xn = (x * _rms_inv(x) * g_ref[...]).astype(BF16)
    q = _dot(xn, wq_ref[...]) * scale
    kt = _dot_nt(wkt_ref[...], xn)
    glr = _dot(xn, wg_ref[...]).astype(BF16)
    z = _dot(glr, wgu_ref[...]) + bgr_ref[...]
    la = jnp.maximum(_log_sigmoid(z) / GLA_GATE_NORMALIZER, GLA_LOG_GATE_MIN)
    lat = la.T
    crow = crow_ref[...]
    ccol = ccol_ref[...]
    for p in range(x.shape[0] // pair):
        rows = slice(p * pair, (p + 1) * pair)
        hi, lo = _split(la[rows])
        cr = _dot(crow, hi) + _dot(crow, lo)
        b_pair = cr[:pair]
        b_chunk = b_pair - cr[pair:]
        qp = q[rows]
        qe_ref[rows, :] = (qp * jnp.exp(b_chunk)).astype(BF16)
        qes_ref[rows, :] = (qp * jnp.exp(b_pair)).astype(BF16)
        hi, lo = _split(lat[:, rows])
        cc = _dot(hi, ccol) + _dot(lo, ccol)
        bt_pair = cc[:, :pair]
        bt_chunk = bt_pair - cc[:, pair:2 * pair]
        end_chunk = cc[:, 2 * pair:3 * pair]
        end_pair = cc[:, 3 * pair:]
        ktp = kt[:, rows]
        ket_ref[:, rows] = (ktp * jnp.exp(-bt_chunk)).astype(BF16)
        klt_ref[:, rows] = (ktp * jnp.exp(end_chunk - bt_chunk)).astype(BF16)
        ktt_ref[:, rows] = (ktp * jnp.exp(end_pair - bt_pair)).astype(BF16)
        dect_ref[:, rows] = jnp.exp(end_pair)


def _gla_prep(h, norm_g, w_in, w_gate_up, b_gate):
    t, d = h.shape
    hk = d // 2
    rank = w_gate_up.shape[0]
    pair = 2 * GLA_CHUNK
    rows = PREP_ROWS
    dk = hk // GLA_HEADS
    wq = w_in[:, :hk].astype(BF16)
    wkt = w_in[:, hk:2 * hk].T.astype(BF16)
    wg = jnp.pad(w_in[:, 3 * d:], ((0, 0), (0, LANES - rank))).astype(BF16)
    wgu = jnp.pad(w_gate_up, ((0, LANES - rank), (0, 0))).astype(BF16)
    c_row, c_col = _pair_constants(GLA_CHUNK)
    const = lambda shape: pl.BlockSpec(shape, lambda i: (0, 0))
    row_out = pl.BlockSpec((rows, hk), lambda i: (i, 0))
    col_out = pl.BlockSpec((hk, rows), lambda i: (0, i))
    return pl.pallas_call(
        functools.partial(_gla_prep_kernel, scale=dk ** -0.5, pair=pair),
        out_shape=(
            jax.ShapeDtypeStruct((t, hk), BF16),
            jax.ShapeDtypeStruct((t, hk), BF16),
            jax.ShapeDtypeStruct((hk, t), BF16),
            jax.ShapeDtypeStruct((hk, t), BF16),
            jax.ShapeDtypeStruct((hk, t), BF16),
            jax.ShapeDtypeStruct((hk, t), F32),
        ),
        grid=(t // rows,),
        in_specs=[
            pl.BlockSpec((rows, d), lambda i: (i, 0)),
            const((1, d)),
            const((d, hk)),
            const((hk, d)),
            const((d, LANES)),
            const((LANES, hk)),
            const((1, hk)),
            const((2 * pair, pair)),
            const((pair, 4 * pair)),
        ],
        out_specs=(row_out, row_out, col_out, col_out, col_out, col_out),
        compiler_params=_params("parallel"),
        name="gla_prep",
    )(h, norm_g.reshape(1, d), wq, wkt, wg, wgu, b_gate.reshape(1, hk), c_row, c_col)


def _gla_core_kernel(qe_ref, qes_ref, ket_ref, klt_ref, ktt_ref, dect_ref, v_ref, r_ref, hg_ref,
                     o_ref, s_ref, *, heads, chunk):
    @pl.when(pl.program_id(1) == 0)
    def _():
        s_ref[...] = jnp.zeros_like(s_ref)

    pair = 2 * chunk
    dk = qe_ref.shape[1] // heads
    dv = v_ref.shape[1] // heads
    row = lax.broadcasted_iota(jnp.int32, (pair, pair), 0)
    col = lax.broadcasted_iota(jnp.int32, (pair, pair), 1)
    second = row >= chunk
    first_keys = col < chunk
    m_intra = jnp.logical_and(col <= row, jnp.logical_not(jnp.logical_xor(second, col >= chunk)))
    m_cross = jnp.logical_and(second, first_keys)
    for h in range(heads):
        ks = slice(h * dk, (h + 1) * dk)
        vs = slice(h * dv, (h + 1) * dv)
        qe = qe_ref[:, ks]
        v = v_ref[:, vs]
        att = jnp.where(m_intra, _dot(qe, ket_ref[ks, :]),
                        jnp.where(m_cross, _dot(qe, klt_ref[ks, :]), 0.0))
        state = s_ref[h]
        o = _dot(att.astype(BF16), v) + _dot(qes_ref[:, ks], state.astype(BF16))
        dec = dect_ref[ks, :]
        dec = jnp.concatenate([dec] * (dv // pair), axis=1) if dv > pair else dec[:, :dv]
        s_ref[h] = dec * state + _dot(ktt_ref[ks, :], v)
        on = o * _rms_inv(o) * hg_ref[...]
        r = r_ref[:, vs].astype(F32)
        o_ref[:, vs] = (on * (r * _sigmoid(r))).astype(BF16)


def _gla_core(qe, qes, ket, klt, ktt, dect, vr, head_g, *, batch):
    t, hk = qe.shape
    hv = vr.shape[1] // 2
    pair = 2 * GLA_CHUNK
    npair = t // batch // pair
    dk = hk // GLA_HEADS
    dv = hv // GLA_HEADS
    row_k = pl.BlockSpec((pair, hk), lambda b, p: (b * npair + p, 0))
    col_k = pl.BlockSpec((hk, pair), lambda b, p: (0, b * npair + p))
    return pl.pallas_call(
        functools.partial(_gla_core_kernel, heads=GLA_HEADS, chunk=GLA_CHUNK),
        out_shape=jax.ShapeDtypeStruct((t, hv), BF16),
        grid=(batch, npair),
        in_specs=[
            row_k, row_k, col_k, col_k, col_k, col_k,
            pl.BlockSpec((pair, hv), lambda b, p: (b * npair + p, 0)),
            pl.BlockSpec((pair, hv), lambda b, p: (b * npair + p, 1)),
            pl.BlockSpec((1, dv), lambda b, p: (0, 0)),
        ],
        out_specs=pl.BlockSpec((pair, hv), lambda b, p: (b * npair + p, 0)),
        scratch_shapes=[pltpu.VMEM((GLA_HEADS, dk, dv), F32)],
        compiler_params=_params("parallel", "arbitrary"),
        name="gla_core",
    )(qe, qes, ket, klt, ktt, dect, vr, vr, head_g.reshape(1, dv))


def _gla_layer(h, batch, norm_g, w_in, w_gate_up, b_gate, head_g, w_out):
    d = h.shape[1]
    hk = d // 2
    qe, qes, ket, klt, ktt, dect = _gla_prep(h, norm_g, w_in, w_gate_up, b_gate)
    vr = _norm_matmul(h, norm_g, w_in[:, 2 * hk:2 * hk + 2 * d], rows=PROJ_ROWS, cols=d,
                      out_dtype=BF16)
    og = _gla_core(qe, qes, ket, klt, ktt, dect, vr, head_g, batch=batch)
    return _matmul_residual(og, w_out.astype(BF16), h, rows=PROJ_ROWS)


META_E, META_W, META_RANK = 0, 2, 4


def _router_kernel(h_ref, g_ref, w_ref, b_ref, tril_ref, meta_ref, cnt_ref, base_ref,
                   *, groups, per_group):
    @pl.when(pl.program_id(0) == 0)
    def _():
        base_ref[...] = jnp.zeros_like(base_ref)

    x = h_ref[...]
    xn = x * _rms_inv(x) * g_ref[...]
    logits = _dot3(xn, w_ref[...]) + b_ref[...]
    lane = lax.broadcasted_iota(jnp.int32, logits.shape, 1).astype(F32)
    neg = -jnp.inf
    far = float(LANES)

    def first_max(vals):
        m = jnp.max(vals, axis=-1, keepdims=True)
        return m, jnp.min(jnp.where(vals == m, lane, far), axis=-1, keepdims=True)

    gl = jnp.where(lane < groups, logits, neg)
    gmax, gidx = first_max(gl)
    p_group = 1.0 / jnp.sum(jnp.exp(gl - gmax), axis=-1, keepdims=True)
    lo = groups + per_group * gidx
    el = jnp.where(jnp.logical_and(lane >= lo, lane < lo + per_group), logits, neg)
    v1, i1 = first_max(el)
    v2, i2 = first_max(jnp.where(lane == i1, neg, el))
    t = jnp.exp(v2 - v1)
    w1 = p_group / (1.0 + t)
    w2 = p_group * t / (1.0 + t)

    oh1 = lane == i1
    oh2 = lane == i2
    onehot = jnp.where(jnp.logical_or(oh1, oh2), 1.0, 0.0).astype(BF16)
    seen = base_ref[...] + _dot(tril_ref[...], onehot)
    rank1 = jnp.sum(jnp.where(oh1, seen, 0.0), axis=-1, keepdims=True) - 1.0
    rank2 = jnp.sum(jnp.where(oh2, seen, 0.0), axis=-1, keepdims=True) - 1.0
    base_ref[...] = seen[-1:, :]
    cnt_ref[...] = jnp.broadcast_to(seen[-1:, :], cnt_ref.shape)

    rec = jnp.zeros_like(logits)
    for k, val in ((META_E, i1 - groups), (META_E + 1, i2 - groups), (META_W, w1),
                   (META_W + 1, w2), (META_RANK, rank1), (META_RANK + 1, rank2)):
        rec = jnp.where(lane == k, val, rec)
    meta_ref[...] = rec


def _moe_router(h, norm_g, w_group, b_group, w_router, b_router):
    t, d = h.shape
    rows = ROUTER_ROWS
    groups = w_group.shape[1]
    ne = w_router.shape[1]
    pad = LANES - groups - ne
    w = jnp.pad(jnp.concatenate([w_group, w_router], axis=1), ((0, 0), (0, pad)))
    b = jnp.pad(jnp.concatenate([b_group, b_router]), (0, pad)).reshape(1, LANES)
    tril = jnp.asarray(np.tril(np.ones((rows, rows), np.float32)), BF16)
    meta, cnt = pl.pallas_call(
        functools.partial(_router_kernel, groups=groups, per_group=ne // groups),
        out_shape=(jax.ShapeDtypeStruct((t, LANES), F32), jax.ShapeDtypeStruct((8, LANES), F32)),
        grid=(t // rows,),
        in_specs=[
            pl.BlockSpec((rows, d), lambda i: (i, 0)),
            pl.BlockSpec((1, d), lambda i: (0, 0)),
            pl.BlockSpec((d, LANES), lambda i: (0, 0)),
            pl.BlockSpec((1, LANES), lambda i: (0, 0)),
            pl.BlockSpec((rows, rows), lambda i: (0, 0)),
        ],
        out_specs=(pl.BlockSpec((rows, LANES), lambda i: (i, 0)),
                   pl.BlockSpec((8, LANES), lambda i: (0, 0))),
        scratch_shapes=[pltpu.VMEM((1, LANES), F32)],
        compiler_params=_params("arbitrary"),
        name="moe_router",
    )(h, norm_g.reshape(1, d), w, b, tril)
    return meta, cnt[0, groups:groups + ne]


def _inverse_kernel(pos0_ref, pos1_ref, pad_ref, src_ref):
    step = pl.program_id(0)
    tokens = pos0_ref.shape[0] // INVERSE_STEPS

    @pl.when(step == 0)
    def _():
        def clear(i, carry):
            src_ref[i] = 0
            return carry

        for e in range(pad_ref.shape[0] // 2):
            lax.fori_loop(pad_ref[2 * e], pad_ref[2 * e + 1], clear, 0)

    @pl.when(step > 0)
    def _():
        def place(i, carry):
            for u in range(DMA_UNROLL):
                tok = (step - 1) * tokens + i * DMA_UNROLL + u
                src_ref[pos0_ref[tok]] = tok
                src_ref[pos1_ref[tok]] = tok
            return carry

        lax.fori_loop(0, tokens // DMA_UNROLL, place, 0)


def _moe_inverse(pos0, pos1, pad_ranges, slots):
    smem = pl.BlockSpec(memory_space=pltpu.SMEM)
    return pl.pallas_call(
        _inverse_kernel,
        out_shape=jax.ShapeDtypeStruct((slots,), jnp.int32),
        grid=(INVERSE_STEPS + 1,),
        in_specs=[smem, smem, smem],
        out_specs=smem,
        compiler_params=_params("arbitrary"),
        name="moe_inverse",
    )(pos0, pos1, pad_ranges)


def _row_copy(src_hbm, row, dst, r_tile, r_sub, sem):
    return pltpu.make_async_copy(src_hbm.at[pl.ds(row, 1)], dst.at[r_tile, pl.ds(r_sub, 1)], sem)


def _rows_wait(dst, sem):
    pltpu.make_async_copy(dst, dst, sem).wait()


def _experts_kernel(src_ref, texp_ref, tnext_ref, nused_ref, h_hbm, g_ref, wg_hbm, wu_hbm, wd_hbm,
                    y_ref,
                    xbuf, xn_ref, wgs, wus, wds, wgb, wub, wdb, sem, wsem, *, rows, layer):
    i = pl.program_id(0)
    nused = nused_ref[0]
    expert = texp_ref[i]

    def start(tile, slot):
        base = tile * rows
        for r in range(rows):
            _row_copy(h_hbm, src_ref[base + r], xbuf.at[slot], r // SUBLANES, r % SUBLANES,
                      sem.at[slot]).start()

    def weight_copies(e):
        return (pltpu.make_async_copy(wg_hbm.at[layer, e], wgs, wsem.at[0]),
                pltpu.make_async_copy(wu_hbm.at[layer, e], wus, wsem.at[1]),
                pltpu.make_async_copy(wd_hbm.at[layer, e], wds, wsem.at[2]))

    last = nused - 1

    @pl.when(i == 0)
    def _():
        for c in weight_copies(expert):
            c.start()
        for j in range(GATHER_DEPTH - 1):
            start(jnp.minimum(j, last), j)

    fresh = jnp.logical_or(i == 0, expert != texp_ref[jnp.maximum(i - 1, 0)])

    nxt = tnext_ref[i]

    @pl.when(jnp.logical_and(fresh, i < nused))
    def _():
        for c in weight_copies(expert):
            c.wait()
        wgb[...] = (wgs[...] * g_ref[...]).astype(BF16)
        wub[...] = (wus[...] * g_ref[...]).astype(BF16)
        wdb[...] = wds[...].astype(BF16)

    @pl.when(jnp.logical_and(jnp.logical_and(fresh, i < nused), nxt != expert))
    def _():
        for c in weight_copies(nxt):
            c.start()

    @pl.when(i < nused)
    def _():
        slot = i % GATHER_DEPTH
        _rows_wait(xbuf.at[slot], sem.at[slot])
        x = xbuf[slot].reshape(xn_ref.shape)
        xn_ref[...] = x.astype(BF16)
        inv = _rms_inv(x)
        ahead = i + GATHER_DEPTH - 1
        start(jnp.minimum(ahead, last), ahead % GATHER_DEPTH)
        xn = xn_ref[...]
        gate = _dot(xn, wgb[...]) * inv
        up = _dot(xn, wub[...]) * inv
        act = (gate * _sigmoid(gate)) * up
        y_ref[...] = _dot(act.astype(BF16), wdb[...])

    @pl.when(i == last)
    def _():
        for j in range(1, GATHER_DEPTH):
            slot = (i + j) % GATHER_DEPTH
            _rows_wait(xbuf.at[slot], sem.at[slot])

    @pl.when(i >= nused)
    def _():
        y_ref[...] = jnp.zeros_like(y_ref)


def _moe_experts(h, norm_g, layer, w_gate, w_up, w_down, src, tile_expert, tile_next, n_used):
    t, d = h.shape
    _, ne, _, f = w_gate.shape
    rows = EXPERT_ROWS
    slots = src.shape[0]
    hbm = pl.BlockSpec(memory_space=pl.ANY)
    return pl.pallas_call(
        functools.partial(_experts_kernel, rows=rows, layer=layer),
        out_shape=jax.ShapeDtypeStruct((slots, d), F32),
        grid_spec=pltpu.PrefetchScalarGridSpec(
            num_scalar_prefetch=4,
            grid=(slots // rows,),
            in_specs=[hbm, pl.BlockSpec((d, 1), lambda i, *_: (0, 0)), hbm, hbm, hbm],
            out_specs=pl.BlockSpec((rows, d), lambda i, *_: (i, 0)),
            scratch_shapes=[pltpu.VMEM((GATHER_DEPTH, rows // SUBLANES, SUBLANES, d), F32),
                            pltpu.VMEM((rows, d), BF16),
                            pltpu.VMEM((d, f), F32), pltpu.VMEM((d, f), F32),
                            pltpu.VMEM((f, d), F32),
                            pltpu.VMEM((d, f), BF16), pltpu.VMEM((d, f), BF16),
                            pltpu.VMEM((f, d), BF16),
                            pltpu.SemaphoreType.DMA((GATHER_DEPTH,)),
                            pltpu.SemaphoreType.DMA((3,))],
        ),
        compiler_params=_params("arbitrary"),
        name="moe_experts",
    )(src, tile_expert, tile_next, n_used, h, norm_g.reshape(d, 1), w_gate, w_up, w_down)


def _combine_kernel(pos0_ref, pos1_ref, h_ref, meta_ref, y_hbm, o_ref, buf0, buf1, sem, *, rows):
    i = pl.program_id(0)

    def start(tile):
        slot = tile % 2

        def body(g, carry):
            for u in range(SUBLANES):
                tok = tile * rows + g * SUBLANES + u
                _row_copy(y_hbm, pos0_ref[tok], buf0.at[slot], g, u, sem.at[0, slot]).start()
                _row_copy(y_hbm, pos1_ref[tok], buf1.at[slot], g, u, sem.at[1, slot]).start()
            return carry

        lax.fori_loop(0, rows // SUBLANES, body, 0)

    @pl.when(i == 0)
    def _():
        start(0)

    @pl.when(i + 1 < pl.num_programs(0))
    def _():
        start(i + 1)

    slot = i % 2
    _rows_wait(buf0.at[slot], sem.at[0, slot])
    _rows_wait(buf1.at[slot], sem.at[1, slot])
    meta = meta_ref[...]
    w0 = meta[:, META_W:META_W + 1]
    w1 = meta[:, META_W + 1:META_W + 2]
    shape = h_ref.shape
    o_ref[...] = h_ref[...] + w0 * buf0[slot].reshape(shape) + w1 * buf1[slot].reshape(shape)


def _moe_combine(h, meta, y, pos0, pos1):
    t, d = h.shape
    rows = COMBINE_ROWS
    gather_buf = pltpu.VMEM((2, rows // SUBLANES, SUBLANES, d), F32)
    return pl.pallas_call(
        functools.partial(_combine_kernel, rows=rows),
        out_shape=jax.ShapeDtypeStruct((t, d), F32),
        grid_spec=pltpu.PrefetchScalarGridSpec(
            num_scalar_prefetch=2,
            grid=(t // rows,),
            in_specs=[
                pl.BlockSpec((rows, d), lambda i, *_: (i, 0)),
                pl.BlockSpec((rows, LANES), lambda i, *_: (i, 0)),
                pl.BlockSpec(memory_space=pl.ANY),
            ],
            out_specs=pl.BlockSpec((rows, d), lambda i, *_: (i, 0)),
            scratch_shapes=[gather_buf, gather_buf, pltpu.SemaphoreType.DMA((2, 2))],
        ),
        compiler_params=_params("arbitrary"),
        name="moe_combine",
    )(pos0, pos1, h, meta, y)


def _moe_layer(h, layer, norm_g, w_group, b_group, w_router, b_router, w_gate, w_up, w_down):
    t, _ = h.shape
    ne = w_router.shape[1]
    rows = EXPERT_ROWS
    meta, counts = _moe_router(h, norm_g, w_group, b_group, w_router, b_router)
    counts = counts.astype(jnp.int32)
    padded = (counts + rows - 1) // rows * rows
    ends = jnp.cumsum(padded)
    starts = ends - padded
    rec = meta[:, :SUBLANES].T
    expert = rec[META_E:META_E + 2].astype(jnp.int32)
    rank = rec[META_RANK:META_RANK + 2].astype(jnp.int32)
    mine = expert[None] == jnp.arange(ne, dtype=jnp.int32)[:, None, None]
    pos = rank + jnp.sum(jnp.where(mine, starts[:, None, None], 0), axis=0)
    slots = 2 * t + ne * rows
    n_tiles = slots // rows
    n_used = (ends[-1] // rows).astype(jnp.int32)
    tile_start = jnp.arange(n_tiles, dtype=jnp.int32) * rows
    tile_start = jnp.minimum(tile_start, ends[-1] - 1)
    tile_expert = jnp.sum(tile_start[:, None] >= ends[None, :], axis=1, dtype=jnp.int32)
    tile_expert = jnp.minimum(tile_expert, ne - 1)
    tile_next = tile_expert[jnp.minimum(ends[tile_expert] // rows, n_used - 1)]
    pos0 = pos[0]
    pos1 = pos[1]
    pad_ranges = jnp.stack([jnp.append(starts + counts, ends[-1]),
                            jnp.append(ends, slots)], axis=1).reshape(-1).astype(jnp.int32)
    src = _moe_inverse(pos0, pos1, pad_ranges, slots)
    y = _moe_experts(h, norm_g, layer, w_gate, w_up, w_down, src, tile_expert, tile_next,
                     n_used.reshape(1))
    return _moe_combine(h, meta, y, pos0, pos1)


def _swa_proj_kernel(x_ref, wq_ref, wkv_ref, kg_ref, q_ref, k_ref, v_ref):
    x = x_ref[...]
    xb = x.astype(BF16)
    inv = _rms_inv(x)
    q_ref[...] = (_dot(xb, wq_ref[...]) * inv).astype(BF16)
    kv = _dot(xb, wkv_ref[...]) * inv
    half = kv.shape[1] // 2
    v_ref[...] = kv[:, half:].astype(BF16)
    for hd in range(half // LANES):
        sl = slice(hd * LANES, (hd + 1) * LANES)
        k = kv[:, sl]
        k_ref[:, sl] = (k * _rms_inv(k) * kg_ref[...]).astype(BF16)


def _swa_proj(h, q_norm_g, kv_norm_g, w_q, w_kv, k_norm):
    t, d = h.shape
    hd = SWA_HEAD_DIM
    kvh = w_kv.shape[1] // (2 * hd)
    rows = SWA_PROJ_ROWS
    rep = LANES // hd
    wq = (q_norm_g[:, None] * w_q).astype(BF16)
    w_dup = (kv_norm_g[:, None] * w_kv).astype(BF16).reshape(d, 2 * kvh, 1, hd)
    w_dup = jnp.broadcast_to(w_dup, (d, 2 * kvh, rep, hd)).reshape(d, 2 * kvh * LANES)
    kg = jnp.tile(k_norm, rep).reshape(1, LANES)
    const = lambda shape: pl.BlockSpec(shape, lambda i: (0, 0))
    return pl.pallas_call(
        _swa_proj_kernel,
        out_shape=(
            jax.ShapeDtypeStruct((t, w_q.shape[1]), BF16),
            jax.ShapeDtypeStruct((t, kvh * LANES), BF16),
            jax.ShapeDtypeStruct((t, kvh * LANES), BF16),
        ),
        grid=(t // rows,),
        in_specs=[
            pl.BlockSpec((rows, d), lambda i: (i, 0)),
            const((d, w_q.shape[1])), const((d, 2 * kvh * LANES)), const((1, LANES)),
        ],
        out_specs=(
            pl.BlockSpec((rows, w_q.shape[1]), lambda i: (i, 0)),
            pl.BlockSpec((rows, kvh * LANES), lambda i: (i, 0)),
            pl.BlockSpec((rows, kvh * LANES), lambda i: (i, 0)),
        ),
        compiler_params=_params("parallel"),
        name="swa_proj",
    )(h, wq, w_dup, kg)


def _swa_attn_kernel(q_ref, kp_ref, kc_ref, vp_ref, vc_ref, bias_ref, sinkw_ref, qg_ref,
                     blk_ref, ones_ref, o_ref, *, kvh, pairs, window):
    hd = LANES // 2
    lane = lax.broadcasted_iota(jnp.int32, (window, LANES), 1)
    low = lane < hd
    width = 2 * window
    olow = lax.broadcasted_iota(jnp.int32, (pairs * window, LANES), 1) < hd
    zero = jnp.zeros((), BF16)

    def block_diag(prev, cur):
        return jnp.concatenate([jnp.where(low, prev, zero), jnp.where(low, cur, zero),
                                jnp.where(low, zero, prev), jnp.where(low, zero, cur)], axis=0)

    for h in range(kvh):
        ksl = slice(h * LANES, (h + 1) * LANES)
        kk = block_diag(kp_ref[:, ksl], kc_ref[:, ksl])
        vv = block_diag(vp_ref[:, ksl], vc_ref[:, ksl])
        base = h * pairs * LANES
        q2 = jnp.concatenate([q_ref[:, base + p * LANES:base + (p + 1) * LANES]
                              for p in range(pairs)], axis=0).astype(F32)
        ms = _dot((q2 * q2).astype(BF16), blk_ref[...])
        qn = (q2 * lax.rsqrt(ms + RMS_EPS) * qg_ref[...]).astype(BF16)
        s = _dot_nt(qn, kk) + bias_ref[0, h]
        probs = []
        row_max = []
        for half in range(2):
            sh = s[:, half * width:(half + 1) * width]
            m = jnp.max(sh, axis=-1, keepdims=True)
            probs.append(jnp.exp2(sh - m).astype(BF16))
            row_max.append(m)
        mixed = _dot(jnp.concatenate(probs, axis=1), jnp.concatenate([vv, ones_ref[...]], axis=1))
        den = mixed[:, LANES:] + jnp.exp2(sinkw_ref[h] - jnp.where(olow, row_max[0], row_max[1]))
        o2 = mixed[:, :LANES] * (1.0 / den)
        for p in range(pairs):
            o_ref[:, base + p * LANES:base + (p + 1) * LANES] = (
                o2[p * window:(p + 1) * window].astype(BF16))


def _t5_bucket(dist):
    max_exact = NUM_BUCKETS // 2
    n = np.maximum(dist, 0)
    large = max_exact + (np.log(np.maximum(n, max_exact) / max_exact)
                         / math.log(REL_MAX_DISTANCE / max_exact)
                         * (NUM_BUCKETS - max_exact)).astype(np.int32)
    return np.where(n < max_exact, n, np.minimum(large, NUM_BUCKETS - 1)).astype(np.int32)


def _swa_attn(q, kd, vd, rel_bias, q_norm, sinks, *, batch):
    t, dq = q.shape
    kvh = kd.shape[1] // LANES
    hq = dq // SWA_HEAD_DIM
    pairs = hq // kvh // 2
    w = WINDOW
    nblk = t // batch // w
    row = np.arange(w)[:, None]
    col = np.arange(2 * w)[None, :]
    dist = w + row - col
    band = (dist >= 0) & (dist < w)
    onehot = (_t5_bucket(dist)[..., None] == np.arange(NUM_BUCKETS)).astype(np.float32)
    bias = jnp.dot(jnp.asarray(onehot), rel_bias.astype(F32), precision=lax.Precision.HIGHEST)
    bias = jnp.where(jnp.asarray(band)[..., None], bias, -jnp.inf)
    bias = bias.transpose(2, 0, 1).reshape(kvh, pairs, 2, w, 2 * w).transpose(0, 1, 3, 2, 4)
    bias = bias.reshape(kvh, pairs * w, 4 * w) * LOG2_E
    prev_cols = jnp.asarray((np.arange(4 * w) // w) % 2 == 0)
    bias = jnp.stack([bias, jnp.where(prev_cols, -jnp.inf, bias)])
    sink = sinks.astype(F32).reshape(kvh, pairs, 1, 2, 1) * LOG2_E
    sink_wide = jnp.broadcast_to(sink, (kvh, pairs, w, 2, SWA_HEAD_DIM))
    sink_wide = sink_wide.reshape(kvh, pairs * w, LANES)
    qg = jnp.tile(q_norm, LANES // SWA_HEAD_DIM) * (SWA_HEAD_DIM ** -0.5 * LOG2_E)
    lane = np.arange(LANES)
    blk = ((lane[:, None] // SWA_HEAD_DIM) == (lane[None, :] // SWA_HEAD_DIM)) / SWA_HEAD_DIM
    ones = (np.arange(4 * w)[:, None] // (2 * w)) == (lane[None, :] // SWA_HEAD_DIM)
    cur = lambda b, n: (b * nblk + n, 0)
    prev = lambda b, n: (b * nblk + jnp.maximum(n - 1, 0), 0)
    const2 = lambda shape: pl.BlockSpec(shape, lambda b, n: (0, 0))
    const3 = lambda shape: pl.BlockSpec(shape, lambda b, n: (0, 0, 0))
    return pl.pallas_call(
        functools.partial(_swa_attn_kernel, kvh=kvh, pairs=pairs, window=w),
        out_shape=jax.ShapeDtypeStruct((t, dq), BF16),
        grid=(batch, nblk),
        in_specs=[
            pl.BlockSpec((w, dq), cur),
            pl.BlockSpec((w, kvh * LANES), prev), pl.BlockSpec((w, kvh * LANES), cur),
            pl.BlockSpec((w, kvh * LANES), prev), pl.BlockSpec((w, kvh * LANES), cur),
            pl.BlockSpec((1, kvh, pairs * w, 4 * w), lambda b, n: (jnp.where(n == 0, 1, 0), 0, 0, 0)),
            const3((kvh, pairs * w, LANES)),
            const2((1, LANES)), const2((LANES, LANES)), const2((4 * w, LANES)),
        ],
        out_specs=pl.BlockSpec((w, dq), cur),
        compiler_params=_params("parallel", "parallel"),
        name="swa_attn",
    )(q, kd, kd, vd, vd, bias, sink_wide, qg.reshape(1, LANES), jnp.asarray(blk, BF16),
      jnp.asarray(ones, BF16))


def _swa_layer(h, batch, kv_norm, w_kv, k_norm, rel_bias, norm_g, w_q, q_norm, sinks, w_out):
    q, kd, vd = _swa_proj(h, norm_g, kv_norm, w_q, w_kv, k_norm)
    o = _swa_attn(q, kd, vd, rel_bias, q_norm, sinks, batch=batch)
    return _matmul_residual(o, w_out.astype(BF16), h, rows=PROJ_ROWS)


def kernel(x, gla_norm, gla_w_in, gla_w_gate_up, gla_b_gate, gla_head_norm, gla_w_out, kv_norm, w_kv, k_norm, rel_bias, swa_norm, swa_w_q, swa_q_norm, swa_sinks, swa_w_out, moe_norm, moe_w_group, moe_b_group, moe_w_router, moe_b_router, moe_w_gate, moe_w_up, moe_w_down):
    batch, seq, d = x.shape
    assert gla_norm.shape[0] == 1 and swa_norm.shape[0] == 1 and moe_norm.shape[0] == 2
    h = x.reshape(batch * seq, d)
    h = _gla_layer(h, batch, gla_norm[0], gla_w_in[0], gla_w_gate_up[0], gla_b_gate[0],
                   gla_head_norm[0], gla_w_out[0])
    h = _moe_layer(h, 0, moe_norm[0], moe_w_group[0], moe_b_group[0], moe_w_router[0],
                   moe_b_router[0], moe_w_gate, moe_w_up, moe_w_down)
    h = _swa_layer(h, batch, kv_norm, w_kv, k_norm, rel_bias, swa_norm[0], swa_w_q[0],
                   swa_q_norm[0], swa_sinks[0], swa_w_out[0])
    h = _moe_layer(h, 1, moe_norm[1], moe_w_group[1], moe_b_group[1], moe_w_router[1],
                   moe_b_router[1], moe_w_gate, moe_w_up, moe_w_down)
    return h.reshape(batch, seq, d)
```

```python
import functools
import math

import jax
import jax.numpy as jnp
import numpy as np
from jax import lax
from jax.experimental import pallas as pl
from jax.experimental.pallas import tpu as pltpu

F32 = jnp.float32
BF16 = jnp.bfloat16

RMS_EPS = 1e-6
GLA_HEADS = 4
GLA_GATE_RANK = 16
GLA_GATE_NORMALIZER = 16.0
GLA_LOG_GATE_MIN = -1.0
GLA_CHUNK = 64
SWA_HEAD_DIM = 64
SWA_GROUPS = 8
WINDOW = 128
NUM_BUCKETS = 32
REL_MAX_DISTANCE = 128
MOE_GROUPS = 4
MOE_EXPERTS_PER_GROUP = 8

LANES = 128
SUBLANES = 8
VMEM_LIMIT_BYTES = 56 * 1024 * 1024

PREP_ROWS = 256
PROJ_ROWS = 512
ROUTER_ROWS = 512
EXPERT_ROWS = 512
COMBINE_ROWS = 256
SWA_PROJ_ROWS = 256
LOG2_E = math.log2(math.e)
DISPATCH_ROWS = 256


def _params(*semantics):
    return pltpu.CompilerParams(dimension_semantics=semantics, vmem_limit_bytes=VMEM_LIMIT_BYTES)


def _dot(a, b):
    return jnp.dot(a, b, preferred_element_type=F32)


def _dot_nt(a, b):
    return lax.dot_general(a, b, (((1,), (1,)), ((), ())), preferred_element_type=F32)


def _split(x):
    hi = x.astype(BF16)
    lo = (x - hi.astype(F32)).astype(BF16)
    return hi, lo


def _dot3(a, b, dot=_dot):
    ah, al = _split(a)
    bh, bl = _split(b)
    return dot(ah, bh) + dot(al, bh) + dot(ah, bl)


def _rms_inv(x):
    return lax.rsqrt(jnp.mean(x * x, axis=-1, keepdims=True) + RMS_EPS)


def _sigmoid(x):
    return 1.0 / (1.0 + jnp.exp(-x))


def _log_sigmoid(x):
    return jnp.minimum(x, 0.0) - jnp.log1p(jnp.exp(-jnp.abs(x)))


def _norm_matmul_kernel(x_ref, w_ref, o_ref, xb_ref, inv_ref):
    @pl.when(pl.program_id(1) == 0)
    def _():
        x = x_ref[...]
        xb_ref[...] = x.astype(BF16)
        inv_ref[...] = _rms_inv(x)

    o_ref[...] = (_dot(xb_ref[...], w_ref[...]) * inv_ref[...]).astype(o_ref.dtype)


def _norm_matmul(x, g, w, *, rows, cols, out_dtype):
    t, d = x.shape
    n = w.shape[1]
    return pl.pallas_call(
        _norm_matmul_kernel,
        out_shape=jax.ShapeDtypeStruct((t, n), out_dtype),
        grid=(t // rows, n // cols),
        in_specs=[
            pl.BlockSpec((rows, d), lambda i, j: (i, 0)),
            pl.BlockSpec((d, cols), lambda i, j: (0, j)),
        ],
        out_specs=pl.BlockSpec((rows, cols), lambda i, j: (i, j)),
        scratch_shapes=[pltpu.VMEM((rows, d), BF16), pltpu.VMEM((rows, 1), F32)],
        compiler_params=_params("parallel", "arbitrary"),
        name="norm_matmul",
    )(x, (g[:, None] * w).astype(BF16))


def _matmul_residual_kernel(a_ref, w_ref, res_ref, o_ref):
    o_ref[...] = res_ref[...] + _dot(a_ref[...], w_ref[...])


def _matmul_residual(a, w, res, *, rows):
    t, k = a.shape
    n = w.shape[1]
    return pl.pallas_call(
        _matmul_residual_kernel,
        out_shape=jax.ShapeDtypeStruct((t, n), F32),
        grid=(t // rows,),
        in_specs=[
            pl.BlockSpec((rows, k), lambda i: (i, 0)),
            pl.BlockSpec((k, n), lambda i: (0, 0)),
            pl.BlockSpec((rows, n), lambda i: (i, 0)),
        ],
        out_specs=pl.BlockSpec((rows, n), lambda i: (i, 0)),
        compiler_params=_params("parallel"),
        name="matmul_residual",
    )(a, w, res)


def _pair_constants(chunk):
    pair = 2 * chunk
    i = np.arange(pair)[:, None]
    j = np.arange(pair)[None, :]
    tri = (j <= i).astype(np.float32)
    sel = ((i >= chunk) & (j < chunk)).astype(np.float32)
    blk = ((i // chunk) == (j // chunk)).astype(np.float32)
    ones = np.ones((pair, pair), np.float32)
    c_row = np.concatenate([tri, sel], axis=0)
    c_col = np.concatenate([tri.T, sel.T, blk, ones], axis=1)
    return jnp.asarray(c_row, BF16), jnp.asarray(c_col, BF16)


def _gla_prep_kernel(x_ref, g_ref, wq_ref, wkt_ref, wg_ref, wgu_ref, bgr_ref,
                     crow_ref, ccol_ref, qe_ref, qes_ref, ket_ref, klt_ref, ktt_ref, dect_ref,
                     *, scale, pair):
    x = x_ref[...]
    xn = (x * _rms_inv(x) * g_ref[...]).astype(BF16)
    q = _dot(xn, wq_ref[...]) * scale
    kt = _dot_nt(wkt_ref[...], xn)
    glr = _dot(xn, wg_ref[...]).astype(BF16)
    z = _dot(glr, wgu_ref[...]) + bgr_ref[...]
    la = jnp.maximum(_log_sigmoid(z) / GLA_GATE_NORMALIZER, GLA_LOG_GATE_MIN)
    lat = la.T
    crow = crow_ref[...]
    ccol = ccol_ref[...]
    for p in range(x.shape[0] // pair):
        rows = slice(p * pair, (p + 1) * pair)
        hi, lo = _split(la[rows])
        cr = _dot(crow, hi) + _dot(crow, lo)
        b_pair = cr[:pair]
        b_chunk = b_pair - cr[pair:]
        qp = q[rows]
        qe_ref[rows, :] = (qp * jnp.exp(b_chunk)).astype(BF16)
        qes_ref[rows, :] = (qp * jnp.exp(b_pair)).astype(BF16)
        hi, lo = _split(lat[:, rows])
        cc = _dot(hi, ccol) + _dot(lo, ccol)
        bt_pair = cc[:, :pair]
        bt_chunk = bt_pair - cc[:, pair:2 * pair]
        end_chunk = cc[:, 2 * pair:3 * pair]
        end_pair = cc[:, 3 * pair:]
        ktp = kt[:, rows]
        ket_ref[:, rows] = (ktp * jnp.exp(-bt_chunk)).astype(BF16)
        klt_ref[:, rows] = (ktp * jnp.exp(end_chunk - bt_chunk)).astype(BF16)
        ktt_ref[:, rows] = (ktp * jnp.exp(end_pair - bt_pair)).astype(BF16)
        dect_ref[:, rows] = jnp.exp(end_pair)


def _gla_prep(h, norm_g, w_in, w_gate_up, b_gate):
    t, d = h.shape
    hk = d // 2
    rank = w_gate_up.shape[0]
    pair = 2 * GLA_CHUNK
    rows = PREP_ROWS
    dk = hk // GLA_HEADS
    wq = w_in[:, :hk].astype(BF16)
    wkt = w_in[:, hk:2 * hk].T.astype(BF16)
    wg = jnp.pad(w_in[:, 3 * d:], ((0, 0), (0, LANES - rank))).astype(BF16)
    wgu = jnp.pad(w_gate_up, ((0, LANES - rank), (0, 0))).astype(BF16)
    c_row, c_col = _pair_constants(GLA_CHUNK)
    const = lambda shape: pl.BlockSpec(shape, lambda i: (0, 0))
    row_out = pl.BlockSpec((rows, hk), lambda i: (i, 0))
    col_out = pl.BlockSpec((hk, rows), lambda i: (0, i))
    return pl.pallas_call(
        functools.partial(_gla_prep_kernel, scale=dk ** -0.5, pair=pair),
        out_shape=(
            jax.ShapeDtypeStruct((t, hk), BF16),
            jax.ShapeDtypeStruct((t, hk), BF16),
            jax.ShapeDtypeStruct((hk, t), BF16),
            jax.ShapeDtypeStruct((hk, t), BF16),
            jax.ShapeDtypeStruct((hk, t), BF16),
            jax.ShapeDtypeStruct((hk, t), F32),
        ),
        grid=(t // rows,),
        in_specs=[
            pl.BlockSpec((rows, d), lambda i: (i, 0)),
            const((1, d)),
            const((d, hk)),
            const((hk, d)),
            const((d, LANES)),
            const((LANES, hk)),
            const((1, hk)),
            const((2 * pair, pair)),
            const((pair, 4 * pair)),
        ],
        out_specs=(row_out, row_out, col_out, col_out, col_out, col_out),
        compiler_params=_params("parallel"),
        name="gla_prep",
    )(h, norm_g.reshape(1, d), wq, wkt, wg, wgu, b_gate.reshape(1, hk), c_row, c_col)


def _gla_core_kernel(qe_ref, qes_ref, ket_ref, klt_ref, ktt_ref, dect_ref, v_ref, r_ref, hg_ref,
                     o_ref, s_ref, *, heads, chunk):
    @pl.when(pl.program_id(1) == 0)
    def _():
        s_ref[...] = jnp.zeros_like(s_ref)

    pair = 2 * chunk
    dk = qe_ref.shape[1] // heads
    dv = v_ref.shape[1] // heads
    row = lax.broadcasted_iota(jnp.int32, (pair, pair), 0)
    col = lax.broadcasted_iota(jnp.int32, (pair, pair), 1)
    second = row >= chunk
    first_keys = col < chunk
    m_intra = jnp.logical_and(col <= row, jnp.logical_not(jnp.logical_xor(second, col >= chunk)))
    m_cross = jnp.logical_and(second, first_keys)
    for h in range(heads):
        ks = slice(h * dk, (h + 1) * dk)
        vs = slice(h * dv, (h + 1) * dv)
        qe = qe_ref[:, ks]
        v = v_ref[:, vs]
        att = jnp.where(m_intra, _dot(qe, ket_ref[ks, :]),
                        jnp.where(m_cross, _dot(qe, klt_ref[ks, :]), 0.0))
        state = s_ref[h]
        o = _dot(att.astype(BF16), v) + _dot(qes_ref[:, ks], state.astype(BF16))
        dec = dect_ref[ks, :]
        dec = jnp.concatenate([dec] * (dv // pair), axis=1) if dv > pair else dec[:, :dv]
        s_ref[h] = dec * state + _dot(ktt_ref[ks, :], v)
        on = o * _rms_inv(o) * hg_ref[...]
        r = r_ref[:, vs].astype(F32)
        o_ref[:, vs] = (on * (r * _sigmoid(r))).astype(BF16)


def _gla_core(qe, qes, ket, klt, ktt, dect, vr, head_g, *, batch):
    t, hk = qe.shape
    hv = vr.shape[1] // 2
    pair = 2 * GLA_CHUNK
    npair = t // batch // pair
    dk = hk // GLA_HEADS
    dv = hv // GLA_HEADS
    row_k = pl.BlockSpec((pair, hk), lambda b, p: (b * npair + p, 0))
    col_k = pl.BlockSpec((hk, pair), lambda b, p: (0, b * npair + p))
    return pl.pallas_call(
        functools.partial(_gla_core_kernel, heads=GLA_HEADS, chunk=GLA_CHUNK),
        out_shape=jax.ShapeDtypeStruct((t, hv), BF16),
        grid=(batch, npair),
        in_specs=[
            row_k, row_k, col_k, col_k, col_k, col_k,
            pl.BlockSpec((pair, hv), lambda b, p: (b * npair + p, 0)),
            pl.BlockSpec((pair, hv), lambda b, p: (b * npair + p, 1)),
            pl.BlockSpec((1, dv), lambda b, p: (0, 0)),
        ],
        out_specs=pl.BlockSpec((pair, hv), lambda b, p: (b * npair + p, 0)),
        scratch_shapes=[pltpu.VMEM((GLA_HEADS, dk, dv), F32)],
        compiler_params=_params("parallel", "arbitrary"),
        name="gla_core",
    )(qe, qes, ket, klt, ktt, dect, vr, vr, head_g.reshape(1, dv))


def _gla_layer(h, batch, norm_g, w_in, w_gate_up, b_gate, head_g, w_out):
    d = h.shape[1]
    hk = d // 2
    qe, qes, ket, klt, ktt, dect = _gla_prep(h, norm_g, w_in, w_gate_up, b_gate)
    vr = _norm_matmul(h, norm_g, w_in[:, 2 * hk:2 * hk + 2 * d], rows=PROJ_ROWS, cols=d,
                      out_dtype=BF16)
    og = _gla_core(qe, qes, ket, klt, ktt, dect, vr, head_g, batch=batch)
    return _matmul_residual(og, w_out.astype(BF16), h, rows=PROJ_ROWS)


META_E, META_W, META_RANK = 0, 2, 4


def _router_kernel(h_ref, g_ref, w_ref, b_ref, tril_ref, meta_ref, cnt_ref, base_ref,
                   *, groups, per_group):
    @pl.when(pl.program_id(0) == 0)
    def _():
        base_ref[...] = jnp.zeros_like(base_ref)

    x = h_ref[...]
    xn = x * _rms_inv(x) * g_ref[...]
    logits = _dot3(xn, w_ref[...]) + b_ref[...]
    lane = lax.broadcasted_iota(jnp.int32, logits.shape, 1).astype(F32)
    neg = -jnp.inf
    far = float(LANES)

    def first_max(vals):
        m = jnp.max(vals, axis=-1, keepdims=True)
        return m, jnp.min(jnp.where(vals == m, lane, far), axis=-1, keepdims=True)

    gl = jnp.where(lane < groups, logits, neg)
    gmax, gidx = first_max(gl)
    p_group = 1.0 / jnp.sum(jnp.exp(gl - gmax), axis=-1, keepdims=True)
    lo = groups + per_group * gidx
    el = jnp.where(jnp.logical_and(lane >= lo, lane < lo + per_group), logits, neg)
    v1, i1 = first_max(el)
    v2, i2 = first_max(jnp.where(lane == i1, neg, el))
    t = jnp.exp(v2 - v1)
    w1 = p_group / (1.0 + t)
    w2 = p_group * t / (1.0 + t)

    oh1 = lane == i1
    oh2 = lane == i2
    onehot = jnp.where(jnp.logical_or(oh1, oh2), 1.0, 0.0).astype(BF16)
    seen = base_ref[...] + _dot(tril_ref[...], onehot)
    rank1 = jnp.sum(jnp.where(oh1, seen, 0.0), axis=-1, keepdims=True) - 1.0
    rank2 = jnp.sum(jnp.where(oh2, seen, 0.0), axis=-1, keepdims=True) - 1.0
    base_ref[...] = seen[-1:, :]
    cnt_ref[...] = jnp.broadcast_to(seen[-1:, :], cnt_ref.shape)

    rec = jnp.zeros_like(logits)
    for k, val in ((META_E, i1 - groups), (META_E + 1, i2 - groups), (META_W, w1),
                   (META_W + 1, w2), (META_RANK, rank1), (META_RANK + 1, rank2)):
        rec = jnp.where(lane == k, val, rec)
    meta_ref[...] = rec


def _moe_router(h, norm_g, w_group, b_group, w_router, b_router):
    t, d = h.shape
    rows = ROUTER_ROWS
    groups = w_group.shape[1]
    ne = w_router.shape[1]
    pad = LANES - groups - ne
    w = jnp.pad(jnp.concatenate([w_group, w_router], axis=1), ((0, 0), (0, pad)))
    b = jnp.pad(jnp.concatenate([b_group, b_router]), (0, pad)).reshape(1, LANES)
    tril = jnp.asarray(np.tril(np.ones((rows, rows), np.float32)), BF16)
    meta, cnt = pl.pallas_call(
        functools.partial(_router_kernel, groups=groups, per_group=ne // groups),
        out_shape=(jax.ShapeDtypeStruct((t, LANES), F32), jax.ShapeDtypeStruct((8, LANES), F32)),
        grid=(t // rows,),
        in_specs=[
            pl.BlockSpec((rows, d), lambda i: (i, 0)),
            pl.BlockSpec((1, d), lambda i: (0, 0)),
            pl.BlockSpec((d, LANES), lambda i: (0, 0)),
            pl.BlockSpec((1, LANES), lambda i: (0, 0)),
            pl.BlockSpec((rows, rows), lambda i: (0, 0)),
        ],
        out_specs=(pl.BlockSpec((rows, LANES), lambda i: (i, 0)),
                   pl.BlockSpec((8, LANES), lambda i: (0, 0))),
        scratch_shapes=[pltpu.VMEM((1, LANES), F32)],
        compiler_params=_params("arbitrary"),
        name="moe_router",
    )(h, norm_g.reshape(1, d), w, b, tril)
    return meta, cnt[0, groups:groups + ne]


def _dispatch_kernel(pos0_ref, pos1_ref, pad_ref, h_hbm, xs_hbm, zbuf, sem, zsem, *, rows):
    i = pl.program_id(0)
    steps = pl.num_programs(0) - 1

    @pl.when(i < steps)
    def _():
        def body(g, carry):
            for u in range(SUBLANES):
                tok = i * rows + g * SUBLANES + u
                src = h_hbm.at[pl.ds(tok, 1)]
                pltpu.make_async_copy(src, xs_hbm.at[pl.ds(pos0_ref[tok], 1)], sem).start()
                pltpu.make_async_copy(src, xs_hbm.at[pl.ds(pos1_ref[tok], 1)], sem).start()
            return carry

        lax.fori_loop(0, rows // SUBLANES, body, 0)

    @pl.when(i > 0)
    def _():
        for _ in range(2):
            pltpu.make_async_copy(h_hbm.at[pl.ds(0, rows)], xs_hbm.at[pl.ds(0, rows)], sem).wait()

    @pl.when(i == steps)
    def _():
        zrows = zbuf.shape[0]
        zbuf[...] = jnp.zeros_like(zbuf)
        nseg = pad_ref.shape[0] // 2 - 1
        sizes = [zrows >> k for k in range(zrows.bit_length()) if zrows >> k >= SUBLANES]

        def pieces(first, end):
            head = (-first) % SUBLANES
            out = [(k < head, pltpu.make_async_copy(zbuf.at[pl.ds(0, 1)],
                                                    xs_hbm.at[pl.ds(first + k, 1)], zsem))
                   for k in range(SUBLANES - 1)]
            off = first + head
            count = end - off
            for s in sizes:
                take = (count & s) != 0
                dst = xs_hbm.at[pl.ds(pl.multiple_of(off, SUBLANES), s)]
                out.append((take, pltpu.make_async_copy(zbuf.at[pl.ds(0, s)], dst, zsem)))
                off = off + jnp.where(take, s, 0)
            return out

        for e in range(nseg):
            ps = pieces(pad_ref[2 * e], pad_ref[2 * e + 1])
            for take, c in ps:
                @pl.when(take)
                def _(c=c):
                    c.start()
            for take, c in ps:
                @pl.when(take)
                def _(c=c):
                    c.wait()

        first = pad_ref[2 * nseg]
        chunks = (pad_ref[2 * nseg + 1] - first) // zrows

        def tail_copy(k):
            dst = xs_hbm.at[pl.ds(pl.multiple_of(first + k * zrows, SUBLANES), zrows)]
            return pltpu.make_async_copy(zbuf, dst, zsem)

        def tail_start(k, carry):
            tail_copy(k).start()
            return carry

        def tail_wait(k, carry):
            tail_copy(k).wait()
            return carry

        lax.fori_loop(0, chunks, tail_start, 0)
        lax.fori_loop(0, chunks, tail_wait, 0)


def _moe_dispatch(h, pos0, pos1, pad_ranges, slots):
    t, d = h.shape
    rows = DISPATCH_ROWS
    hbm = pl.BlockSpec(memory_space=pl.ANY)
    return pl.pallas_call(
        functools.partial(_dispatch_kernel, rows=rows),
        out_shape=jax.ShapeDtypeStruct((slots, d), F32),
        grid_spec=pltpu.PrefetchScalarGridSpec(
            num_scalar_prefetch=3,
            grid=(t // rows + 1,),
            in_specs=[hbm],
            out_specs=hbm,
            scratch_shapes=[pltpu.VMEM((EXPERT_ROWS // 2, d), F32), pltpu.SemaphoreType.DMA,
                            pltpu.SemaphoreType.DMA],
        ),
        compiler_params=_params("arbitrary"),
        name="moe_dispatch",
    )(pos0, pos1, pad_ranges, h)


def _row_copy(src_hbm, row, dst, r_tile, r_sub, sem):
    return pltpu.make_async_copy(src_hbm.at[pl.ds(row, 1)], dst.at[r_tile, pl.ds(r_sub, 1)], sem)


def _rows_wait(dst, sem):
    pltpu.make_async_copy(dst, dst, sem).wait()


def _experts_kernel(texp_ref, tnext_ref, nused_ref, x_ref, g_ref, wg_hbm, wu_hbm, wd_hbm, y_ref,
                    wgs, wus, wds, wgb, wub, wdb, wsem, *, layer):
    i = pl.program_id(0)
    nused = nused_ref[0]
    expert = texp_ref[i]

    def weight_copies(e):
        return (pltpu.make_async_copy(wg_hbm.at[layer, e], wgs, wsem.at[0]),
                pltpu.make_async_copy(wu_hbm.at[layer, e], wus, wsem.at[1]),
                pltpu.make_async_copy(wd_hbm.at[layer, e], wds, wsem.at[2]))

    @pl.when(i == 0)
    def _():
        for c in weight_copies(expert):
            c.start()

    fresh = jnp.logical_or(i == 0, expert != texp_ref[jnp.maximum(i - 1, 0)])

    nxt = tnext_ref[i]

    @pl.when(jnp.logical_and(fresh, i < nused))
    def _():
        for c in weight_copies(expert):
            c.wait()
        wgb[...] = (wgs[...] * g_ref[...]).astype(BF16)
        wub[...] = (wus[...] * g_ref[...]).astype(BF16)
        wdb[...] = wds[...].astype(BF16)

    @pl.when(jnp.logical_and(jnp.logical_and(fresh, i < nused), nxt != expert))
    def _():
        for c in weight_copies(nxt):
            c.start()

    @pl.when(i < nused)
    def _():
        x = x_ref[...]
        xn = x.astype(BF16)
        inv = _rms_inv(x)
        gate = _dot(xn, wgb[...]) * inv
        up = _dot(xn, wub[...]) * inv
        act = (gate * _sigmoid(gate)) * up
        y_ref[...] = _dot(act.astype(BF16), wdb[...])

    @pl.when(i >= nused)
    def _():
        y_ref[...] = jnp.zeros_like(y_ref)


def _moe_experts(xs, norm_g, layer, w_gate, w_up, w_down, tile_expert, tile_next, n_used):
    slots, d = xs.shape
    _, ne, _, f = w_gate.shape
    rows = EXPERT_ROWS
    hbm = pl.BlockSpec(memory_space=pl.ANY)
    return pl.pallas_call(
        functools.partial(_experts_kernel, layer=layer),
        out_shape=jax.ShapeDtypeStruct((slots, d), F32),
        grid_spec=pltpu.PrefetchScalarGridSpec(
            num_scalar_prefetch=3,
            grid=(slots // rows,),
            in_specs=[pl.BlockSpec((rows, d), lambda i, *_: (i, 0)),
                      pl.BlockSpec((d, 1), lambda i, *_: (0, 0)), hbm, hbm, hbm],
            out_specs=pl.BlockSpec((rows, d), lambda i, *_: (i, 0)),
            scratch_shapes=[pltpu.VMEM((d, f), F32), pltpu.VMEM((d, f), F32),
                            pltpu.VMEM((f, d), F32),
                            pltpu.VMEM((d, f), BF16), pltpu.VMEM((d, f), BF16),
                            pltpu.VMEM((f, d), BF16),
                            pltpu.SemaphoreType.DMA((3,))],
        ),
        compiler_params=_params("arbitrary"),
        name="moe_experts",
    )(tile_expert, tile_next, n_used, xs, norm_g.reshape(d, 1), w_gate, w_up, w_down)


def _combine_kernel(pos0_ref, pos1_ref, h_ref, meta_ref, y_hbm, o_ref, buf0, buf1, sem, *, rows):
    i = pl.program_id(0)

    def start(tile):
        slot = tile % 2

        def body(g, carry):
            for u in range(SUBLANES):
                tok = tile * rows + g * SUBLANES + u
                _row_copy(y_hbm, pos0_ref[tok], buf0.at[slot], g, u, sem.at[0, slot]).start()
                _row_copy(y_hbm, pos1_ref[tok], buf1.at[slot], g, u, sem.at[1, slot]).start()
            return carry

        lax.fori_loop(0, rows // SUBLANES, body, 0)

    @pl.when(i == 0)
    def _():
        start(0)

    @pl.when(i + 1 < pl.num_programs(0))
    def _():
        start(i + 1)

    slot = i % 2
    _rows_wait(buf0.at[slot], sem.at[0, slot])
    _rows_wait(buf1.at[slot], sem.at[1, slot])
    meta = meta_ref[...]
    w0 = meta[:, META_W:META_W + 1]
    w1 = meta[:, META_W + 1:META_W + 2]
    shape = h_ref.shape
    o_ref[...] = h_ref[...] + w0 * buf0[slot].reshape(shape) + w1 * buf1[slot].reshape(shape)


def _moe_combine(h, meta, y, pos0, pos1):
    t, d = h.shape
    rows = COMBINE_ROWS
    gather_buf = pltpu.VMEM((2, rows // SUBLANES, SUBLANES, d), F32)
    return pl.pallas_call(
        functools.partial(_combine_kernel, rows=rows),
        out_shape=jax.ShapeDtypeStruct((t, d), F32),
        grid_spec=pltpu.PrefetchScalarGridSpec(
            num_scalar_prefetch=2,
            grid=(t // rows,),
            in_specs=[
                pl.BlockSpec((rows, d), lambda i, *_: (i, 0)),
                pl.BlockSpec((rows, LANES), lambda i, *_: (i, 0)),
                pl.BlockSpec(memory_space=pl.ANY),
            ],
            out_specs=pl.BlockSpec((rows, d), lambda i, *_: (i, 0)),
            scratch_shapes=[gather_buf, gather_buf, pltpu.SemaphoreType.DMA((2, 2))],
        ),
        compiler_params=_params("arbitrary"),
        name="moe_combine",
    )(pos0, pos1, h, meta, y)


def _moe_layer(h, layer, norm_g, w_group, b_group, w_router, b_router, w_gate, w_up, w_down):
    t, _ = h.shape
    ne = w_router.shape[1]
    rows = EXPERT_ROWS
    meta, counts = _moe_router(h, norm_g, w_group, b_group, w_router, b_router)
    counts = counts.astype(jnp.int32)
    padded = (counts + rows - 1) // rows * rows
    ends = jnp.cumsum(padded)
    starts = ends - padded
    rec = meta[:, :SUBLANES].T
    expert = rec[META_E:META_E + 2].astype(jnp.int32)
    rank = rec[META_RANK:META_RANK + 2].astype(jnp.int32)
    mine = expert[None] == jnp.arange(ne, dtype=jnp.int32)[:, None, None]
    pos = rank + jnp.sum(jnp.where(mine, starts[:, None, None], 0), axis=0)
    slots = 2 * t + ne * rows
    n_tiles = slots // rows
    n_used = (ends[-1] // rows).astype(jnp.int32)
    tile_start = jnp.arange(n_tiles, dtype=jnp.int32) * rows
    tile_start = jnp.minimum(tile_start, ends[-1] - 1)
    tile_expert = jnp.sum(tile_start[:, None] >= ends[None, :], axis=1, dtype=jnp.int32)
    tile_expert = jnp.minimum(tile_expert, ne - 1)
    tile_next = tile_expert[jnp.minimum(ends[tile_expert] // rows, n_used - 1)]
    pos0 = pos[0]
    pos1 = pos[1]
    pad_ranges = jnp.stack([jnp.append(starts + counts, ends[-1]),
                            jnp.append(ends, slots)], axis=1).reshape(-1).astype(jnp.int32)
    xs = _moe_dispatch(h, pos0, pos1, pad_ranges, slots)
    y = _moe_experts(xs, norm_g, layer, w_gate, w_up, w_down, tile_expert, tile_next,
                     n_used.reshape(1))
    return _moe_combine(h, meta, y, pos0, pos1)


def _swa_proj_kernel(x_ref, wq_ref, wkv_ref, kg_ref, q_ref, k_ref, v_ref):
    x = x_ref[...]
    xb = x.astype(BF16)
    inv = _rms_inv(x)
    q_ref[...] = (_dot(xb, wq_ref[...]) * inv).astype(BF16)
    kv = _dot(xb, wkv_ref[...]) * inv
    half = kv.shape[1] // 2
    v_ref[...] = kv[:, half:].astype(BF16)
    for hd in range(half // LANES):
        sl = slice(hd * LANES, (hd + 1) * LANES)
        k = kv[:, sl]
        k_ref[:, sl] = (k * _rms_inv(k) * kg_ref[...]).astype(BF16)


def _swa_proj(h, q_norm_g, kv_norm_g, w_q, w_kv, k_norm):
    t, d = h.shape
    hd = SWA_HEAD_DIM
    kvh = w_kv.shape[1] // (2 * hd)
    rows = SWA_PROJ_ROWS
    rep = LANES // hd
    wq = (q_norm_g[:, None] * w_q).astype(BF16)
    w_dup = (kv_norm_g[:, None] * w_kv).astype(BF16).reshape(d, 2 * kvh, 1, hd)
    w_dup = jnp.broadcast_to(w_dup, (d, 2 * kvh, rep, hd)).reshape(d, 2 * kvh * LANES)
    kg = jnp.tile(k_norm, rep).reshape(1, LANES)
    const = lambda shape: pl.BlockSpec(shape, lambda i: (0, 0))
    return pl.pallas_call(
        _swa_proj_kernel,
        out_shape=(
            jax.ShapeDtypeStruct((t, w_q.shape[1]), BF16),
            jax.ShapeDtypeStruct((t, kvh * LANES), BF16),
            jax.ShapeDtypeStruct((t, kvh * LANES), BF16),
        ),
        grid=(t // rows,),
        in_specs=[
            pl.BlockSpec((rows, d), lambda i: (i, 0)),
            const((d, w_q.shape[1])), const((d, 2 * kvh * LANES)), const((1, LANES)),
        ],
        out_specs=(
            pl.BlockSpec((rows, w_q.shape[1]), lambda i: (i, 0)),
            pl.BlockSpec((rows, kvh * LANES), lambda i: (i, 0)),
            pl.BlockSpec((rows, kvh * LANES), lambda i: (i, 0)),
        ),
        compiler_params=_params("parallel"),
        name="swa_proj",
    )(h, wq, w_dup, kg)


def _swa_attn_kernel(q_ref, kp_ref, kc_ref, vp_ref, vc_ref, bias_ref, sinkw_ref, qg_ref,
                     blk_ref, ones_ref, o_ref, *, kvh, pairs, window):
    hd = LANES // 2
    lane = lax.broadcasted_iota(jnp.int32, (window, LANES), 1)
    low = lane < hd
    width = 2 * window
    olow = lax.broadcasted_iota(jnp.int32, (pairs * window, LANES), 1) < hd
    zero = jnp.zeros((), BF16)

    def block_diag(prev, cur):
        return jnp.concatenate([jnp.where(low, prev, zero), jnp.where(low, cur, zero),
                                jnp.where(low, zero, prev), jnp.where(low, zero, cur)], axis=0)

    for h in range(kvh):
        ksl = slice(h * LANES, (h + 1) * LANES)
        kk = block_diag(kp_ref[:, ksl], kc_ref[:, ksl])
        vv = block_diag(vp_ref[:, ksl], vc_ref[:, ksl])
        base = h * pairs * LANES
        q2 = jnp.concatenate([q_ref[:, base + p * LANES:base + (p + 1) * LANES]
                              for p in range(pairs)], axis=0).astype(F32)
        ms = _dot((q2 * q2).astype(BF16), blk_ref[...])
        qn = (q2 * lax.rsqrt(ms + RMS_EPS) * qg_ref[...]).astype(BF16)
        s = _dot_nt(qn, kk) + bias_ref[0, h]
        probs = []
        row_max = []
        for half in range(2):
            sh = s[:, half * width:(half + 1) * width]
            m = jnp.max(sh, axis=-1, keepdims=True)
            probs.append(jnp.exp2(sh - m).astype(BF16))
            row_max.append(m)
        mixed = _dot(jnp.concatenate(probs, axis=1), jnp.concatenate([vv, ones_ref[...]], axis=1))
        den = mixed[:, LANES:] + jnp.exp2(sinkw_ref[h] - jnp.where(olow, row_max[0], row_max[1]))
        o2 = mixed[:, :LANES] * (1.0 / den)
        for p in range(pairs):
            o_ref[:, base + p * LANES:base + (p + 1) * LANES] = (
                o2[p * window:(p + 1) * window].astype(BF16))


def _t5_bucket(dist):
    max_exact = NUM_BUCKETS // 2
    n = np.maximum(dist, 0)
    large = max_exact + (np.log(np.maximum(n, max_exact) / max_exact)
                         / math.log(REL_MAX_DISTANCE / max_exact)
                         * (NUM_BUCKETS - max_exact)).astype(np.int32)
    return np.where(n < max_exact, n, np.minimum(large, NUM_BUCKETS - 1)).astype(np.int32)


def _swa_attn(q, kd, vd, rel_bias, q_norm, sinks, *, batch):
    t, dq = q.shape
    kvh = kd.shape[1] // LANES
    hq = dq // SWA_HEAD_DIM
    pairs = hq // kvh // 2
    w = WINDOW
    nblk = t // batch // w
    row = np.arange(w)[:, None]
    col = np.arange(2 * w)[None, :]
    dist = w + row - col
    band = (dist >= 0) & (dist < w)
    onehot = (_t5_bucket(dist)[..., None] == np.arange(NUM_BUCKETS)).astype(np.float32)
    bias = jnp.dot(jnp.asarray(onehot), rel_bias.astype(F32), precision=lax.Precision.HIGHEST)
    bias = jnp.where(jnp.asarray(band)[..., None], bias, -jnp.inf)
    bias = bias.transpose(2, 0, 1).reshape(kvh, pairs, 2, w, 2 * w).transpose(0, 1, 3, 2, 4)
    bias = bias.reshape(kvh, pairs * w, 4 * w) * LOG2_E
    prev_cols = jnp.asarray((np.arange(4 * w) // w) % 2 == 0)
    bias = jnp.stack([bias, jnp.where(prev_cols, -jnp.inf, bias)])
    sink = sinks.astype(F32).reshape(kvh, pairs, 1, 2, 1) * LOG2_E
    sink_wide = jnp.broadcast_to(sink, (kvh, pairs, w, 2, SWA_HEAD_DIM))
    sink_wide = sink_wide.reshape(kvh, pairs * w, LANES)
    qg = jnp.tile(q_norm, LANES // SWA_HEAD_DIM) * (SWA_HEAD_DIM ** -0.5 * LOG2_E)
    lane = np.arange(LANES)
    blk = ((lane[:, None] // SWA_HEAD_DIM) == (lane[None, :] // SWA_HEAD_DIM)) / SWA_HEAD_DIM
    ones = (np.arange(4 * w)[:, None] // (2 * w)) == (lane[None, :] // SWA_HEAD_DIM)
    cur = lambda b, n: (b * nblk + n, 0)
    prev = lambda b, n: (b * nblk + jnp.maximum(n - 1, 0), 0)
    const2 = lambda shape: pl.BlockSpec(shape, lambda b, n: (0, 0))
    const3 = lambda shape: pl.BlockSpec(shape, lambda b, n: (0, 0, 0))
    return pl.pallas_call(
        functools.partial(_swa_attn_kernel, kvh=kvh, pairs=pairs, window=w),
        out_shape=jax.ShapeDtypeStruct((t, dq), BF16),
        grid=(batch, nblk),
        in_specs=[
            pl.BlockSpec((w, dq), cur),
            pl.BlockSpec((w, kvh * LANES), prev), pl.BlockSpec((w, kvh * LANES), cur),
            pl.BlockSpec((w, kvh * LANES), prev), pl.BlockSpec((w, kvh * LANES), cur),
            pl.BlockSpec((1, kvh, pairs * w, 4 * w), lambda b, n: (jnp.where(n == 0, 1, 0), 0, 0, 0)),
            const3((kvh, pairs * w, LANES)),
            const2((1, LANES)), const2((LANES, LANES)), const2((4 * w, LANES)),
        ],
        out_specs=pl.BlockSpec((w, dq), cur),
        compiler_params=_params("parallel", "parallel"),
        name="swa_attn",
    )(q, kd, kd, vd, vd, bias, sink_wide, qg.reshape(1, LANES), jnp.asarray(blk, BF16),
      jnp.asarray(ones, BF16))


def _swa_layer(h, batch, kv_norm, w_kv, k_norm, rel_bias, norm_g, w_q, q_norm, sinks, w_out):
    q, kd, vd = _swa_proj(h, norm_g, kv_norm, w_q, w_kv, k_norm)
    o = _swa_attn(q, kd, vd, rel_bias, q_norm, sinks, batch=batch)
    return _matmul_residual(o, w_out.astype(BF16), h, rows=PROJ_ROWS)


def kernel(x, gla_norm, gla_w_in, gla_w_gate_up, gla_b_gate, gla_head_norm, gla_w_out, kv_norm, w_kv, k_norm, rel_bias, swa_norm, swa_w_q, swa_q_norm, swa_sinks, swa_w_out, moe_norm, moe_w_group, moe_b_group, moe_w_router, moe_b_router, moe_w_gate, moe_w_up, moe_w_down):
    batch, seq, d = x.shape
    assert gla_norm.shape[0] == 1 and swa_norm.shape[0] == 1 and moe_norm.shape[0] == 2
    h = x.reshape(batch * seq, d)
    h = _gla_layer(h, batch, gla_norm[0], gla_w_in[0], gla_w_gate_up[0], gla_b_gate[0],
                   gla_head_norm[0], gla_w_out[0])
    h = _moe_layer(h, 0, moe_norm[0], moe_w_group[0], moe_b_group[0], moe_w_router[0],
                   moe_b_router[0], moe_w_gate, moe_w_up, moe_w_down)
    h = _swa_layer(h, batch, kv_norm, w_kv, k_norm, rel_bias, swa_norm[0], swa_w_q[0],
                   swa_q_norm[0], swa_sinks[0], swa_w_out[0])
    h = _moe_layer(h, 1, moe_norm[1], moe_w_group[1], moe_b_group[1], moe_w_router[1],
                   moe_b_router[1], moe_w_gate, moe_w_up, moe_w_down)
    return h.reshape(batch, seq, d)
```

```python
import functools
import math

import jax
import jax.numpy as jnp
import numpy as np
from jax import lax
from jax.experimental import pallas as pl
from jax.experimental.pallas import tpu as pltpu

F32 = jnp.float32
BF16 = jnp.bfloat16

RMS_EPS = 1e-6
GLA_HEADS = 4
GLA_GATE_RANK = 16
GLA_GATE_NORMALIZER = 16.0
GLA_LOG_GATE_MIN = -1.0
GLA_CHUNK = 64
SWA_HEAD_DIM = 64
SWA_GROUPS = 8
WINDOW = 128
NUM_BUCKETS = 32
REL_MAX_DISTANCE = 128
MOE_GROUPS = 4
MOE_EXPERTS_PER_GROUP = 8

LANES = 128
SUBLANES = 8
VMEM_LIMIT_BYTES = 56 * 1024 * 1024

PREP_ROWS = 256
PROJ_ROWS = 512
ROUTER_ROWS = 512
EXPERT_ROWS = 512
COMBINE_ROWS = 256
SWA_PROJ_ROWS = 256
LOG2_E = math.log2(math.e)
DISPATCH_ROWS = 256


def _params(*semantics):
    return pltpu.CompilerParams(dimension_semantics=semantics, vmem_limit_bytes=VMEM_LIMIT_BYTES)


def _dot(a, b):
    return jnp.dot(a, b, preferred_element_type=F32)


def _dot_nt(a, b):
    return lax.dot_general(a, b, (((1,), (1,)), ((), ())), preferred_element_type=F32)


def _split(x):
    hi = x.astype(BF16)
    lo = (x - hi.astype(F32)).astype(BF16)
    return hi, lo


def _dot3(a, b, dot=_dot):
    ah, al = _split(a)
    bh, bl = _split(b)
    return dot(ah, bh) + dot(al, bh) + dot(ah, bl)


def _rms_inv(x):
    return lax.rsqrt(jnp.mean(x * x, axis=-1, keepdims=True) + RMS_EPS)


def _sigmoid(x):
    return 1.0 / (1.0 + jnp.exp(-x))


def _log_sigmoid(x):
    return jnp.minimum(x, 0.0) - jnp.log1p(jnp.exp(-jnp.abs(x)))


def _norm_matmul_kernel(x_ref, w_ref, o_ref, xb_ref, inv_ref):
    @pl.when(pl.program_id(1) == 0)
    def _():
        x = x_ref[...]
        xb_ref[...] = x.astype(BF16)
        inv_ref[...] = _rms_inv(x)

    o_ref[...] = (_dot(xb_ref[...], w_ref[...]) * inv_ref[...]).astype(o_ref.dtype)


def _norm_matmul(x, g, w, *, rows, cols, out_dtype):
    t, d = x.shape
    n = w.shape[1]
    return pl.pallas_call(
        _norm_matmul_kernel,
        out_shape=jax.ShapeDtypeStruct((t, n), out_dtype),
        grid=(t // rows, n // cols),
        in_specs=[
            pl.BlockSpec((rows, d), lambda i, j: (i, 0)),
            pl.BlockSpec((d, cols), lambda i, j: (0, j)),
        ],
        out_specs=pl.BlockSpec((rows, cols), lambda i, j: (i, j)),
        scratch_shapes=[pltpu.VMEM((rows, d), BF16), pltpu.VMEM((rows, 1), F32)],
        compiler_params=_params("parallel", "arbitrary"),
        name="norm_matmul",
    )(x, (g[:, None] * w).astype(BF16))


def _matmul_residual_kernel(a_ref, w_ref, res_ref, o_ref):
    o_ref[...] = res_ref[...] + _dot(a_ref[...], w_ref[...])


def _matmul_residual(a, w, res, *, rows):
    t, k = a.shape
    n = w.shape[1]
    return pl.pallas_call(
        _matmul_residual_kernel,
        out_shape=jax.ShapeDtypeStruct((t, n), F32),
        grid=(t // rows,),
        in_specs=[
            pl.BlockSpec((rows, k), lambda i: (i, 0)),
            pl.BlockSpec((k, n), lambda i: (0, 0)),
            pl.BlockSpec((rows, n), lambda i: (i, 0)),
        ],
        out_specs=pl.BlockSpec((rows, n), lambda i: (i, 0)),
        compiler_params=_params("parallel"),
        name="matmul_residual",
    )(a, w, res)


def _pair_constants(chunk):
    pair = 2 * chunk
    i = np.arange(pair)[:, None]
    j = np.arange(pair)[None, :]
    tri = (j <= i).astype(np.float32)
    sel = ((i >= chunk) & (j < chunk)).astype(np.float32)
    blk = ((i // chunk) == (j // chunk)).astype(np.float32)
    ones = np.ones((pair, pair), np.float32)
    c_row = np.concatenate([tri, sel], axis=0)
    c_col = np.concatenate([tri.T, sel.T, blk, ones], axis=1)
    return jnp.asarray(c_row, BF16), jnp.asarray(c_col, BF16)


def _gla_prep_kernel(x_ref, g_ref, wq_ref, wkt_ref, wg_ref, wgu_ref, bgr_ref,
                     crow_ref, ccol_ref, qe_ref, qes_ref, ket_ref, klt_ref, ktt_ref, dect_ref,
                     *, scale, pair):
    x = x_ref[...]
    xn = (x * _rms_inv(x) * g_ref[...]).astype(BF16)
    q = _dot(xn, wq_ref[...]) * scale
    kt = _dot_nt(wkt_ref[...], xn)
    glr = _dot(xn, wg_ref[...]).astype(BF16)
    z = _dot(glr, wgu_ref[...]) + bgr_ref[...]
    la = jnp.maximum(_log_sigmoid(z) / GLA_GATE_NORMALIZER, GLA_LOG_GATE_MIN)
    lat = la.T
    crow = crow_ref[...]
    ccol = ccol_ref[...]
    for p in range(x.shape[0] // pair):
        rows = slice(p * pair, (p + 1) * pair)
        hi, lo = _split(la[rows])
        cr = _dot(crow, hi) + _dot(crow, lo)
        b_pair = cr[:pair]
        b_chunk = b_pair - cr[pair:]
        qp = q[rows]
        qe_ref[rows, :] = (qp * jnp.exp(b_chunk)).astype(BF16)
        qes_ref[rows, :] = (qp * jnp.exp(b_pair)).astype(BF16)
        hi, lo = _split(lat[:, rows])
        cc = _dot(hi, ccol) + _dot(lo, ccol)
        bt_pair = cc[:, :pair]
        bt_chunk = bt_pair - cc[:, pair:2 * pair]
        end_chunk = cc[:, 2 * pair:3 * pair]
        end_pair = cc[:, 3 * pair:]
        ktp = kt[:, rows]
        ket_ref[:, rows] = (ktp * jnp.exp(-bt_chunk)).astype(BF16)
        klt_ref[:, rows] = (ktp * jnp.exp(end_chunk - bt_chunk)).astype(BF16)
        ktt_ref[:, rows] = (ktp * jnp.exp(end_pair - bt_pair)).astype(BF16)
        dect_ref[:, rows] = jnp.exp(end_pair)


def _gla_prep(h, norm_g, w_in, w_gate_up, b_gate):
    t, d = h.shape
    hk = d // 2
    rank = w_gate_up.shape[0]
    pair = 2 * GLA_CHUNK
    rows = PREP_ROWS
    dk = hk // GLA_HEADS
    wq = w_in[:, :hk].astype(BF16)
    wkt = w_in[:, hk:2 * hk].T.astype(BF16)
    wg = jnp.pad(w_in[:, 3 * d:], ((0, 0), (0, LANES - rank))).astype(BF16)
    wgu = jnp.pad(w_gate_up, ((0, LANES - rank), (0, 0))).astype(BF16)
    c_row, c_col = _pair_constants(GLA_CHUNK)
    const = lambda shape: pl.BlockSpec(shape, lambda i: (0, 0))
    row_out = pl.BlockSpec((rows, hk), lambda i: (i, 0))
    col_out = pl.BlockSpec((hk, rows), lambda i: (0, i))
    return pl.pallas_call(
        functools.partial(_gla_prep_kernel, scale=dk ** -0.5, pair=pair),
        out_shape=(
            jax.ShapeDtypeStruct((t, hk), BF16),
            jax.ShapeDtypeStruct((t, hk), BF16),
            jax.ShapeDtypeStruct((hk, t), BF16),
            jax.ShapeDtypeStruct((hk, t), BF16),
            jax.ShapeDtypeStruct((hk, t), BF16),
            jax.ShapeDtypeStruct((hk, t), F32),
        ),
        grid=(t // rows,),
        in_specs=[
            pl.BlockSpec((rows, d), lambda i: (i, 0)),
            const((1, d)),
            const((d, hk)),
            const((hk, d)),
            const((d, LANES)),
            const((LANES, hk)),
            const((1, hk)),
            const((2 * pair, pair)),
            const((pair, 4 * pair)),
        ],
        out_specs=(row_out, row_out, col_out, col_out, col_out, col_out),
        compiler_params=_params("parallel"),
        name="gla_prep",
    )(h, norm_g.reshape(1, d), wq, wkt, wg, wgu, b_gate.reshape(1, hk), c_row, c_col)


def _gla_core_kernel(qe_ref, qes_ref, ket_ref, klt_ref, ktt_ref, dect_ref, v_ref, r_ref, hg_ref,
                     o_ref, s_ref, *, heads, chunk):
    @pl.when(pl.program_id(1) == 0)
    def _():
        s_ref[...] = jnp.zeros_like(s_ref)

    pair = 2 * chunk
    dk = qe_ref.shape[1] // heads
    dv = v_ref.shape[1] // heads
    row = lax.broadcasted_iota(jnp.int32, (pair, pair), 0)
    col = lax.broadcasted_iota(jnp.int32, (pair, pair), 1)
    second = row >= chunk
    first_keys = col < chunk
    m_intra = jnp.logical_and(col <= row, jnp.logical_not(jnp.logical_xor(second, col >= chunk)))
    m_cross = jnp.logical_and(second, first_keys)
    for h in range(heads):
        ks = slice(h * dk, (h + 1) * dk)
        vs = slice(h * dv, (h + 1) * dv)
        qe = qe_ref[:, ks]
        v = v_ref[:, vs]
        att = jnp.where(m_intra, _dot(qe, ket_ref[ks, :]),
                        jnp.where(m_cross, _dot(qe, klt_ref[ks, :]), 0.0))
        state = s_ref[h]
        o = _dot(att.astype(BF16), v) + _dot(qes_ref[:, ks], state.astype(BF16))
        dec = dect_ref[ks, :]
        dec = jnp.concatenate([dec] * (dv // pair), axis=1) if dv > pair else dec[:, :dv]
        s_ref[h] = dec * state + _dot(ktt_ref[ks, :], v)
        on = o * _rms_inv(o) * hg_ref[...]
        r = r_ref[:, vs].astype(F32)
        o_ref[:, vs] = (on * (r * _sigmoid(r))).astype(BF16)


def _gla_core(qe, qes, ket, klt, ktt, dect, vr, head_g, *, batch):
    t, hk = qe.shape
    hv = vr.shape[1] // 2
    pair = 2 * GLA_CHUNK
    npair = t // batch // pair
    dk = hk // GLA_HEADS
    dv = hv // GLA_HEADS
    row_k = pl.BlockSpec((pair, hk), lambda b, p: (b * npair + p, 0))
    col_k = pl.BlockSpec((hk, pair), lambda b, p: (0, b * npair + p))
    return pl.pallas_call(
        functools.partial(_gla_core_kernel, heads=GLA_HEADS, chunk=GLA_CHUNK),
        out_shape=jax.ShapeDtypeStruct((t, hv), BF16),
        grid=(batch, npair),
        in_specs=[
            row_k, row_k, col_k, col_k, col_k, col_k,
            pl.BlockSpec((pair, hv), lambda b, p: (b * npair + p, 0)),
            pl.BlockSpec((pair, hv), lambda b, p: (b * npair + p, 1)),
            pl.BlockSpec((1, dv), lambda b, p: (0, 0)),
        ],
        out_specs=pl.BlockSpec((pair, hv), lambda b, p: (b * npair + p, 0)),
        scratch_shapes=[pltpu.VMEM((GLA_HEADS, dk, dv), F32)],
        compiler_params=_params("parallel", "arbitrary"),
        name="gla_core",
    )(qe, qes, ket, klt, ktt, dect, vr, vr, head_g.reshape(1, dv))


def _gla_layer(h, batch, norm_g, w_in, w_gate_up, b_gate, head_g, w_out):
    d = h.shape[1]
    hk = d // 2
    qe, qes, ket, klt, ktt, dect = _gla_prep(h, norm_g, w_in, w_gate_up, b_gate)
    vr = _norm_matmul(h, norm_g, w_in[:, 2 * hk:2 * hk + 2 * d], rows=PROJ_ROWS, cols=d,
                      out_dtype=BF16)
    og = _gla_core(qe, qes, ket, klt, ktt, dect, vr, head_g, batch=batch)
    return _matmul_residual(og, w_out.astype(BF16), h, rows=PROJ_ROWS)


META_E, META_W, META_RANK = 0, 2, 4


def _router_kernel(h_ref, g_ref, w_ref, b_ref, tril_ref, meta_ref, cnt_ref, base_ref,
                   *, groups, per_group):
    @pl.when(pl.program_id(0) == 0)
    def _():
        base_ref[...] = jnp.zeros_like(base_ref)

    x = h_ref[...]
    xn = x * _rms_inv(x) * g_ref[...]
    logits = _dot3(xn, w_ref[...]) + b_ref[...]
    lane = lax.broadcasted_iota(jnp.int32, logits.shape, 1).astype(F32)
    neg = -jnp.inf
    far = float(LANES)

    def first_max(vals):
        m = jnp.max(vals, axis=-1, keepdims=True)
        return m, jnp.min(jnp.where(vals == m, lane, far), axis=-1, keepdims=True)

    gl = jnp.where(lane < groups, logits, neg)
    gmax, gidx = first_max(gl)
    p_group = 1.0 / jnp.sum(jnp.exp(gl - gmax), axis=-1, keepdims=True)
    lo = groups + per_group * gidx
    el = jnp.where(jnp.logical_and(lane >= lo, lane < lo + per_group), logits, neg)
    v1, i1 = first_max(el)
    v2, i2 = first_max(jnp.where(lane == i1, neg, el))
    t = jnp.exp(v2 - v1)
    w1 = p_group / (1.0 + t)
    w2 = p_group * t / (1.0 + t)

    oh1 = lane == i1
    oh2 = lane == i2
    onehot = jnp.where(jnp.logical_or(oh1, oh2), 1.0, 0.0).astype(BF16)
    seen = base_ref[...] + _dot(tril_ref[...], onehot)
    rank1 = jnp.sum(jnp.where(oh1, seen, 0.0), axis=-1, keepdims=True) - 1.0
    rank2 = jnp.sum(jnp.where(oh2, seen, 0.0), axis=-1, keepdims=True) - 1.0
    base_ref[...] = seen[-1:, :]
    cnt_ref[...] = jnp.broadcast_to(seen[-1:, :], cnt_ref.shape)

    rec = jnp.zeros_like(logits)
    for k, val in ((META_E, i1 - groups), (META_E + 1, i2 - groups), (META_W, w1),
                   (META_W + 1, w2), (META_RANK, rank1), (META_RANK + 1, rank2)):
        rec = jnp.where(lane == k, val, rec)
    meta_ref[...] = rec


def _moe_router(h, norm_g, w_group, b_group, w_router, b_router):
    t, d = h.shape
    rows = ROUTER_ROWS
    groups = w_group.shape[1]
    ne = w_router.shape[1]
    pad = LANES - groups - ne
    w = jnp.pad(jnp.concatenate([w_group, w_router], axis=1), ((0, 0), (0, pad)))
    b = jnp.pad(jnp.concatenate([b_group, b_router]), (0, pad)).reshape(1, LANES)
    tril = jnp.asarray(np.tril(np.ones((rows, rows), np.float32)), BF16)
    meta, cnt = pl.pallas_call(
        functools.partial(_router_kernel, groups=groups, per_group=ne // groups),
        out_shape=(jax.ShapeDtypeStruct((t, LANES), F32), jax.ShapeDtypeStruct((8, LANES), F32)),
        grid=(t // rows,),
        in_specs=[
            pl.BlockSpec((rows, d), lambda i: (i, 0)),
            pl.BlockSpec((1, d), lambda i: (0, 0)),
            pl.BlockSpec((d, LANES), lambda i: (0, 0)),
            pl.BlockSpec((1, LANES), lambda i: (0, 0)),
            pl.BlockSpec((rows, rows), lambda i: (0, 0)),
        ],
        out_specs=(pl.BlockSpec((rows, LANES), lambda i: (i, 0)),
                   pl.BlockSpec((8, LANES), lambda i: (0, 0))),
        scratch_shapes=[pltpu.VMEM((1, LANES), F32)],
        compiler_params=_params("arbitrary"),
        name="moe_router",
    )(h, norm_g.reshape(1, d), w, b, tril)
    return meta, cnt[0, groups:groups + ne]


def _dispatch_kernel(pos0_ref, pos1_ref, pad_ref, h_ref, xs_hbm, xbuf, zbuf, sem, zsem, *, rows):
    i = pl.program_id(0)
    steps = pl.num_programs(0) - 1
    slot = i % 2

    def wait_slot(s):
        for _ in range(2):
            pltpu.make_async_copy(xbuf.at[s], xs_hbm.at[pl.ds(0, rows)], sem.at[s]).wait()

    @pl.when(i >= 2)
    def _():
        wait_slot(slot)

    @pl.when(i < steps)
    def _():
        xbuf[slot] = h_ref[...]

        def body(g, carry):
            for u in range(SUBLANES):
                r = g * SUBLANES + u
                tok = i * rows + r
                src = xbuf.at[slot, pl.ds(r, 1)]
                pltpu.make_async_copy(src, xs_hbm.at[pl.ds(pos0_ref[tok], 1)], sem.at[slot]).start()
                pltpu.make_async_copy(src, xs_hbm.at[pl.ds(pos1_ref[tok], 1)], sem.at[slot]).start()
            return carry

        lax.fori_loop(0, rows // SUBLANES, body, 0)

    @pl.when(i == steps)
    def _():
        wait_slot(1 - slot)
        zrows = zbuf.shape[0]
        zbuf[...] = jnp.zeros_like(zbuf)
        nseg = pad_ref.shape[0] // 2 - 1
        sizes = [zrows >> k for k in range(zrows.bit_length()) if zrows >> k >= SUBLANES]

        def pieces(first, end):
            head = (-first) % SUBLANES
            out = [(k < head, pltpu.make_async_copy(zbuf.at[pl.ds(0, 1)],
                                                    xs_hbm.at[pl.ds(first + k, 1)], zsem))
                   for k in range(SUBLANES - 1)]
            off = first + head
            count = end - off
            for s in sizes:
                take = (count & s) != 0
                dst = xs_hbm.at[pl.ds(pl.multiple_of(off, SUBLANES), s)]
                out.append((take, pltpu.make_async_copy(zbuf.at[pl.ds(0, s)], dst, zsem)))
                off = off + jnp.where(take, s, 0)
            return out

        for e in range(nseg):
            ps = pieces(pad_ref[2 * e], pad_ref[2 * e + 1])
            for take, c in ps:
                @pl.when(take)
                def _(c=c):
                    c.start()
            for take, c in ps:
                @pl.when(take)
                def _(c=c):
                    c.wait()

        first = pad_ref[2 * nseg]
        chunks = (pad_ref[2 * nseg + 1] - first) // zrows

        def tail_copy(k):
            dst = xs_hbm.at[pl.ds(pl.multiple_of(first + k * zrows, SUBLANES), zrows)]
            return pltpu.make_async_copy(zbuf, dst, zsem)

        def tail_start(k, carry):
            tail_copy(k).start()
            return carry

        def tail_wait(k, carry):
            tail_copy(k).wait()
            return carry

        lax.fori_loop(0, chunks, tail_start, 0)
        lax.fori_loop(0, chunks, tail_wait, 0)


def _moe_dispatch(h, pos0, pos1, pad_ranges, slots):
    t, d = h.shape
    rows = DISPATCH_ROWS
    hbm = pl.BlockSpec(memory_space=pl.ANY)
    return pl.pallas_call(
        functools.partial(_dispatch_kernel, rows=rows),
        out_shape=jax.ShapeDtypeStruct((slots, d), F32),
        grid_spec=pltpu.PrefetchScalarGridSpec(
            num_scalar_prefetch=3,
            grid=(t // rows + 1,),
            in_specs=[pl.BlockSpec((rows, d), lambda i, *_: (jnp.minimum(i, t // rows - 1), 0))],
            out_specs=hbm,
            scratch_shapes=[pltpu.VMEM((2, rows, d), F32), pltpu.VMEM((EXPERT_ROWS // 2, d), F32),
                            pltpu.SemaphoreType.DMA((2,)), pltpu.SemaphoreType.DMA],
        ),
        compiler_params=_params("arbitrary"),
        name="moe_dispatch",
    )(pos0, pos1, pad_ranges, h)


def _row_copy(src_hbm, row, dst, r_tile, r_sub, sem):
    return pltpu.make_async_copy(src_hbm.at[pl.ds(row, 1)], dst.at[r_tile, pl.ds(r_sub, 1)], sem)


def _rows_wait(dst, sem):
    pltpu.make_async_copy(dst, dst, sem).wait()


def _experts_kernel(texp_ref, tnext_ref, nused_ref, x_ref, g_ref, wg_hbm, wu_hbm, wd_hbm, y_ref,
                    wgs, wus, wds, wgb, wub, wdb, wsem, *, layer):
    i = pl.program_id(0)
    nused = nused_ref[0]
    expert = texp_ref[i]

    def weight_copies(e):
        return (pltpu.make_async_copy(wg_hbm.at[layer, e], wgs, wsem.at[0]),
                pltpu.make_async_copy(wu_hbm.at[layer, e], wus, wsem.at[1]),
                pltpu.make_async_copy(wd_hbm.at[layer, e], wds, wsem.at[2]))

    @pl.when(i == 0)
    def _():
        for c in weight_copies(expert):
            c.start()

    fresh = jnp.logical_or(i == 0, expert != texp_ref[jnp.maximum(i - 1, 0)])

    nxt = tnext_ref[i]

    @pl.when(jnp.logical_and(fresh, i < nused))
    def _():
        for c in weight_copies(expert):
            c.wait()
        wgb[...] = (wgs[...] * g_ref[...]).astype(BF16)
        wub[...] = (wus[...] * g_ref[...]).astype(BF16)
        wdb[...] = wds[...].astype(BF16)

    @pl.when(jnp.logical_and(jnp.logical_and(fresh, i < nused), nxt != expert))
    def _():
        for c in weight_copies(nxt):
            c.start()

    @pl.when(i < nused)
    def _():
        x = x_ref[...]
        xn = x.astype(BF16)
        inv = _rms_inv(x)
        gate = _dot(xn, wgb[...]) * inv
        up = _dot(xn, wub[...]) * inv
        act = (gate * _sigmoid(gate)) * up
        y_ref[...] = _dot(act.astype(BF16), wdb[...])

    @pl.when(i >= nused)
    def _():
        y_ref[...] = jnp.zeros_like(y_ref)


def _moe_experts(xs, norm_g, layer, w_gate, w_up, w_down, tile_expert, tile_next, n_used):
    slots, d = xs.shape
    _, ne, _, f = w_gate.shape
    rows = EXPERT_ROWS
    hbm = pl.BlockSpec(memory_space=pl.ANY)
    return pl.pallas_call(
        functools.partial(_experts_kernel, layer=layer),
        out_shape=jax.ShapeDtypeStruct((slots, d), F32),
        grid_spec=pltpu.PrefetchScalarGridSpec(
            num_scalar_prefetch=3,
            grid=(slots // rows,),
            in_specs=[pl.BlockSpec((rows, d), lambda i, *_: (i, 0)),
                      pl.BlockSpec((d, 1), lambda i, *_: (0, 0)), hbm, hbm, hbm],
            out_specs=pl.BlockSpec((rows, d), lambda i, *_: (i, 0)),
            scratch_shapes=[pltpu.VMEM((d, f), F32), pltpu.VMEM((d, f), F32),
                            pltpu.VMEM((f, d), F32),
                            pltpu.VMEM((d, f), BF16), pltpu.VMEM((d, f), BF16),
                            pltpu.VMEM((f, d), BF16),
                            pltpu.SemaphoreType.DMA((3,))],
        ),
        compiler_params=_params("arbitrary"),
        name="moe_experts",
    )(tile_expert, tile_next, n_used, xs, norm_g.reshape(d, 1), w_gate, w_up, w_down)


def _combine_kernel(pos0_ref, pos1_ref, h_ref, meta_ref, y_hbm, o_ref, buf0, buf1, sem, *, rows):
    i = pl.program_id(0)

    def start(tile):
        slot = tile % 2

        def body(g, carry):
            for u in range(SUBLANES):
                tok = tile * rows + g * SUBLANES + u
                _row_copy(y_hbm, pos0_ref[tok], buf0.at[slot], g, u, sem.at[0, slot]).start()
                _row_copy(y_hbm, pos1_ref[tok], buf1.at[slot], g, u, sem.at[1, slot]).start()
            return carry

        lax.fori_loop(0, rows // SUBLANES, body, 0)

    @pl.when(i == 0)
    def _():
        start(0)

    @pl.when(i + 1 < pl.num_programs(0))
    def _():
        start(i + 1)

    slot = i % 2
    _rows_wait(buf0.at[slot], sem.at[0, slot])
    _rows_wait(buf1.at[slot], sem.at[1, slot])
    meta = meta_ref[...]
    w0 = meta[:, META_W:META_W + 1]
    w1 = meta[:, META_W + 1:META_W + 2]
    shape = h_ref.shape
    o_ref[...] = h_ref[...] + w0 * buf0[slot].reshape(shape) + w1 * buf1[slot].reshape(shape)


def _moe_combine(h, meta, y, pos0, pos1):
    t, d = h.shape
    rows = COMBINE_ROWS
    gather_buf = pltpu.VMEM((2, rows // SUBLANES, SUBLANES, d), F32)
    return pl.pallas_call(
        functools.partial(_combine_kernel, rows=rows),
        out_shape=jax.ShapeDtypeStruct((t, d), F32),
        grid_spec=pltpu.PrefetchScalarGridSpec(
            num_scalar_prefetch=2,
            grid=(t // rows,),
            in_specs=[
                pl.BlockSpec((rows, d), lambda i, *_: (i, 0)),
                pl.BlockSpec((rows, LANES), lambda i, *_: (i, 0)),
                pl.BlockSpec(memory_space=pl.ANY),
            ],
            out_specs=pl.BlockSpec((rows, d), lambda i, *_: (i, 0)),
            scratch_shapes=[gather_buf, gather_buf, pltpu.SemaphoreType.DMA((2, 2))],
        ),
        compiler_params=_params("arbitrary"),
        name="moe_combine",
    )(pos0, pos1, h, meta, y)


def _moe_layer(h, layer, norm_g, w_group, b_group, w_router, b_router, w_gate, w_up, w_down):
    t, _ = h.shape
    ne = w_router.shape[1]
    rows = EXPERT_ROWS
    meta, counts = _moe_router(h, norm_g, w_group, b_group, w_router, b_router)
    counts = counts.astype(jnp.int32)
    padded = (counts + rows - 1) // rows * rows
    ends = jnp.cumsum(padded)
    starts = ends - padded
    rec = meta[:, :SUBLANES].T
    expert = rec[META_E:META_E + 2].astype(jnp.int32)
    rank = rec[META_RANK:META_RANK + 2].astype(jnp.int32)
    mine = expert[None] == jnp.arange(ne, dtype=jnp.int32)[:, None, None]
    pos = rank + jnp.sum(jnp.where(mine, starts[:, None, None], 0), axis=0)
    slots = 2 * t + ne * rows
    n_tiles = slots // rows
    n_used = (ends[-1] // rows).astype(jnp.int32)
    tile_start = jnp.arange(n_tiles, dtype=jnp.int32) * rows
    tile_start = jnp.minimum(tile_start, ends[-1] - 1)
    tile_expert = jnp.sum(tile_start[:, None] >= ends[None, :], axis=1, dtype=jnp.int32)
    tile_expert = jnp.minimum(tile_expert, ne - 1)
    tile_next = tile_expert[jnp.minimum(ends[tile_expert] // rows, n_used - 1)]
    pos0 = pos[0]
    pos1 = pos[1]
    pad_ranges = jnp.stack([jnp.append(starts + counts, ends[-1]),
                            jnp.append(ends, slots)], axis=1).reshape(-1).astype(jnp.int32)
    xs = _moe_dispatch(h, pos0, pos1, pad_ranges, slots)
    y = _moe_experts(xs, norm_g, layer, w_gate, w_up, w_down, tile_expert, tile_next,
                     n_used.reshape(1))
    return _moe_combine(h, meta, y, pos0, pos1)


def _swa_proj_kernel(x_ref, wq_ref, wkv_ref, kg_ref, q_ref, k_ref, v_ref):
    x = x_ref[...]
    xb = x.astype(BF16)
    inv = _rms_inv(x)
    q_ref[...] = (_dot(xb, wq_ref[...]) * inv).astype(BF16)
    kv = _dot(xb, wkv_ref[...]) * inv
    half = kv.shape[1] // 2
    v_ref[...] = kv[:, half:].astype(BF16)
    for hd in range(half // LANES):
        sl = slice(hd * LANES, (hd + 1) * LANES)
        k = kv[:, sl]
        k_ref[:, sl] = (k * _rms_inv(k) * kg_ref[...]).astype(BF16)


def _swa_proj(h, q_norm_g, kv_norm_g, w_q, w_kv, k_norm):
    t, d = h.shape
    hd = SWA_HEAD_DIM
    kvh = w_kv.shape[1] // (2 * hd)
    rows = SWA_PROJ_ROWS
    rep = LANES // hd
    wq = (q_norm_g[:, None] * w_q).astype(BF16)
    w_dup = (kv_norm_g[:, None] * w_kv).astype(BF16).reshape(d, 2 * kvh, 1, hd)
    w_dup = jnp.broadcast_to(w_dup, (d, 2 * kvh, rep, hd)).reshape(d, 2 * kvh * LANES)
    kg = jnp.tile(k_norm, rep).reshape(1, LANES)
    const = lambda shape: pl.BlockSpec(shape, lambda i: (0, 0))
    return pl.pallas_call(
        _swa_proj_kernel,
        out_shape=(
            jax.ShapeDtypeStruct((t, w_q.shape[1]), BF16),
            jax.ShapeDtypeStruct((t, kvh * LANES), BF16),
            jax.ShapeDtypeStruct((t, kvh * LANES), BF16),
        ),
        grid=(t // rows,),
        in_specs=[
            pl.BlockSpec((rows, d), lambda i: (i, 0)),
            const((d, w_q.shape[1])), const((d, 2 * kvh * LANES)), const((1, LANES)),
        ],
        out_specs=(
            pl.BlockSpec((rows, w_q.shape[1]), lambda i: (i, 0)),
            pl.BlockSpec((rows, kvh * LANES), lambda i: (i, 0)),
            pl.BlockSpec((rows, kvh * LANES), lambda i: (i, 0)),
        ),
        compiler_params=_params("parallel"),
        name="swa_proj",
    )(h, wq, w_dup, kg)


def _swa_attn_kernel(q_ref, kp_ref, kc_ref, vp_ref, vc_ref, bias_ref, sinkw_ref, qg_ref,
                     blk_ref, ones_ref, o_ref, *, kvh, pairs, window):
    hd = LANES // 2
    lane = lax.broadcasted_iota(jnp.int32, (window, LANES), 1)
    low = lane < hd
    width = 2 * window
    olow = lax.broadcasted_iota(jnp.int32, (pairs * window, LANES), 1) < hd
    zero = jnp.zeros((), BF16)

    def block_diag(prev, cur):
        return jnp.concatenate([jnp.where(low, prev, zero), jnp.where(low, cur, zero),
                                jnp.where(low, zero, prev), jnp.where(low, zero, cur)], axis=0)

    for h in range(kvh):
        ksl = slice(h * LANES, (h + 1) * LANES)
        kk = block_diag(kp_ref[:, ksl], kc_ref[:, ksl])
        vv = block_diag(vp_ref[:, ksl], vc_ref[:, ksl])
        base = h * pairs * LANES
        q2 = jnp.concatenate([q_ref[:, base + p * LANES:base + (p + 1) * LANES]
                              for p in range(pairs)], axis=0).astype(F32)
        ms = _dot((q2 * q2).astype(BF16), blk_ref[...])
        qn = (q2 * lax.rsqrt(ms + RMS_EPS) * qg_ref[...]).astype(BF16)
        s = _dot_nt(qn, kk) + bias_ref[0, h]
        probs = []
        row_max = []
        for half in range(2):
            sh = s[:, half * width:(half + 1) * width]
            m = jnp.max(sh, axis=-1, keepdims=True)
            probs.append(jnp.exp2(sh - m).astype(BF16))
            row_max.append(m)
        mixed = _dot(jnp.concatenate(probs, axis=1), jnp.concatenate([vv, ones_ref[...]], axis=1))
        den = mixed[:, LANES:] + jnp.exp2(sinkw_ref[h] - jnp.where(olow, row_max[0], row_max[1]))
        o2 = mixed[:, :LANES] * (1.0 / den)
        for p in range(pairs):
            o_ref[:, base + p * LANES:base + (p + 1) * LANES] = (
                o2[p * window:(p + 1) * window].astype(BF16))


def _t5_bucket(dist):
    max_exact = NUM_BUCKETS // 2
    n = np.maximum(dist, 0)
    large = max_exact + (np.log(np.maximum(n, max_exact) / max_exact)
                         / math.log(REL_MAX_DISTANCE / max_exact)
                         * (NUM_BUCKETS - max_exact)).astype(np.int32)
    return np.where(n < max_exact, n, np.minimum(large, NUM_BUCKETS - 1)).astype(np.int32)


def _swa_attn(q, kd, vd, rel_bias, q_norm, sinks, *, batch):
    t, dq = q.shape
    kvh = kd.shape[1] // LANES
    hq = dq // SWA_HEAD_DIM
    pairs = hq // kvh // 2
    w = WINDOW
    nblk = t // batch // w
    row = np.arange(w)[:, None]
    col = np.arange(2 * w)[None, :]
    dist = w + row - col
    band = (dist >= 0) & (dist < w)
    onehot = (_t5_bucket(dist)[..., None] == np.arange(NUM_BUCKETS)).astype(np.float32)
    bias = jnp.dot(jnp.asarray(onehot), rel_bias.astype(F32), precision=lax.Precision.HIGHEST)
    bias = jnp.where(jnp.asarray(band)[..., None], bias, -jnp.inf)
    bias = bias.transpose(2, 0, 1).reshape(kvh, pairs, 2, w, 2 * w).transpose(0, 1, 3, 2, 4)
    bias = bias.reshape(kvh, pairs * w, 4 * w) * LOG2_E
    prev_cols = jnp.asarray((np.arange(4 * w) // w) % 2 == 0)
    bias = jnp.stack([bias, jnp.where(prev_cols, -jnp.inf, bias)])
    sink = sinks.astype(F32).reshape(kvh, pairs, 1, 2, 1) * LOG2_E
    sink_wide = jnp.broadcast_to(sink, (kvh, pairs, w, 2, SWA_HEAD_DIM))
    sink_wide = sink_wide.reshape(kvh, pairs * w, LANES)
    qg = jnp.tile(q_norm, LANES // SWA_HEAD_DIM) * (SWA_HEAD_DIM ** -0.5 * LOG2_E)
    lane = np.arange(LANES)
    blk = ((lane[:, None] // SWA_HEAD_DIM) == (lane[None, :] // SWA_HEAD_DIM)) / SWA_HEAD_DIM
    ones = (np.arange(4 * w)[:, None] // (2 * w)) == (lane[None, :] // SWA_HEAD_DIM)
    cur = lambda b, n: (b * nblk + n, 0)
    prev = lambda b, n: (b * nblk + jnp.maximum(n - 1, 0), 0)
    const2 = lambda shape: pl.BlockSpec(shape, lambda b, n: (0, 0))
    const3 = lambda shape: pl.BlockSpec(shape, lambda b, n: (0, 0, 0))
    return pl.pallas_call(
        functools.partial(_swa_attn_kernel, kvh=kvh, pairs=pairs, window=w),
        out_shape=jax.ShapeDtypeStruct((t, dq), BF16),
        grid=(batch, nblk),
        in_specs=[
            pl.BlockSpec((w, dq), cur),
            pl.BlockSpec((w, kvh * LANES), prev), pl.BlockSpec((w, kvh * LANES), cur),
            pl.BlockSpec((w, kvh * LANES), prev), pl.BlockSpec((w, kvh * LANES), cur),
            pl.BlockSpec((1, kvh, pairs * w, 4 * w), lambda b, n: (jnp.where(n == 0, 1, 0), 0, 0, 0)),
            const3((kvh, pairs * w, LANES)),
            const2((1, LANES)), const2((LANES, LANES)), const2((4 * w, LANES)),
        ],
        out_specs=pl.BlockSpec((w, dq), cur),
        compiler_params=_params("parallel", "parallel"),
        name="swa_attn",
    )(q, kd, kd, vd, vd, bias, sink_wide, qg.reshape(1, LANES), jnp.asarray(blk, BF16),
      jnp.asarray(ones, BF16))


def _swa_layer(h, batch, kv_norm, w_kv, k_norm, rel_bias, norm_g, w_q, q_norm, sinks, w_out):
    q, kd, vd = _swa_proj(h, norm_g, kv_norm, w_q, w_kv, k_norm)
    o = _swa_attn(q, kd, vd, rel_bias, q_norm, sinks, batch=batch)
    return _matmul_residual(o, w_out.astype(BF16), h, rows=PROJ_ROWS)


def kernel(x, gla_norm, gla_w_in, gla_w_gate_up, gla_b_gate, gla_head_norm, gla_w_out, kv_norm, w_kv, k_norm, rel_bias, swa_norm, swa_w_q, swa_q_norm, swa_sinks, swa_w_out, moe_norm, moe_w_group, moe_b_group, moe_w_router, moe_b_router, moe_w_gate, moe_w_up, moe_w_down):
    batch, seq, d = x.shape
    assert gla_norm.shape[0] == 1 and swa_norm.shape[0] == 1 and moe_norm.shape[0] == 2
    h = x.reshape(batch * seq, d)
    h = _gla_layer(h, batch, gla_norm[0], gla_w_in[0], gla_w_gate_up[0], gla_b_gate[0],
                   gla_head_norm[0], gla_w_out[0])
    h = _moe_layer(h, 0, moe_norm[0], moe_w_group[0], moe_b_group[0], moe_w_router[0],
                   moe_b_router[0], moe_w_gate, moe_w_up, moe_w_down)
    h = _swa_layer(h, batch, kv_norm, w_kv, k_norm, rel_bias, swa_norm[0], swa_w_q[0],
                   swa_q_norm[0], swa_sinks[0], swa_w_out[0])
    h = _moe_layer(h, 1, moe_norm[1], moe_w_group[1], moe_b_group[1], moe_w_router[1],
                   moe_b_router[1], moe_w_gate, moe_w_up, moe_w_down)
    return h.reshape(batch, seq, d)
```

```python
import functools
import math

import jax
import jax.numpy as jnp
import numpy as np
from jax import lax
from jax.experimental import pallas as pl
from jax.experimental.pallas import tpu as pltpu

F32 = jnp.float32
BF16 = jnp.bfloat16

RMS_EPS = 1e-6
GLA_HEADS = 4
GLA_GATE_RANK = 16
GLA_GATE_NORMALIZER = 16.0
GLA_LOG_GATE_MIN = -1.0
GLA_CHUNK = 64
SWA_HEAD_DIM = 64
SWA_GROUPS = 8
WINDOW = 128
NUM_BUCKETS = 32
REL_MAX_DISTANCE = 128
MOE_GROUPS = 4
MOE_EXPERTS_PER_GROUP = 8

LANES = 128
SUBLANES = 8
VMEM_LIMIT_BYTES = 56 * 1024 * 1024

PREP_ROWS = 256
PROJ_ROWS = 512
ROUTER_ROWS = 512
EXPERT_ROWS = 512
COMBINE_ROWS = 256
SWA_PROJ_ROWS = 256
LOG2_E = math.log2(math.e)
DISPATCH_ROWS = 256


def _params(*semantics):
    return pltpu.CompilerParams(dimension_semantics=semantics, vmem_limit_bytes=VMEM_LIMIT_BYTES)


def _dot(a, b):
    return jnp.dot(a, b, preferred_element_type=F32)


def _dot_nt(a, b):
    return lax.dot_general(a, b, (((1,), (1,)), ((), ())), preferred_element_type=F32)


def _split(x):
    hi = x.astype(BF16)
    lo = (x - hi.astype(F32)).astype(BF16)
    return hi, lo


def _dot3(a, b, dot=_dot):
    ah, al = _split(a)
    bh, bl = _split(b)
    return dot(ah, bh) + dot(al, bh) + dot(ah, bl)


def _rms_inv(x):
    return lax.rsqrt(jnp.mean(x * x, axis=-1, keepdims=True) + RMS_EPS)


def _sigmoid(x):
    return 1.0 / (1.0 + jnp.exp(-x))


def _log_sigmoid(x):
    return jnp.minimum(x, 0.0) - jnp.log1p(jnp.exp(-jnp.abs(x)))


def _norm_matmul_kernel(x_ref, w_ref, o_ref, xb_ref, inv_ref):
    @pl.when(pl.program_id(1) == 0)
    def _():
        x = x_ref[...]
        xb_ref[...] = x.astype(BF16)
        inv_ref[...] = _rms_inv(x)

    o_ref[...] = (_dot(xb_ref[...], w_ref[...]) * inv_ref[...]).astype(o_ref.dtype)


def _norm_matmul(x, g, w, *, rows, cols, out_dtype):
    t, d = x.shape
    n = w.shape[1]
    return pl.pallas_call(
        _norm_matmul_kernel,
        out_shape=jax.ShapeDtypeStruct((t, n), out_dtype),
        grid=(t // rows, n // cols),
        in_specs=[
            pl.BlockSpec((rows, d), lambda i, j: (i, 0)),
            pl.BlockSpec((d, cols), lambda i, j: (0, j)),
        ],
        out_specs=pl.BlockSpec((rows, cols), lambda i, j: (i, j)),
        scratch_shapes=[pltpu.VMEM((rows, d), BF16), pltpu.VMEM((rows, 1), F32)],
        compiler_params=_params("parallel", "arbitrary"),
        name="norm_matmul",
    )(x, (g[:, None] * w).astype(BF16))


def _matmul_residual_kernel(a_ref, w_ref, res_ref, o_ref):
    o_ref[...] = res_ref[...] + _dot(a_ref[...], w_ref[...])


def _matmul_residual(a, w, res, *, rows):
    t, k = a.shape
    n = w.shape[1]
    return pl.pallas_call(
        _matmul_residual_kernel,
        out_shape=jax.ShapeDtypeStruct((t, n), F32),
        grid=(t // rows,),
        in_specs=[
            pl.BlockSpec((rows, k), lambda i: (i, 0)),
            pl.BlockSpec((k, n), lambda i: (0, 0)),
            pl.BlockSpec((rows, n), lambda i: (i, 0)),
        ],
        out_specs=pl.BlockSpec((rows, n), lambda i: (i, 0)),
        compiler_params=_params("parallel"),
        name="matmul_residual",
    )(a, w, res)


def _pair_constants(chunk):
    pair = 2 * chunk
    i = np.arange(pair)[:, None]
    j = np.arange(pair)[None, :]
    tri = (j <= i).astype(np.float32)
    sel = ((i >= chunk) & (j < chunk)).astype(np.float32)
    blk = ((i // chunk) == (j // chunk)).astype(np.float32)
    ones = np.ones((pair, pair), np.float32)
    c_row = np.concatenate([tri, sel], axis=0)
    c_col = np.concatenate([tri.T, sel.T, blk, ones], axis=1)
    return jnp.asarray(c_row, BF16), jnp.asarray(c_col, BF16)


def _gla_prep_kernel(x_ref, g_ref, wq_ref, wkt_ref, wg_ref, wgu_ref, bgr_ref,
                     crow_ref, ccol_ref, qe_ref, qes_ref, ket_ref, klt_ref, ktt_ref, dect_ref,
                     *, scale, pair):
    x = x_ref[...]
    xn = (x * _rms_inv(x) * g_ref[...]).astype(BF16)
    q = _dot(xn, wq_ref[...]) * scale
    kt = _dot_nt(wkt_ref[...], xn)
    glr = _dot(xn, wg_ref[...]).astype(BF16)
    z = _dot(glr, wgu_ref[...]) + bgr_ref[...]
    la = jnp.maximum(_log_sigmoid(z) / GLA_GATE_NORMALIZER, GLA_LOG_GATE_MIN)
    lat = la.T
    crow = crow_ref[...]
    ccol = ccol_ref[...]
    for p in range(x.shape[0] // pair):
        rows = slice(p * pair, (p + 1) * pair)
        hi, lo = _split(la[rows])
        cr = _dot(crow, hi) + _dot(crow, lo)
        b_pair = cr[:pair]
        b_chunk = b_pair - cr[pair:]
        qp = q[rows]
        qe_ref[rows, :] = (qp * jnp.exp(b_chunk)).astype(BF16)
        qes_ref[rows, :] = (qp * jnp.exp(b_pair)).astype(BF16)
        hi, lo = _split(lat[:, rows])
        cc = _dot(hi, ccol) + _dot(lo, ccol)
        bt_pair = cc[:, :pair]
        bt_chunk = bt_pair - cc[:, pair:2 * pair]
        end_chunk = cc[:, 2 * pair:3 * pair]
        end_pair = cc[:, 3 * pair:]
        ktp = kt[:, rows]
        ket_ref[:, rows] = (ktp * jnp.exp(-bt_chunk)).astype(BF16)
        klt_ref[:, rows] = (ktp * jnp.exp(end_chunk - bt_chunk)).astype(BF16)
        ktt_ref[:, rows] = (ktp * jnp.exp(end_pair - bt_pair)).astype(BF16)
        dect_ref[:, rows] = jnp.exp(end_pair)


def _gla_prep(h, norm_g, w_in, w_gate_up, b_gate):
    t, d = h.shape
    hk = d // 2
    rank = w_gate_up.shape[0]
    pair = 2 * GLA_CHUNK
    rows = PREP_ROWS
    dk = hk // GLA_HEADS
    wq = w_in[:, :hk].astype(BF16)
    wkt = w_in[:, hk:2 * hk].T.astype(BF16)
    wg = jnp.pad(w_in[:, 3 * d:], ((0, 0), (0, LANES - rank))).astype(BF16)
    wgu = jnp.pad(w_gate_up, ((0, LANES - rank), (0, 0))).astype(BF16)
    c_row, c_col = _pair_constants(GLA_CHUNK)
    const = lambda shape: pl.BlockSpec(shape, lambda i: (0, 0))
    row_out = pl.BlockSpec((rows, hk), lambda i: (i, 0))
    col_out = pl.BlockSpec((hk, rows), lambda i: (0, i))
    return pl.pallas_call(
        functools.partial(_gla_prep_kernel, scale=dk ** -0.5, pair=pair),
        out_shape=(
            jax.ShapeDtypeStruct((t, hk), BF16),
            jax.ShapeDtypeStruct((t, hk), BF16),
            jax.ShapeDtypeStruct((hk, t), BF16),
            jax.ShapeDtypeStruct((hk, t), BF16),
            jax.ShapeDtypeStruct((hk, t), BF16),
            jax.ShapeDtypeStruct((hk, t), F32),
        ),
        grid=(t // rows,),
        in_specs=[
            pl.BlockSpec((rows, d), lambda i: (i, 0)),
            const((1, d)),
            const((d, hk)),
            const((hk, d)),
            const((d, LANES)),
            const((LANES, hk)),
            const((1, hk)),
            const((2 * pair, pair)),
            const((pair, 4 * pair)),
        ],
        out_specs=(row_out, row_out, col_out, col_out, col_out, col_out),
        compiler_params=_params("parallel"),
        name="gla_prep",
    )(h, norm_g.reshape(1, d), wq, wkt, wg, wgu, b_gate.reshape(1, hk), c_row, c_col)


def _gla_core_kernel(qe_ref, qes_ref, ket_ref, klt_ref, ktt_ref, dect_ref, v_ref, r_ref, hg_ref,
                     o_ref, s_ref, *, heads, chunk):
    @pl.when(pl.program_id(1) == 0)
    def _():
        s_ref[...] = jnp.zeros_like(s_ref)

    pair = 2 * chunk
    dk = qe_ref.shape[1] // heads
    dv = v_ref.shape[1] // heads
    row = lax.broadcasted_iota(jnp.int32, (pair, pair), 0)
    col = lax.broadcasted_iota(jnp.int32, (pair, pair), 1)
    second = row >= chunk
    first_keys = col < chunk
    m_intra = jnp.logical_and(col <= row, jnp.logical_not(jnp.logical_xor(second, col >= chunk)))
    m_cross = jnp.logical_and(second, first_keys)
    for h in range(heads):
        ks = slice(h * dk, (h + 1) * dk)
        vs = slice(h * dv, (h + 1) * dv)
        qe = qe_ref[:, ks]
        v = v_ref[:, vs]
        att = jnp.where(m_intra, _dot(qe, ket_ref[ks, :]),
                        jnp.where(m_cross, _dot(qe, klt_ref[ks, :]), 0.0))
        state = s_ref[h]
        o = _dot(att.astype(BF16), v) + _dot(qes_ref[:, ks], state.astype(BF16))
        dec = dect_ref[ks, :]
        dec = jnp.concatenate([dec] * (dv // pair), axis=1) if dv > pair else dec[:, :dv]
        s_ref[h] = dec * state + _dot(ktt_ref[ks, :], v)
        on = o * _rms_inv(o) * hg_ref[...]
        r = r_ref[:, vs].astype(F32)
        o_ref[:, vs] = (on * (r * _sigmoid(r))).astype(BF16)


def _gla_core(qe, qes, ket, klt, ktt, dect, vr, head_g, *, batch):
    t, hk = qe.shape
    hv = vr.shape[1] // 2
    pair = 2 * GLA_CHUNK
    npair = t // batch // pair
    dk = hk // GLA_HEADS
    dv = hv // GLA_HEADS
    row_k = pl.BlockSpec((pair, hk), lambda b, p: (b * npair + p, 0))
    col_k = pl.BlockSpec((hk, pair), lambda b, p: (0, b * npair + p))
    return pl.pallas_call(
        functools.partial(_gla_core_kernel, heads=GLA_HEADS, chunk=GLA_CHUNK),
        out_shape=jax.ShapeDtypeStruct((t, hv), BF16),
        grid=(batch, npair),
        in_specs=[
            row_k, row_k, col_k, col_k, col_k, col_k,
            pl.BlockSpec((pair, hv), lambda b, p: (b * npair + p, 0)),
            pl.BlockSpec((pair, hv), lambda b, p: (b * npair + p, 1)),
            pl.BlockSpec((1, dv), lambda b, p: (0, 0)),
        ],
        out_specs=pl.BlockSpec((pair, hv), lambda b, p: (b * npair + p, 0)),
        scratch_shapes=[pltpu.VMEM((GLA_HEADS, dk, dv), F32)],
        compiler_params=_params("parallel", "arbitrary"),
        name="gla_core",
    )(qe, qes, ket, klt, ktt, dect, vr, vr, head_g.reshape(1, dv))


def _gla_layer(h, batch, norm_g, w_in, w_gate_up, b_gate, head_g, w_out):
    d = h.shape[1]
    hk = d // 2
    qe, qes, ket, klt, ktt, dect = _gla_prep(h, norm_g, w_in, w_gate_up, b_gate)
    vr = _norm_matmul(h, norm_g, w_in[:, 2 * hk:2 * hk + 2 * d], rows=PROJ_ROWS, cols=d,
                      out_dtype=BF16)
    og = _gla_core(qe, qes, ket, klt, ktt, dect, vr, head_g, batch=batch)
    return _matmul_residual(og, w_out.astype(BF16), h, rows=PROJ_ROWS)


META_E, META_W, META_RANK = 0, 2, 4


def _router_kernel(h_ref, g_ref, w_ref, b_ref, tril_ref, meta_ref, cnt_ref, base_ref,
                   *, groups, per_group):
    @pl.when(pl.program_id(0) == 0)
    def _():
        base_ref[...] = jnp.zeros_like(base_ref)

    x = h_ref[...]
    xn = x * _rms_inv(x) * g_ref[...]
    logits = _dot3(xn, w_ref[...]) + b_ref[...]
    lane = lax.broadcasted_iota(jnp.int32, logits.shape, 1).astype(F32)
    neg = -jnp.inf
    far = float(LANES)

    def first_max(vals):
        m = jnp.max(vals, axis=-1, keepdims=True)
        return m, jnp.min(jnp.where(vals == m, lane, far), axis=-1, keepdims=True)

    gl = jnp.where(lane < groups, logits, neg)
    gmax, gidx = first_max(gl)
    p_group = 1.0 / jnp.sum(jnp.exp(gl - gmax), axis=-1, keepdims=True)
    lo = groups + per_group * gidx
    el = jnp.where(jnp.logical_and(lane >= lo, lane < lo + per_group), logits, neg)
    v1, i1 = first_max(el)
    v2, i2 = first_max(jnp.where(lane == i1, neg, el))
    t = jnp.exp(v2 - v1)
    w1 = p_group / (1.0 + t)
    w2 = p_group * t / (1.0 + t)

    oh1 = lane == i1
    oh2 = lane == i2
    onehot = jnp.where(jnp.logical_or(oh1, oh2), 1.0, 0.0).astype(BF16)
    seen = base_ref[...] + _dot(tril_ref[...], onehot)
    rank1 = jnp.sum(jnp.where(oh1, seen, 0.0), axis=-1, keepdims=True) - 1.0
    rank2 = jnp.sum(jnp.where(oh2, seen, 0.0), axis=-1, keepdims=True) - 1.0
    base_ref[...] = seen[-1:, :]
    cnt_ref[...] = jnp.broadcast_to(seen[-1:, :], cnt_ref.shape)

    rec = jnp.zeros_like(logits)
    for k, val in ((META_E, i1 - groups), (META_E + 1, i2 - groups), (META_W, w1),
                   (META_W + 1, w2), (META_RANK, rank1), (META_RANK + 1, rank2)):
        rec = jnp.where(lane == k, val, rec)
    meta_ref[...] = rec


def _moe_router(h, norm_g, w_group, b_group, w_router, b_router):
    t, d = h.shape
    rows = ROUTER_ROWS
    groups = w_group.shape[1]
    ne = w_router.shape[1]
    pad = LANES - groups - ne
    w = jnp.pad(jnp.concatenate([w_group, w_router], axis=1), ((0, 0), (0, pad)))
    b = jnp.pad(jnp.concatenate([b_group, b_router]), (0, pad)).reshape(1, LANES)
    tril = jnp.asarray(np.tril(np.ones((rows, rows), np.float32)), BF16)
    meta, cnt = pl.pallas_call(
        functools.partial(_router_kernel, groups=groups, per_group=ne // groups),
        out_shape=(jax.ShapeDtypeStruct((t, LANES), F32), jax.ShapeDtypeStruct((8, LANES), F32)),
        grid=(t // rows,),
        in_specs=[
            pl.BlockSpec((rows, d), lambda i: (i, 0)),
            pl.BlockSpec((1, d), lambda i: (0, 0)),
            pl.BlockSpec((d, LANES), lambda i: (0, 0)),
            pl.BlockSpec((1, LANES), lambda i: (0, 0)),
            pl.BlockSpec((rows, rows), lambda i: (0, 0)),
        ],
        out_specs=(pl.BlockSpec((rows, LANES), lambda i: (i, 0)),
                   pl.BlockSpec((8, LANES), lambda i: (0, 0))),
        scratch_shapes=[pltpu.VMEM((1, LANES), F32)],
        compiler_params=_params("arbitrary"),
        name="moe_router",
    )(h, norm_g.reshape(1, d), w, b, tril)
    return meta, cnt[0, groups:groups + ne]


def _dispatch_kernel(pos0_ref, pos1_ref, pad_ref, h_ref, xs_hbm, xbuf, zbuf, sem, zsem, *, rows):
    i = pl.program_id(0)
    steps = pl.num_programs(0) - 1
    slot = i % 2

    def wait_slot(s):
        for _ in range(2):
            pltpu.make_async_copy(xbuf.at[s], xs_hbm.at[pl.ds(0, rows)], sem.at[s]).wait()

    @pl.when(i >= 2)
    def _():
        wait_slot(slot)

    @pl.when(i < steps)
    def _():
        xbuf[slot] = h_ref[...]

        def body(g, carry):
            for u in range(SUBLANES):
                r = g * SUBLANES + u
                tok = i * rows + r
                src = xbuf.at[slot, pl.ds(r, 1)]
                pltpu.make_async_copy(src, xs_hbm.at[pl.ds(pos0_ref[tok], 1)],
                                      sem.at[slot]).start(priority=0)
                pltpu.make_async_copy(src, xs_hbm.at[pl.ds(pos1_ref[tok], 1)],
                                      sem.at[slot]).start(priority=1)
            return carry

        lax.fori_loop(0, rows // SUBLANES, body, 0)

    @pl.when(i == steps)
    def _():
        wait_slot(1 - slot)
        zrows = zbuf.shape[0]
        zbuf[...] = jnp.zeros_like(zbuf)
        nseg = pad_ref.shape[0] // 2 - 1
        sizes = [zrows >> k for k in range(zrows.bit_length()) if zrows >> k >= SUBLANES]

        def pieces(first, end):
            head = (-first) % SUBLANES
            out = [(k < head, pltpu.make_async_copy(zbuf.at[pl.ds(0, 1)],
                                                    xs_hbm.at[pl.ds(first + k, 1)], zsem))
                   for k in range(SUBLANES - 1)]
            off = first + head
            count = end - off
            for s in sizes:
                take = (count & s) != 0
                dst = xs_hbm.at[pl.ds(pl.multiple_of(off, SUBLANES), s)]
                out.append((take, pltpu.make_async_copy(zbuf.at[pl.ds(0, s)], dst, zsem)))
                off = off + jnp.where(take, s, 0)
            return out

        for e in range(nseg):
            ps = pieces(pad_ref[2 * e], pad_ref[2 * e + 1])
            for take, c in ps:
                @pl.when(take)
                def _(c=c):
                    c.start()
            for take, c in ps:
                @pl.when(take)
                def _(c=c):
                    c.wait()

        first = pad_ref[2 * nseg]
        chunks = (pad_ref[2 * nseg + 1] - first) // zrows

        def tail_copy(k):
            dst = xs_hbm.at[pl.ds(pl.multiple_of(first + k * zrows, SUBLANES), zrows)]
            return pltpu.make_async_copy(zbuf, dst, zsem)

        def tail_start(k, carry):
            tail_copy(k).start()
            return carry

        def tail_wait(k, carry):
            tail_copy(k).wait()
            return carry

        lax.fori_loop(0, chunks, tail_start, 0)
        lax.fori_loop(0, chunks, tail_wait, 0)


def _moe_dispatch(h, pos0, pos1, pad_ranges, slots):
    t, d = h.shape
    rows = DISPATCH_ROWS
    hbm = pl.BlockSpec(memory_space=pl.ANY)
    return pl.pallas_call(
        functools.partial(_dispatch_kernel, rows=rows),
        out_shape=jax.ShapeDtypeStruct((slots, d), F32),
        grid_spec=pltpu.PrefetchScalarGridSpec(
            num_scalar_prefetch=3,
            grid=(t // rows + 1,),
            in_specs=[pl.BlockSpec((rows, d), lambda i, *_: (jnp.minimum(i, t // rows - 1), 0))],
            out_specs=hbm,
            scratch_shapes=[pltpu.VMEM((2, rows, d), F32), pltpu.VMEM((EXPERT_ROWS // 2, d), F32),
                            pltpu.SemaphoreType.DMA((2,)), pltpu.SemaphoreType.DMA],
        ),
        compiler_params=_params("arbitrary"),
        name="moe_dispatch",
    )(pos0, pos1, pad_ranges, h)


def _row_copy(src_hbm, row, dst, r_tile, r_sub, sem):
    return pltpu.make_async_copy(src_hbm.at[pl.ds(row, 1)], dst.at[r_tile, pl.ds(r_sub, 1)], sem)


def _rows_wait(dst, sem):
    pltpu.make_async_copy(dst, dst, sem).wait()


def _experts_kernel(texp_ref, tnext_ref, nused_ref, x_ref, g_ref, wg_hbm, wu_hbm, wd_hbm, y_ref,
                    wgs, wus, wds, wgb, wub, wdb, wsem, *, layer):
    i = pl.program_id(0)
    nused = nused_ref[0]
    expert = texp_ref[i]

    def weight_copies(e):
        return (pltpu.make_async_copy(wg_hbm.at[layer, e], wgs, wsem.at[0]),
                pltpu.make_async_copy(wu_hbm.at[layer, e], wus, wsem.at[1]),
                pltpu.make_async_copy(wd_hbm.at[layer, e], wds, wsem.at[2]))

    @pl.when(i == 0)
    def _():
        for c in weight_copies(expert):
            c.start()

    fresh = jnp.logical_or(i == 0, expert != texp_ref[jnp.maximum(i - 1, 0)])

    nxt = tnext_ref[i]

    @pl.when(jnp.logical_and(fresh, i < nused))
    def _():
        for c in weight_copies(expert):
            c.wait()
        wgb[...] = (wgs[...] * g_ref[...]).astype(BF16)
        wub[...] = (wus[...] * g_ref[...]).astype(BF16)
        wdb[...] = wds[...].astype(BF16)

    @pl.when(jnp.logical_and(jnp.logical_and(fresh, i < nused), nxt != expert))
    def _():
        for c in weight_copies(nxt):
            c.start()

    @pl.when(i < nused)
    def _():
        x = x_ref[...]
        xn = x.astype(BF16)
        inv = _rms_inv(x)
        gate = _dot(xn, wgb[...]) * inv
        up = _dot(xn, wub[...]) * inv
        act = (gate * _sigmoid(gate)) * up
        y_ref[...] = _dot(act.astype(BF16), wdb[...])

    @pl.when(i >= nused)
    def _():
        y_ref[...] = jnp.zeros_like(y_ref)


def _moe_experts(xs, norm_g, layer, w_gate, w_up, w_down, tile_expert, tile_next, n_used):
    slots, d = xs.shape
    _, ne, _, f = w_gate.shape
    rows = EXPERT_ROWS
    hbm = pl.BlockSpec(memory_space=pl.ANY)
    return pl.pallas_call(
        functools.partial(_experts_kernel, layer=layer),
        out_shape=jax.ShapeDtypeStruct((slots, d), F32),
        grid_spec=pltpu.PrefetchScalarGridSpec(
            num_scalar_prefetch=3,
            grid=(slots // rows,),
            in_specs=[pl.BlockSpec((rows, d), lambda i, *_: (i, 0)),
                      pl.BlockSpec((d, 1), lambda i, *_: (0, 0)), hbm, hbm, hbm],
            out_specs=pl.BlockSpec((rows, d), lambda i, *_: (i, 0)),
            scratch_shapes=[pltpu.VMEM((d, f), F32), pltpu.VMEM((d, f), F32),
                            pltpu.VMEM((f, d), F32),
                            pltpu.VMEM((d, f), BF16), pltpu.VMEM((d, f), BF16),
                            pltpu.VMEM((f, d), BF16),
                            pltpu.SemaphoreType.DMA((3,))],
        ),
        compiler_params=_params("arbitrary"),
        name="moe_experts",
    )(tile_expert, tile_next, n_used, xs, norm_g.reshape(d, 1), w_gate, w_up, w_down)


def _combine_kernel(pos0_ref, pos1_ref, h_ref, meta_ref, y_hbm, o_ref, buf0, buf1, sem, *, rows):
    i = pl.program_id(0)

    def start(tile):
        slot = tile % 2

        def body(g, carry):
            for u in range(SUBLANES):
                tok = tile * rows + g * SUBLANES + u
                _row_copy(y_hbm, pos0_ref[tok], buf0.at[slot], g, u, sem.at[0, slot]).start()
                _row_copy(y_hbm, pos1_ref[tok], buf1.at[slot], g, u, sem.at[1, slot]).start()
            return carry

        lax.fori_loop(0, rows // SUBLANES, body, 0)

    @pl.when(i == 0)
    def _():
        start(0)

    @pl.when(i + 1 < pl.num_programs(0))
    def _():
        start(i + 1)

    slot = i % 2
    _rows_wait(buf0.at[slot], sem.at[0, slot])
    _rows_wait(buf1.at[slot], sem.at[1, slot])
    meta = meta_ref[...]
    w0 = meta[:, META_W:META_W + 1]
    w1 = meta[:, META_W + 1:META_W + 2]
    shape = h_ref.shape
    o_ref[...] = h_ref[...] + w0 * buf0[slot].reshape(shape) + w1 * buf1[slot].reshape(shape)


def _moe_combine(h, meta, y, pos0, pos1):
    t, d = h.shape
    rows = COMBINE_ROWS
    gather_buf = pltpu.VMEM((2, rows // SUBLANES, SUBLANES, d), F32)
    return pl.pallas_call(
        functools.partial(_combine_kernel, rows=rows),
        out_shape=jax.ShapeDtypeStruct((t, d), F32),
        grid_spec=pltpu.PrefetchScalarGridSpec(
            num_scalar_prefetch=2,
            grid=(t // rows,),
            in_specs=[
                pl.BlockSpec((rows, d), lambda i, *_: (i, 0)),
                pl.BlockSpec((rows, LANES), lambda i, *_: (i, 0)),
                pl.BlockSpec(memory_space=pl.ANY),
            ],
            out_specs=pl.BlockSpec((rows, d), lambda i, *_: (i, 0)),
            scratch_shapes=[gather_buf, gather_buf, pltpu.SemaphoreType.DMA((2, 2))],
        ),
        compiler_params=_params("arbitrary"),
        name="moe_combine",
    )(pos0, pos1, h, meta, y)


def _moe_layer(h, layer, norm_g, w_group, b_group, w_router, b_router, w_gate, w_up, w_down):
    t, _ = h.shape
    ne = w_router.shape[1]
    rows = EXPERT_ROWS
    meta, counts = _moe_router(h, norm_g, w_group, b_group, w_router, b_router)
    counts = counts.astype(jnp.int32)
    padded = (counts + rows - 1) // rows * rows
    ends = jnp.cumsum(padded)
    starts = ends - padded
    rec = meta[:, :SUBLANES].T
    expert = rec[META_E:META_E + 2].astype(jnp.int32)
    rank = rec[META_RANK:META_RANK + 2].astype(jnp.int32)
    mine = expert[None] == jnp.arange(ne, dtype=jnp.int32)[:, None, None]
    pos = rank + jnp.sum(jnp.where(mine, starts[:, None, None], 0), axis=0)
    slots = 2 * t + ne * rows
    n_tiles = slots // rows
    n_used = (ends[-1] // rows).astype(jnp.int32)
    tile_start = jnp.arange(n_tiles, dtype=jnp.int32) * rows
    tile_start = jnp.minimum(tile_start, ends[-1] - 1)
    tile_expert = jnp.sum(tile_start[:, None] >= ends[None, :], axis=1, dtype=jnp.int32)
    tile_expert = jnp.minimum(tile_expert, ne - 1)
    tile_next = tile_expert[jnp.minimum(ends[tile_expert] // rows, n_used - 1)]
    pos0 = pos[0]
    pos1 = pos[1]
    pad_ranges = jnp.stack([jnp.append(starts + counts, ends[-1]),
                            jnp.append(ends, slots)], axis=1).reshape(-1).astype(jnp.int32)
    xs = _moe_dispatch(h, pos0, pos1, pad_ranges, slots)
    y = _moe_experts(xs, norm_g, layer, w_gate, w_up, w_down, tile_expert, tile_next,
                     n_used.reshape(1))
    return _moe_combine(h, meta, y, pos0, pos1)


def _swa_proj_kernel(x_ref, wq_ref, wkv_ref, kg_ref, q_ref, k_ref, v_ref):
    x = x_ref[...]
    xb = x.astype(BF16)
    inv = _rms_inv(x)
    q_ref[...] = (_dot(xb, wq_ref[...]) * inv).astype(BF16)
    kv = _dot(xb, wkv_ref[...]) * inv
    half = kv.shape[1] // 2
    v_ref[...] = kv[:, half:].astype(BF16)
    for hd in range(half // LANES):
        sl = slice(hd * LANES, (hd + 1) * LANES)
        k = kv[:, sl]
        k_ref[:, sl] = (k * _rms_inv(k) * kg_ref[...]).astype(BF16)


def _swa_proj(h, q_norm_g, kv_norm_g, w_q, w_kv, k_norm):
    t, d = h.shape
    hd = SWA_HEAD_DIM
    kvh = w_kv.shape[1] // (2 * hd)
    rows = SWA_PROJ_ROWS
    rep = LANES // hd
    wq = (q_norm_g[:, None] * w_q).astype(BF16)
    w_dup = (kv_norm_g[:, None] * w_kv).astype(BF16).reshape(d, 2 * kvh, 1, hd)
    w_dup = jnp.broadcast_to(w_dup, (d, 2 * kvh, rep, hd)).reshape(d, 2 * kvh * LANES)
    kg = jnp.tile(k_norm, rep).reshape(1, LANES)
    const = lambda shape: pl.BlockSpec(shape, lambda i: (0, 0))
    return pl.pallas_call(
        _swa_proj_kernel,
        out_shape=(
            jax.ShapeDtypeStruct((t, w_q.shape[1]), BF16),
            jax.ShapeDtypeStruct((t, kvh * LANES), BF16),
            jax.ShapeDtypeStruct((t, kvh * LANES), BF16),
        ),
        grid=(t // rows,),
        in_specs=[
            pl.BlockSpec((rows, d), lambda i: (i, 0)),
            const((d, w_q.shape[1])), const((d, 2 * kvh * LANES)), const((1, LANES)),
        ],
        out_specs=(
            pl.BlockSpec((rows, w_q.shape[1]), lambda i: (i, 0)),
            pl.BlockSpec((rows, kvh * LANES), lambda i: (i, 0)),
            pl.BlockSpec((rows, kvh * LANES), lambda i: (i, 0)),
        ),
        compiler_params=_params("parallel"),
        name="swa_proj",
    )(h, wq, w_dup, kg)


def _swa_attn_kernel(q_ref, kp_ref, kc_ref, vp_ref, vc_ref, bias_ref, sinkw_ref, qg_ref,
                     blk_ref, ones_ref, o_ref, *, kvh, pairs, window):
    hd = LANES // 2
    lane = lax.broadcasted_iota(jnp.int32, (window, LANES), 1)
    low = lane < hd
    width = 2 * window
    olow = lax.broadcasted_iota(jnp.int32, (pairs * window, LANES), 1) < hd
    zero = jnp.zeros((), BF16)

    def block_diag(prev, cur):
        return jnp.concatenate([jnp.where(low, prev, zero), jnp.where(low, cur, zero),
                                jnp.where(low, zero, prev), jnp.where(low, zero, cur)], axis=0)

    for h in range(kvh):
        ksl = slice(h * LANES, (h + 1) * LANES)
        kk = block_diag(kp_ref[:, ksl], kc_ref[:, ksl])
        vv = block_diag(vp_ref[:, ksl], vc_ref[:, ksl])
        base = h * pairs * LANES
        q2 = jnp.concatenate([q_ref[:, base + p * LANES:base + (p + 1) * LANES]
                              for p in range(pairs)], axis=0).astype(F32)
        ms = _dot((q2 * q2).astype(BF16), blk_ref[...])
        qn = (q2 * lax.rsqrt(ms + RMS_EPS) * qg_ref[...]).astype(BF16)
        s = _dot_nt(qn, kk) + bias_ref[0, h]
        probs = []
        row_max = []
        for half in range(2):
            sh = s[:, half * width:(half + 1) * width]
            m = jnp.max(sh, axis=-1, keepdims=True)
            probs.append(jnp.exp2(sh - m).astype(BF16))
            row_max.append(m)
        mixed = _dot(jnp.concatenate(probs, axis=1), jnp.concatenate([vv, ones_ref[...]], axis=1))
        den = mixed[:, LANES:] + jnp.exp2(sinkw_ref[h] - jnp.where(olow, row_max[0], row_max[1]))
        o2 = mixed[:, :LANES] * (1.0 / den)
        for p in range(pairs):
            o_ref[:, base + p * LANES:base + (p + 1) * LANES] = (
                o2[p * window:(p + 1) * window].astype(BF16))


def _t5_bucket(dist):
    max_exact = NUM_BUCKETS // 2
    n = np.maximum(dist, 0)
    large = max_exact + (np.log(np.maximum(n, max_exact) / max_exact)
                         / math.log(REL_MAX_DISTANCE / max_exact)
                         * (NUM_BUCKETS - max_exact)).astype(np.int32)
    return np.where(n < max_exact, n, np.minimum(large, NUM_BUCKETS - 1)).astype(np.int32)


def _swa_attn(q, kd, vd, rel_bias, q_norm, sinks, *, batch):
    t, dq = q.shape
    kvh = kd.shape[1] // LANES
    hq = dq // SWA_HEAD_DIM
    pairs = hq // kvh // 2
    w = WINDOW
    nblk = t // batch // w
    row = np.arange(w)[:, None]
    col = np.arange(2 * w)[None, :]
    dist = w + row - col
    band = (dist >= 0) & (dist < w)
    onehot = (_t5_bucket(dist)[..., None] == np.arange(NUM_BUCKETS)).astype(np.float32)
    bias = jnp.dot(jnp.asarray(onehot), rel_bias.astype(F32), precision=lax.Precision.HIGHEST)
    bias = jnp.where(jnp.asarray(band)[..., None], bias, -jnp.inf)
    bias = bias.transpose(2, 0, 1).reshape(kvh, pairs, 2, w, 2 * w).transpose(0, 1, 3, 2, 4)
    bias = bias.reshape(kvh, pairs * w, 4 * w) * LOG2_E
    prev_cols = jnp.asarray((np.arange(4 * w) // w) % 2 == 0)
    bias = jnp.stack([bias, jnp.where(prev_cols, -jnp.inf, bias)])
    sink = sinks.astype(F32).reshape(kvh, pairs, 1, 2, 1) * LOG2_E
    sink_wide = jnp.broadcast_to(sink, (kvh, pairs, w, 2, SWA_HEAD_DIM))
    sink_wide = sink_wide.reshape(kvh, pairs * w, LANES)
    qg = jnp.tile(q_norm, LANES // SWA_HEAD_DIM) * (SWA_HEAD_DIM ** -0.5 * LOG2_E)
    lane = np.arange(LANES)
    blk = ((lane[:, None] // SWA_HEAD_DIM) == (lane[None, :] // SWA_HEAD_DIM)) / SWA_HEAD_DIM
    ones = (np.arange(4 * w)[:, None] // (2 * w)) == (lane[None, :] // SWA_HEAD_DIM)
    cur = lambda b, n: (b * nblk + n, 0)
    prev = lambda b, n: (b * nblk + jnp.maximum(n - 1, 0), 0)
    const2 = lambda shape: pl.BlockSpec(shape, lambda b, n: (0, 0))
    const3 = lambda shape: pl.BlockSpec(shape, lambda b, n: (0, 0, 0))
    return pl.pallas_call(
        functools.partial(_swa_attn_kernel, kvh=kvh, pairs=pairs, window=w),
        out_shape=jax.ShapeDtypeStruct((t, dq), BF16),
        grid=(batch, nblk),
        in_specs=[
            pl.BlockSpec((w, dq), cur),
            pl.BlockSpec((w, kvh * LANES), prev), pl.BlockSpec((w, kvh * LANES), cur),
            pl.BlockSpec((w, kvh * LANES), prev), pl.BlockSpec((w, kvh * LANES), cur),
            pl.BlockSpec((1, kvh, pairs * w, 4 * w), lambda b, n: (jnp.where(n == 0, 1, 0), 0, 0, 0)),
            const3((kvh, pairs * w, LANES)),
            const2((1, LANES)), const2((LANES, LANES)), const2((4 * w, LANES)),
        ],
        out_specs=pl.BlockSpec((w, dq), cur),
        compiler_params=_params("parallel", "parallel"),
        name="swa_attn",
    )(q, kd, kd, vd, vd, bias, sink_wide, qg.reshape(1, LANES), jnp.asarray(blk, BF16),
      jnp.asarray(ones, BF16))


def _swa_layer(h, batch, kv_norm, w_kv, k_norm, rel_bias, norm_g, w_q, q_norm, sinks, w_out):
    q, kd, vd = _swa_proj(h, norm_g, kv_norm, w_q, w_kv, k_norm)
    o = _swa_attn(q, kd, vd, rel_bias, q_norm, sinks, batch=batch)
    return _matmul_residual(o, w_out.astype(BF16), h, rows=PROJ_ROWS)


def kernel(x, gla_norm, gla_w_in, gla_w_gate_up, gla_b_gate, gla_head_norm, gla_w_out, kv_norm, w_kv, k_norm, rel_bias, swa_norm, swa_w_q, swa_q_norm, swa_sinks, swa_w_out, moe_norm, moe_w_group, moe_b_group, moe_w_router, moe_b_router, moe_w_gate, moe_w_up, moe_w_down):
    batch, seq, d = x.shape
    assert gla_norm.shape[0] == 1 and swa_norm.shape[0] == 1 and moe_norm.shape[0] == 2
    h = x.reshape(batch * seq, d)
    h = _gla_layer(h, batch, gla_norm[0], gla_w_in[0], gla_w_gate_up[0], gla_b_gate[0],
                   gla_head_norm[0], gla_w_out[0])
    h = _moe_layer(h, 0, moe_norm[0], moe_w_group[0], moe_b_group[0], moe_w_router[0],
                   moe_b_router[0], moe_w_gate, moe_w_up, moe_w_down)
    h = _swa_layer(h, batch, kv_norm, w_kv, k_norm, rel_bias, swa_norm[0], swa_w_q[0],
                   swa_q_norm[0], swa_sinks[0], swa_w_out[0])
    h = _moe_layer(h, 1, moe_norm[1], moe_w_group[1], moe_b_group[1], moe_w_router[1],
                   moe_b_router[1], moe_w_gate, moe_w_up, moe_w_down)
    return h.reshape(batch, seq, d)
```
